```python
import math
import jax, jax.numpy as jnp
from jax import lax
import numpy as np

D_MODEL = 2048
BATCH = 1
SEQ = 8192
DEPTH = 4

N_MIXERS = 3
N_HEADS = 32
N_KV_HEADS = 4
HEAD_DIM = 64
QKV_WIDTH = (N_HEADS + 2 * N_KV_HEADS) * HEAD_DIM
WINDOW = 128
BLOCK = 128
PAD_POS = -(1 << 20)
LRU_WIDTH = D_MODEL
N_GATE_BLOCKS = 8
GATE_BLOCK = LRU_WIDTH // N_GATE_BLOCKS
CONV_WIDTH = 4
LRU_C = 8.0
SSM_WIDTH = D_MODEL
SSM_GROUP = 16
N_SSM_GROUPS = SSM_WIDTH // SSM_GROUP
SSM_STATE = 64
DT_MIN = 0.001
DT_MAX = 0.1
D_FF = 4 * D_MODEL
PLE_DIM = 256
EPS = 1e-6

kernel_name = 'hybrid_swa_rglru_s5_trunk'


def _rmsnorm(x, gain):
    xf = x.astype(jnp.float32)
    y = xf * lax.rsqrt(jnp.mean(xf * xf, axis=-1, keepdims=True) + EPS)
    return (y * gain.astype(jnp.float32)).astype(x.dtype)


def _alibi_slopes():
    return jnp.asarray(2.0 ** (-8.0 * np.arange(1, N_HEADS + 1) / N_HEADS), dtype=jnp.float32)


def _band(a, fill):
    B, T = a.shape[0], a.shape[1]
    nb = T // BLOCK
    ab = a.reshape((B, nb, BLOCK) + a.shape[2:])
    prev = jnp.concatenate([jnp.full_like(ab[:, :1], fill), ab[:, :-1]], axis=1)
    return jnp.concatenate([prev, ab], axis=2)


def sliding_window_attention(h, positions, w_qkv, q_norm, k_norm, sinks, w_o):
    B, T, _ = h.shape
    nb = T // BLOCK
    G = N_HEADS // N_KV_HEADS
    qkv = h @ w_qkv
    q, k, v = jnp.split(qkv, [N_HEADS * HEAD_DIM, (N_HEADS + N_KV_HEADS) * HEAD_DIM], axis=-1)
    q = _rmsnorm(q.reshape(B, T, N_KV_HEADS, G, HEAD_DIM), q_norm)
    k = _rmsnorm(k.reshape(B, T, N_KV_HEADS, HEAD_DIM), k_norm)
    v = v.reshape(B, T, N_KV_HEADS, HEAD_DIM)
    qb = q.reshape(B, nb, BLOCK, N_KV_HEADS, G, HEAD_DIM)
    kb = _band(k, 0)
    vb = _band(v, 0)
    kpos = _band(positions, PAD_POS)
    qpos = positions.reshape(B, nb, BLOCK)
    dist = qpos[:, :, :, None] - kpos[:, :, None, :]
    valid = (dist >= 0) & (dist < WINDOW)
    s = jnp.einsum('bnqhgd,bnkhd->bnhgqk', qb, kb).astype(jnp.float32) * (HEAD_DIM ** -0.5)
    slopes = _alibi_slopes().reshape(N_KV_HEADS, G)
    s = s - slopes[None, None, :, :, None, None] * dist[:, :, None, None].astype(jnp.float32)
    s = jnp.where(valid[:, :, None, None], s, -jnp.inf)
    sink = jnp.broadcast_to(sinks.astype(jnp.float32).reshape(1, 1, N_KV_HEADS, G, 1, 1), s.shape[:-1] + (1,))
    prob = jax.nn.softmax(jnp.concatenate([s, sink], axis=-1), axis=-1)[..., :-1]
    o = jnp.einsum('bnhgqk,bnkhd->bnqhgd', prob.astype(v.dtype), vb)
    return o.reshape(B, T, N_HEADS * HEAD_DIM) @ w_o


def _causal_depthwise_conv(x, w, b):
    y = lax.conv_general_dilated(x, w[:, None, :], window_strides=(1,), padding=[(CONV_WIDTH - 1, 0)],
                                 dimension_numbers=('NWC', 'WIO', 'NWC'), feature_group_count=x.shape[-1])
    return y + b


def _linear_scan(a, b):
    def combine(l, r):
        a_l, b_l = l
        a_r, b_r = r
        return a_l * a_r, a_r * b_l + b_r
    _, hs = lax.associative_scan(combine, (a, b), axis=1)
    return hs


def rg_lru_block(h, w_in, conv_w, conv_b, w_gate_a, b_gate_a, w_gate_x, b_gate_x, lru_lambda, w_o):
    B, T, _ = h.shape
    xb, gb = jnp.split(h @ w_in, 2, axis=-1)
    gate = jax.nn.gelu(gb, approximate=True)
    xb = _causal_depthwise_conv(xb, conv_w, conv_b)
    xblk = xb.reshape(B, T, N_GATE_BLOCKS, GATE_BLOCK)
    r = jax.nn.sigmoid(jnp.einsum('btnw,nwv->btnv', xblk, w_gate_a).reshape(B, T, LRU_WIDTH) + b_gate_a)
    i = jax.nn.sigmoid(jnp.einsum('btnw,nwv->btnv', xblk, w_gate_x).reshape(B, T, LRU_WIDTH) + b_gate_x)
    log_a = -LRU_C * r.astype(jnp.float32) * jax.nn.softplus(-lru_lambda.astype(jnp.float32))
    a = jnp.exp(log_a)
    mult = jnp.sqrt(-jnp.expm1(2.0 * log_a))
    hs = _linear_scan(a, mult * (i * xb).astype(jnp.float32))
    return (hs.astype(h.dtype) * gate) @ w_o


def s5_block(h, w_in, a_re, a_im, log_dt, b_re, b_im, c_re, c_im, d_skip, w_glu):
    B, T, _ = h.shape
    f32 = jnp.float32
    u = (h @ w_in).reshape(B, T, N_SSM_GROUPS, SSM_GROUP).astype(f32)
    lr = a_re.astype(f32)
    li = a_im.astype(f32)
    dt = jnp.exp(log_dt.astype(f32))[:, None]
    mag = jnp.exp(lr * dt)
    abar_re = mag * jnp.cos(li * dt)
    abar_im = mag * jnp.sin(li * dt)
    n_re = abar_re - 1.0
    n_im = abar_im
    den = lr * lr + li * li
    z_re = (n_re * lr + n_im * li) / den
    z_im = (n_im * lr - n_re * li) / den
    br = b_re.astype(f32)
    bi = b_im.astype(f32)
    bbar_re = z_re[:, :, None] * br - z_im[:, :, None] * bi
    bbar_im = z_re[:, :, None] * bi + z_im[:, :, None] * br
    bu_re = jnp.einsum('btgi,gpi->btgp', u, bbar_re)
    bu_im = jnp.einsum('btgi,gpi->btgp', u, bbar_im)
    A_re = jnp.broadcast_to(abar_re, bu_re.shape)
    A_im = jnp.broadcast_to(abar_im, bu_im.shape)

    def combine(l, r):
        ar_l, ai_l, sr_l, si_l = l
        ar_r, ai_r, sr_r, si_r = r
        return (ar_r * ar_l - ai_r * ai_l,
                ar_r * ai_l + ai_r * ar_l,
                ar_r * sr_l - ai_r * si_l + sr_r,
                ar_r * si_l + ai_r * sr_l + si_r)

    _, _, s_re, s_im = lax.associative_scan(combine, (A_re, A_im, bu_re, bu_im), axis=1)
    y = (jnp.einsum('btgp,gip->btgi', s_re, c_re.astype(f32))
         - jnp.einsum('btgp,gip->btgi', s_im, c_im.astype(f32))
         + d_skip.astype(f32).reshape(N_SSM_GROUPS, SSM_GROUP) * u)
    g = jax.nn.gelu(y.reshape(B, T, SSM_WIDTH).astype(h.dtype))
    val, gt = jnp.split(g @ w_glu, 2, axis=-1)
    return val * jax.nn.sigmoid(gt)


def _squared_relu_mlp(h, w_up, w_down):
    return jnp.square(jax.nn.relu(h @ w_up)) @ w_down


def setup_inputs(seed: int = 0) -> dict:
    key = jax.random.key(seed)
    keys = iter(jax.random.split(key, 128))
    f32 = jnp.float32

    def dense(shape, fan_in):
        return jax.random.normal(next(keys), shape, f32) * fan_in ** -0.5

    def gain(n):
        return 1.0 + 0.02 * jax.random.normal(next(keys), (n,), f32)

    def small(shape):
        return 0.01 * jax.random.normal(next(keys), shape, f32)

    inp = {}
    inp['x'] = jax.random.normal(next(keys), (BATCH, SEQ, D_MODEL), f32)
    inp['p'] = jax.random.normal(next(keys), (DEPTH, BATCH, SEQ, PLE_DIM), f32)
    inp['positions'] = jnp.broadcast_to(jnp.arange(SEQ, dtype=jnp.int32), (BATCH, SEQ))
    for i in range(DEPTH):
        pre = 'l%d_' % i
        kind = i % N_MIXERS
        inp[pre + 'mix_norm'] = gain(D_MODEL)
        if kind == 0:
            inp[pre + 'w_qkv'] = dense((D_MODEL, QKV_WIDTH), D_MODEL)
            inp[pre + 'q_norm'] = gain(HEAD_DIM)
            inp[pre + 'k_norm'] = gain(HEAD_DIM)
            inp[pre + 'sinks'] = jax.random.normal(next(keys), (N_HEADS,), f32)
            inp[pre + 'w_o'] = dense((N_HEADS * HEAD_DIM, D_MODEL), N_HEADS * HEAD_DIM)
        elif kind == 1:
            inp[pre + 'w_in'] = dense((D_MODEL, 2 * LRU_WIDTH), D_MODEL)
            inp[pre + 'conv_w'] = dense((CONV_WIDTH, LRU_WIDTH), CONV_WIDTH)
            inp[pre + 'conv_b'] = small((LRU_WIDTH,))
            inp[pre + 'w_gate_a'] = dense((N_GATE_BLOCKS, GATE_BLOCK, GATE_BLOCK), GATE_BLOCK)
            inp[pre + 'b_gate_a'] = small((LRU_WIDTH,))
            inp[pre + 'w_gate_x'] = dense((N_GATE_BLOCKS, GATE_BLOCK, GATE_BLOCK), GATE_BLOCK)
            inp[pre + 'b_gate_x'] = small((LRU_WIDTH,))
            a0 = jax.random.uniform(next(keys), (LRU_WIDTH,), f32, 0.9, 0.999)
            inp[pre + 'lru_lambda'] = jnp.log(a0) - jnp.log1p(-a0)
            inp[pre + 'w_o'] = dense((LRU_WIDTH, D_MODEL), LRU_WIDTH)
        else:
            inp[pre + 'w_in'] = dense((D_MODEL, SSM_WIDTH), D_MODEL)
            inp[pre + 'a_re'] = -0.5 + small((N_SSM_GROUPS, SSM_STATE))
            inp[pre + 'a_im'] = jnp.pi * jnp.arange(SSM_STATE, dtype=f32)[None, :] + small((N_SSM_GROUPS, SSM_STATE))
            inp[pre + 'log_dt'] = jax.random.uniform(next(keys), (N_SSM_GROUPS,), f32, math.log(DT_MIN), math.log(DT_MAX))
            inp[pre + 'b_re'] = dense((N_SSM_GROUPS, SSM_STATE, SSM_GROUP), 2 * SSM_GROUP)
            inp[pre + 'b_im'] = dense((N_SSM_GROUPS, SSM_STATE, SSM_GROUP), 2 * SSM_GROUP)
            inp[pre + 'c_re'] = dense((N_SSM_GROUPS, SSM_GROUP, SSM_STATE), SSM_STATE)
            inp[pre + 'c_im'] = dense((N_SSM_GROUPS, SSM_GROUP, SSM_STATE), SSM_STATE)
            inp[pre + 'd_skip'] = jax.random.normal(next(keys), (SSM_WIDTH,), f32)
            inp[pre + 'w_glu'] = dense((SSM_WIDTH, 2 * D_MODEL), SSM_WIDTH)
        inp[pre + 'mlp_norm'] = gain(D_MODEL)
        inp[pre + 'w_up'] = dense((D_MODEL, D_FF), D_MODEL)
        inp[pre + 'w_down'] = dense((D_FF, D_MODEL), D_FF)
        inp[pre + 'ple_norm'] = gain(D_MODEL)
        inp[pre + 'w_ple_gate'] = dense((D_MODEL, D_MODEL), D_MODEL)
        inp[pre + 'w_ple'] = dense((PLE_DIM, D_MODEL), PLE_DIM)
    return inp


def reference(x, p, positions,
              l0_mix_norm, l0_w_qkv, l0_q_norm, l0_k_norm, l0_sinks, l0_w_o,
              l0_mlp_norm, l0_w_up, l0_w_down, l0_ple_norm, l0_w_ple_gate, l0_w_ple,
              l1_mix_norm, l1_w_in, l1_conv_w, l1_conv_b, l1_w_gate_a, l1_b_gate_a, l1_w_gate_x, l1_b_gate_x,
              l1_lru_lambda, l1_w_o,
              l1_mlp_norm, l1_w_up, l1_w_down, l1_ple_norm, l1_w_ple_gate, l1_w_ple,
              l2_mix_norm, l2_w_in, l2_a_re, l2_a_im, l2_log_dt, l2_b_re, l2_b_im, l2_c_re, l2_c_im, l2_d_skip,
              l2_w_glu,
              l2_mlp_norm, l2_w_up, l2_w_down, l2_ple_norm, l2_w_ple_gate, l2_w_ple,
              l3_mix_norm, l3_w_qkv, l3_q_norm, l3_k_norm, l3_sinks, l3_w_o,
              l3_mlp_norm, l3_w_up, l3_w_down, l3_ple_norm, l3_w_ple_gate, l3_w_ple):
    layers = [
        (l0_mix_norm, (l0_w_qkv, l0_q_norm, l0_k_norm, l0_sinks, l0_w_o),
         l0_mlp_norm, l0_w_up, l0_w_down, l0_ple_norm, l0_w_ple_gate, l0_w_ple),
        (l1_mix_norm, (l1_w_in, l1_conv_w, l1_conv_b, l1_w_gate_a, l1_b_gate_a, l1_w_gate_x, l1_b_gate_x,
                       l1_lru_lambda, l1_w_o),
         l1_mlp_norm, l1_w_up, l1_w_down, l1_ple_norm, l1_w_ple_gate, l1_w_ple),
        (l2_mix_norm, (l2_w_in, l2_a_re, l2_a_im, l2_log_dt, l2_b_re, l2_b_im, l2_c_re, l2_c_im, l2_d_skip,
                       l2_w_glu),
         l2_mlp_norm, l2_w_up, l2_w_down, l2_ple_norm, l2_w_ple_gate, l2_w_ple),
        (l3_mix_norm, (l3_w_qkv, l3_q_norm, l3_k_norm, l3_sinks, l3_w_o),
         l3_mlp_norm, l3_w_up, l3_w_down, l3_ple_norm, l3_w_ple_gate, l3_w_ple),
    ]
    h = x
    for i in range(DEPTH):
        mix_norm, mix_params, mlp_norm, w_up, w_down, ple_norm, w_ple_gate, w_ple = layers[i]
        hn = _rmsnorm(h, mix_norm)
        kind = i % N_MIXERS
        if kind == 0:
            mixed = sliding_window_attention(hn, positions, *mix_params)
        elif kind == 1:
            mixed = rg_lru_block(hn, *mix_params)
        else:
            mixed = s5_block(hn, *mix_params)
        h = h + mixed
        h = h + _squared_relu_mlp(_rmsnorm(h, mlp_norm), w_up, w_down)
        h = h + (p[i] @ w_ple) * jax.nn.sigmoid(_rmsnorm(h, ple_norm) @ w_ple_gate)
    return h
```

```python
import functools
import math

import jax
import jax.numpy as jnp
import numpy as np
from jax import lax
from jax.experimental import pallas as pl
from jax.experimental.pallas import tpu as pltpu

F32 = jnp.float32
BF16 = jnp.bfloat16

D_MODEL = 2048
DEPTH = 4
N_HEADS = 32
N_KV_HEADS = 4
HEAD_DIM = 64
Q_PER_KV = N_HEADS // N_KV_HEADS
WINDOW = 128
BLOCK = 128
PAD_POS = -(1 << 20)
N_GATE_BLOCKS = 8
GATE_BLOCK = D_MODEL // N_GATE_BLOCKS
CONV_WIDTH = 4
LRU_C = 8.0
SSM_GROUP = 16
N_SSM_GROUPS = D_MODEL // SSM_GROUP
SSM_STATE = 64
D_FF = 4 * D_MODEL
PLE_DIM = 256
EPS = 1e-6

LANES = 128
MXU_DIM = 256
VMEM_LIMIT_BYTES = 56 * 1024 * 1024
MASK_VALUE = -1e30

S5_CHUNK = 16
S5_GROUPS_PER_TILE = LANES // SSM_GROUP
S5_STATE_COLS = S5_GROUPS_PER_TILE * SSM_STATE
N_LANE_TILES = D_MODEL // LANES


def _params(*semantics):
    return pltpu.CompilerParams(dimension_semantics=semantics,
                                vmem_limit_bytes=VMEM_LIMIT_BYTES)


def _rms(x, gain):
    ms = jnp.mean(x * x, axis=-1, keepdims=True)
    return x * lax.rsqrt(ms + EPS) * gain


def _gelu_tanh(x):
    return 0.5 * x * (1.0 + jnp.tanh(math.sqrt(2.0 / math.pi) * (x + 0.044715 * (x * x * x))))


def _sigmoid(x):
    return 1.0 / (1.0 + jnp.exp(-x))


def _dot(a, b):
    return jnp.dot(a, b, preferred_element_type=F32)


def _dot_nt(a, b):
    return lax.dot_general(a, b, (((1,), (1,)), ((), ())), preferred_element_type=F32)


def _fused_matmul_kernel(*refs, n_w, n_extra, n_out, has_norm, epilogue):
    pos = 1
    lhs_ref = refs[0]
    gain_ref = None
    if has_norm:
        gain_ref = refs[1]
        pos = 2
    w_refs = refs[pos:pos + n_w]
    pos += n_w
    extra_refs = refs[pos:pos + n_extra]
    pos += n_extra
    out_refs = refs[pos:pos + n_out]
    pos += n_out
    j = pl.program_id(1)
    if has_norm:
        hn_ref = refs[pos]

        @pl.when(j == 0)
        def _():
            hn_ref[...] = _rms(lhs_ref[...], gain_ref[...]).astype(BF16)

        a = hn_ref[...]
    else:
        a = lhs_ref[...]
    accs = [_dot(a, w[...]) for w in w_refs]
    epilogue(j, accs, extra_refs, out_refs)


def _fused_matmul(grid, lhs, lhs_spec, tm, weights, extras, outs, epilogue, gain=None):
    has_norm = gain is not None
    k_dim = weights[0][0].shape[0]
    args = [lhs]
    in_specs = [lhs_spec]
    if has_norm:
        args.append(gain.reshape(1, k_dim).astype(F32))
        in_specs.append(pl.BlockSpec((1, k_dim), lambda i, j: (0, 0)))
    for a, s in list(weights) + list(extras):
        args.append(a)
        in_specs.append(s)
    kern = functools.partial(_fused_matmul_kernel, n_w=len(weights), n_extra=len(extras),
                             n_out=len(outs), has_norm=has_norm, epilogue=epilogue)
    scratch = [pltpu.VMEM((tm, k_dim), BF16)] if has_norm else []
    res = pl.pallas_call(
        kern,
        grid=grid,
        in_specs=in_specs,
        out_specs=[s for _, s in outs],
        out_shape=[o for o, _ in outs],
        scratch_shapes=scratch,
        compiler_params=_params("parallel", "arbitrary"),
    )(*args)
    return res


def _row_block(t):
    return min(1024, t)


QKV_TN = 512
QKV_COLS = D_MODEL + 2 * N_KV_HEADS * LANES
QKV_NORM_BLOCKS = (D_MODEL + N_KV_HEADS * LANES) // QKV_TN


def _qkv_epilogue(j, accs, extras, outs):
    gain_ref, seg_ref = extras
    (o_ref,) = outs
    x = accs[0]

    @pl.when(j < QKV_NORM_BLOCKS)
    def _():
        x2 = (x * x).astype(BF16)
        ss = jnp.concatenate(
            [_dot(x2[:, c * MXU_DIM:(c + 1) * MXU_DIM], seg_ref[...])
             for c in range(QKV_TN // MXU_DIM)], axis=1)
        r = lax.rsqrt(ss * (1.0 / HEAD_DIM) + EPS)
        o_ref[...] = (x * r * gain_ref[...]).astype(BF16)

    @pl.when(j >= QKV_NORM_BLOCKS)
    def _():
        o_ref[...] = x.astype(BF16)


def _qkv_proj(h, mix_norm, w_prep, head_gain, seg_ones):
    t = h.shape[0]
    tm = _row_block(t)
    grid = (t // tm, QKV_COLS // QKV_TN)
    (qkv,) = _fused_matmul(
        grid, h, pl.BlockSpec((tm, D_MODEL), lambda i, j: (i, 0)), tm,
        weights=[(w_prep, pl.BlockSpec((D_MODEL, QKV_TN), lambda i, j: (0, j)))],
        extras=[(head_gain, pl.BlockSpec((1, QKV_TN), lambda i, j: (0, j))),
                (seg_ones, pl.BlockSpec((MXU_DIM, MXU_DIM), lambda i, j: (0, 0)))],
        outs=[(jax.ShapeDtypeStruct((t, QKV_COLS), BF16),
               pl.BlockSpec((tm, QKV_TN), lambda i, j: (i, j)))],
        epilogue=_qkv_epilogue, gain=mix_norm)
    return qkv


def _alibi_slopes():
    return [float(2.0 ** (-8.0 * (i + 1) / N_HEADS)) for i in range(N_HEADS)]


def _attn_kernel(q_ref, kc_ref, kp_ref, vc_ref, vp_ref, qpos_ref, kposc_ref, kposp_ref,
                 sinks_ref, o_ref):
    first = pl.program_id(0) == 0
    qpos = qpos_ref[...]
    kpos_prev = jnp.where(first, PAD_POS, kposp_ref[...])
    kpos = jnp.concatenate([kpos_prev, kposc_ref[...]], axis=1)
    dist = qpos - kpos
    valid = (dist >= 0) & (dist < WINDOW)
    distf = dist.astype(F32)
    lane = lax.broadcasted_iota(jnp.int32, (BLOCK, LANES), 1)
    lo_f = (lane < HEAD_DIM).astype(F32)
    lo = lo_f.astype(BF16)
    hi = (1.0 - lo_f).astype(BF16)
    lo2 = jnp.concatenate([lo, lo], axis=0)
    hi2 = jnp.concatenate([hi, hi], axis=0)
    keep_prev = (jnp.ones((BLOCK, LANES), F32) * jnp.where(first, 0.0, 1.0)).astype(BF16)
    slopes = _alibi_slopes()
    for h in range(N_KV_HEADS):
        ksl = slice(h * LANES, (h + 1) * LANES)
        kk = jnp.concatenate([kp_ref[:, ksl] * keep_prev, kc_ref[:, ksl]], axis=0)
        vd = jnp.concatenate([vp_ref[:, ksl] * keep_prev, vc_ref[:, ksl]], axis=0)
        qs = []
        for g in range(Q_PER_KV):
            col = h * Q_PER_KV * HEAD_DIM + (g // 2) * LANES
            qp = q_ref[:, col:col + LANES]
            qs.append(qp * (lo if g % 2 == 0 else hi))
        s_all = _dot_nt(jnp.concatenate(qs, axis=0), kk)
        ps = []
        rls = []
        for g in range(Q_PER_KV):
            head = h * Q_PER_KV + g
            sg = s_all[g * BLOCK:(g + 1) * BLOCK] - slopes[head] * distf
            sg = jnp.where(valid, sg, MASK_VALUE)
            sink = sinks_ref[head]
            m = jnp.maximum(jnp.max(sg, axis=-1, keepdims=True), sink)
            p = jnp.exp(sg - m)
            denom = jnp.sum(p, axis=-1, keepdims=True) + jnp.exp(sink - m)
            ps.append(p.astype(BF16))
            rls.append(1.0 / denom)
        lhs2 = jnp.concatenate(
            [jnp.concatenate([ps[2 * p], ps[2 * p + 1]], axis=1) for p in range(Q_PER_KV // 2)],
            axis=0)
        rhs2 = jnp.concatenate([vd * lo2, vd * hi2], axis=0)
        o2 = _dot(lhs2, rhs2)
        for p in range(Q_PER_KV // 2):
            scale = jnp.where(lane < HEAD_DIM, rls[2 * p], rls[2 * p + 1])
            col = h * Q_PER_KV * HEAD_DIM + p * LANES
            o_ref[:, col:col + LANES] = (o2[p * BLOCK:(p + 1) * BLOCK] * scale).astype(BF16)


def _attn_core(qkv, positions, sinks):
    t = qkv.shape[0]
    nb = t // BLOCK
    pos_col = positions.reshape(t, 1)
    pos_row = positions.reshape(nb, 1, BLOCK)
    k_blk = D_MODEL // QKV_TN
    v_blk = k_blk + 1
    prev = lambda n: jnp.maximum(n - 1, 0)
    return pl.pallas_call(
        _attn_kernel,
        grid=(nb,),
        in_specs=[
            pl.BlockSpec((BLOCK, D_MODEL), lambda n: (n, 0)),
            pl.BlockSpec((BLOCK, QKV_TN), lambda n: (n, k_blk)),
            pl.BlockSpec((BLOCK, QKV_TN), lambda n: (prev(n), k_blk)),
            pl.BlockSpec((BLOCK, QKV_TN), lambda n: (n, v_blk)),
            pl.BlockSpec((BLOCK, QKV_TN), lambda n: (prev(n), v_blk)),
            pl.BlockSpec((BLOCK, 1), lambda n: (n, 0)),
            pl.BlockSpec((None, 1, BLOCK), lambda n: (n, 0, 0)),
            pl.BlockSpec((None, 1, BLOCK), lambda n: (prev(n), 0, 0)),
            pl.BlockSpec(memory_space=pltpu.SMEM),
        ],
        out_specs=pl.BlockSpec((BLOCK, D_MODEL), lambda n: (n, 0)),
        out_shape=jax.ShapeDtypeStruct((t, D_MODEL), BF16),
        compiler_params=_params("arbitrary"),
    )(qkv, qkv, qkv, qkv, qkv, pos_col, pos_row, pos_row, sinks.astype(F32))


def _residual_epilogue(j, accs, extras, outs):
    outs[0][...] = extras[0][...] + accs[0]


def _proj_residual(a, w, h):
    t, k_dim = a.shape
    tm = _row_block(t)
    tn = 1024
    (out,) = _fused_matmul(
        (t // tm, D_MODEL // tn), a, pl.BlockSpec((tm, k_dim), lambda i, j: (i, 0)), tm,
        weights=[(w, pl.BlockSpec((k_dim, tn), lambda i, j: (0, j)))],
        extras=[(h, pl.BlockSpec((tm, tn), lambda i, j: (i, j)))],
        outs=[(jax.ShapeDtypeStruct((t, D_MODEL), F32),
               pl.BlockSpec((tm, tn), lambda i, j: (i, j)))],
        epilogue=_residual_epilogue)
    return out


def _prep_qkv_weight(w_qkv):
    wq = w_qkv[:, :D_MODEL]
    wk = w_qkv[:, D_MODEL:D_MODEL + N_KV_HEADS * HEAD_DIM].reshape(D_MODEL, N_KV_HEADS, HEAD_DIM)
    wv = w_qkv[:, D_MODEL + N_KV_HEADS * HEAD_DIM:].reshape(D_MODEL, N_KV_HEADS, HEAD_DIM)
    dup = lambda w: jnp.concatenate([w, w], axis=-1).reshape(D_MODEL, N_KV_HEADS * LANES)
    return jnp.concatenate([wq, dup(wk), dup(wv)], axis=1).astype(BF16)


def _attention_layer(h, positions, mix_norm, w_qkv, q_norm, k_norm, sinks, w_o):
    head_gain = jnp.concatenate([
        jnp.tile(q_norm.astype(F32), N_HEADS) * (HEAD_DIM ** -0.5),
        jnp.tile(k_norm.astype(F32), 2 * N_KV_HEADS),
        jnp.ones((N_KV_HEADS * LANES,), F32)]).reshape(1, QKV_COLS)
    seg_ones = jnp.kron(jnp.eye(MXU_DIM // HEAD_DIM, dtype=F32),
                        jnp.ones((HEAD_DIM, HEAD_DIM), F32)).astype(BF16)
    qkv = _qkv_proj(h, mix_norm, _prep_qkv_weight(w_qkv), head_gain, seg_ones)
    o = _attn_core(qkv, positions, sinks)
    return _proj_residual(o, w_o.astype(BF16), h)


def _mlp_kernel(h_ref, gain_ref, wu_ref, wd_ref, o_ref, hn_ref):
    j = pl.program_id(1)

    @pl.when(j == 0)
    def _():
        x = h_ref[...]
        hn_ref[...] = _rms(x, gain_ref[...]).astype(BF16)
        o_ref[...] = x

    act = jnp.maximum(_dot(hn_ref[...], wu_ref[...]), 0.0)
    act = (act * act).astype(BF16)
    o_ref[...] += _dot(act, wd_ref[...])


def _mlp(h, gain, w_up, w_down):
    t = h.shape[0]
    tm = _row_block(t)
    tf = 512
    return pl.pallas_call(
        _mlp_kernel,
        grid=(t // tm, D_FF // tf),
        in_specs=[
            pl.BlockSpec((tm, D_MODEL), lambda i, j: (i, 0)),
            pl.BlockSpec((1, D_MODEL), lambda i, j: (0, 0)),
            pl.BlockSpec((D_MODEL, tf), lambda i, j: (0, j)),
            pl.BlockSpec((tf, D_MODEL), lambda i, j: (j, 0)),
        ],
        out_specs=pl.BlockSpec((tm, D_MODEL), lambda i, j: (i, 0)),
        out_shape=jax.ShapeDtypeStruct((t, D_MODEL), F32),
        scratch_shapes=[pltpu.VMEM((tm, D_MODEL), BF16)],
        compiler_params=_params("parallel", "arbitrary"),
    )(h, gain.reshape(1, D_MODEL).astype(F32), w_up, w_down)


def _ple_epilogue(j, accs, extras, outs):
    hres_ref, p_ref, wple_ref = extras
    emb = _dot(p_ref[...].astype(BF16), wple_ref[...])
    outs[0][...] = hres_ref[...] + emb * _sigmoid(accs[0])


def _ple(h, gain, p_i, w_ple, w_gate):
    t = h.shape[0]
    tm = _row_block(t)
    tn = 1024
    (out,) = _fused_matmul(
        (t // tm, D_MODEL // tn), h, pl.BlockSpec((tm, D_MODEL), lambda i, j: (i, 0)), tm,
        weights=[(w_gate, pl.BlockSpec((D_MODEL, tn), lambda i, j: (0, j)))],
        extras=[(h, pl.BlockSpec((tm, tn), lambda i, j: (i, j))),
                (p_i, pl.BlockSpec((tm, PLE_DIM), lambda i, j: (i, 0))),
                (w_ple, pl.BlockSpec((PLE_DIM, tn), lambda i, j: (0, j)))],
        outs=[(jax.ShapeDtypeStruct((t, D_MODEL), F32),
               pl.BlockSpec((tm, tn), lambda i, j: (i, j)))],
        epilogue=_ple_epilogue, gain=gain)
    return out


def _rglru_in_epilogue(j, accs, extras, outs):
    outs[0][...] = accs[0]
    outs[1][...] = _gelu_tanh(accs[1]).astype(BF16)


def _rglru_in(h, mix_norm, w_in):
    t = h.shape[0]
    tm = _row_block(t)
    tn = 512
    nj = D_MODEL // tn
    return _fused_matmul(
        (t // tm, nj), h, pl.BlockSpec((tm, D_MODEL), lambda i, j: (i, 0)), tm,
        weights=[(w_in, pl.BlockSpec((D_MODEL, tn), lambda i, j: (0, j))),
                 (w_in, pl.BlockSpec((D_MODEL, tn), lambda i, j: (0, j + nj)))],
        extras=[],
        outs=[(jax.ShapeDtypeStruct((t, D_MODEL), F32), pl.BlockSpec((tm, tn), lambda i, j: (i, j))),
              (jax.ShapeDtypeStruct((t, D_MODEL), BF16), pl.BlockSpec((tm, tn), lambda i, j: (i, j)))],
        epilogue=_rglru_in_epilogue, gain=mix_norm)


RGLRU_TB = 256
HALO = 8


def _rglru_scan_kernel(xb_ref, gate_ref, cw_ref, cb_ref, wa_ref, ba_ref, wx_ref, bx_ref, lam_ref,
                       y_ref, xs_ref, a_ref, b_ref, carry_ref):
    tb = RGLRU_TB

    @pl.when(pl.program_id(0) == 0)
    def _():
        xs_ref[0:HALO, :] = jnp.zeros((HALO, D_MODEL), F32)
        carry_ref[...] = jnp.zeros((1, D_MODEL), F32)

    xs_ref[HALO:HALO + tb, :] = xb_ref[...]
    xc = cb_ref[...] + cw_ref[CONV_WIDTH - 1:CONV_WIDTH, :] * xs_ref[HALO:HALO + tb, :]
    for k in range(CONV_WIDTH - 1):
        off = HALO - (CONV_WIDTH - 1) + k
        xc = xc + cw_ref[k:k + 1, :] * xs_ref[off:off + tb, :]
    xs_ref[0:HALO, :] = xs_ref[tb:tb + HALO, :]

    sp = jnp.log(1.0 + jnp.exp(-jnp.abs(lam_ref[...]))) + jnp.maximum(-lam_ref[...], 0.0)
    for n in range(N_GATE_BLOCKS):
        sl = slice(n * GATE_BLOCK, (n + 1) * GATE_BLOCK)
        xn = xc[:, sl]
        xn16 = xn.astype(BF16)
        r = _sigmoid(_dot(xn16, wa_ref[n]) + ba_ref[:, sl])
        i_gate = _sigmoid(_dot(xn16, wx_ref[n]) + bx_ref[:, sl])
        log_a = (-LRU_C) * r * sp[:, sl]
        a = jnp.exp(log_a)
        mult = jnp.sqrt(1.0 - jnp.exp(2.0 * log_a))
        a_ref[:, sl] = a
        b_ref[:, sl] = mult * (i_gate * xn)

    def step(t, hprev):
        hnew = a_ref[pl.ds(t, 1), :] * hprev + b_ref[pl.ds(t, 1), :]
        b_ref[pl.ds(t, 1), :] = hnew
        return hnew

    carry_ref[...] = lax.fori_loop(0, tb, step, carry_ref[...], unroll=8)
    y_ref[...] = (b_ref[...] * gate_ref[...].astype(F32)).astype(BF16)


def _rglru_scan(xb, gate, conv_w, conv_b, w_gate_a, b_gate_a, w_gate_x, b_gate_x, lru_lambda):
    t = xb.shape[0]
    tb = RGLRU_TB
    row = lambda v: v.reshape(1, D_MODEL).astype(F32)
    full2 = lambda shape: pl.BlockSpec(shape, lambda i: (0, 0))
    full3 = lambda shape: pl.BlockSpec(shape, lambda i: (0, 0, 0))
    return pl.pallas_call(
        _rglru_scan_kernel,
        grid=(t // tb,),
        in_specs=[
            pl.BlockSpec((tb, D_MODEL), lambda i: (i, 0)),
            pl.BlockSpec((tb, D_MODEL), lambda i: (i, 0)),
            full2((CONV_WIDTH, D_MODEL)), full2((1, D_MODEL)),
            full3((N_GATE_BLOCKS, GATE_BLOCK, GATE_BLOCK)), full2((1, D_MODEL)),
            full3((N_GATE_BLOCKS, GATE_BLOCK, GATE_BLOCK)), full2((1, D_MODEL)),
            full2((1, D_MODEL)),
        ],
        out_specs=pl.BlockSpec((tb, D_MODEL), lambda i: (i, 0)),
        out_shape=jax.ShapeDtypeStruct((t, D_MODEL), BF16),
        scratch_shapes=[pltpu.VMEM((tb + HALO, D_MODEL), F32), pltpu.VMEM((tb, D_MODEL), F32),
                        pltpu.VMEM((tb, D_MODEL), F32), pltpu.VMEM((1, D_MODEL), F32)],
        compiler_params=_params("arbitrary"),
    )(xb, gate, conv_w.astype(F32), row(conv_b), w_gate_a.astype(BF16), row(b_gate_a),
      w_gate_x.astype(BF16), row(b_gate_x), row(lru_lambda))


def _rglru_layer(h, mix_norm, w_in, conv_w, conv_b, w_gate_a, b_gate_a, w_gate_x, b_gate_x,
                 lru_lambda, w_o):
    xb, gate = _rglru_in(h, mix_norm, w_in.astype(BF16))
    y = _rglru_scan(xb, gate, conv_w, conv_b, w_gate_a, b_gate_a, w_gate_x, b_gate_x, lru_lambda)
    return _proj_residual(y, w_o.astype(BF16), h)


def _s5_in_epilogue(j, accs, extras, outs):
    outs[0][...] = accs[0]


def _s5_in(h, mix_norm, w_in):
    t = h.shape[0]
    nc = t // S5_CHUNK
    tn = 1024
    hv = h.reshape(nc, S5_CHUNK * D_MODEL)
    (u,) = _fused_matmul(
        (S5_CHUNK, D_MODEL // tn), hv, pl.BlockSpec((nc, D_MODEL), lambda l, j: (0, l)), nc,
        weights=[(w_in, pl.BlockSpec((D_MODEL, tn), lambda l, j: (0, j)))],
        extras=[],
        outs=[(jax.ShapeDtypeStruct((S5_CHUNK, nc, D_MODEL), F32),
               pl.BlockSpec((None, nc, tn), lambda l, j: (l, 0, j)))],
        epilogue=_s5_in_epilogue, gain=mix_norm)
    return u


def _s5_ssm_kernel(u_ref, lr_ref, li_ref, ldt_ref, bre_ref, bim_ref, cre_ref, cim_ref, dsk_ref,
                   g_ref, tbig_ref, wend_ref, wc_ref, e_ref, s_ref):
    nc = u_ref.shape[1]
    L = S5_CHUNK
    P = S5_STATE_COLS
    lr = lr_ref[...]
    li = li_ref[...]
    dt = jnp.exp(ldt_ref[...])
    mag = jnp.exp(lr * dt)
    abar_re = mag * jnp.cos(li * dt)
    abar_im = mag * jnp.sin(li * dt)
    n_re = abar_re - 1.0
    n_im = abar_im
    den = lr * lr + li * li
    z_re = (n_re * lr + n_im * li) / den
    z_im = (n_im * lr - n_re * li) / den
    bre = bre_ref[...]
    bim = bim_ref[...]
    bbar_re = z_re * bre - z_im * bim
    bbar_im = z_re * bim + z_im * bre
    cre = cre_ref[...]
    cim = cim_ref[...]

    def power(d):
        m = jnp.exp((d * dt) * lr)
        ang = (d * dt) * li
        return m * jnp.cos(ang), m * jnp.sin(ang)

    def w_in_state(d):
        pr, pi = power(d)
        return jnp.concatenate([pr * bbar_re - pi * bbar_im, pr * bbar_im + pi * bbar_re],
                               axis=1).astype(BF16)

    def w_out_state(d):
        pr, pi = power(d)
        return jnp.concatenate([cre * pr - cim * pi, -(cre * pi + cim * pr)], axis=1).astype(BF16)

    wt0 = w_out_state(0)
    zero_blk = jnp.zeros((LANES, LANES), BF16)
    for d in range(L):
        wj = w_in_state(d)
        m_d = _dot_nt(wj, wt0).astype(BF16)
        for jj in range(L - d):
            tt = jj + d
            tbig_ref[jj * LANES:(jj + 1) * LANES, tt * LANES:(tt + 1) * LANES] = m_d
        if d > 0:
            for tt in range(L - d):
                jj = tt + d
                tbig_ref[jj * LANES:(jj + 1) * LANES, tt * LANES:(tt + 1) * LANES] = zero_blk
        wend_ref[(L - 1 - d) * LANES:(L - d) * LANES, :] = wj
        wc_ref[d * LANES:(d + 1) * LANES, :] = w_out_state(d + 1)

    x = jnp.concatenate([u_ref[l] for l in range(L)], axis=1).astype(BF16)
    e_ref[...] = _dot(x, wend_ref[...])
    pl_re, pl_im = power(L)

    s_ref[0:1, :] = jnp.zeros((1, 2 * P), F32)

    def step(c, carry):
        s_re, s_im = carry
        e = e_ref[pl.ds(c, 1), :]
        n_re_ = s_re * pl_re - s_im * pl_im + e[:, :P]
        n_im_ = s_re * pl_im + s_im * pl_re + e[:, P:]
        s_ref[pl.ds(c + 1, 1), :] = jnp.concatenate([n_re_, n_im_], axis=1)
        return n_re_, n_im_

    zero = jnp.zeros((1, P), F32)
    lax.fori_loop(0, nc - 1, step, (zero, zero))

    y = _dot(x, tbig_ref[...]) + _dot_nt(s_ref[...].astype(BF16), wc_ref[...])
    dsk = dsk_ref[...]
    for l in range(L):
        yl = y[:, l * LANES:(l + 1) * LANES] + dsk * u_ref[l]
        g_ref[l] = _gelu_tanh(yl).astype(BF16)


def _s5_ssm(u, a_re, a_im, log_dt, b_re, b_im, c_re, c_im, d_skip):
    L, nc, _ = u.shape
    P = S5_STATE_COLS
    gpt = S5_GROUPS_PER_TILE
    nt = N_LANE_TILES
    f = lambda v: v.astype(F32)
    row = lambda v: f(v).reshape(nt, 1, P)
    ldt = jnp.broadcast_to(f(log_dt)[:, None], (N_SSM_GROUPS, SSM_STATE))
    eye = jnp.eye(gpt, dtype=F32)

    def b_layout(b):
        bt = f(b).reshape(nt, gpt, SSM_STATE, SSM_GROUP)
        return jnp.einsum('kgpi,gh->kgihp', bt, eye).reshape(nt, LANES, P)

    def c_layout(c):
        ct = f(c).reshape(nt, gpt, SSM_GROUP, SSM_STATE)
        return jnp.einsum('kgip,gh->kgihp', ct, eye).reshape(nt, LANES, P)

    tile_row = pl.BlockSpec((None, 1, P), lambda k: (k, 0, 0))
    tile_mat = pl.BlockSpec((None, LANES, P), lambda k: (k, 0, 0))
    return pl.pallas_call(
        _s5_ssm_kernel,
        grid=(nt,),
        in_specs=[pl.BlockSpec((L, nc, LANES), lambda k: (0, 0, k)),
                  tile_row, tile_row, tile_row, tile_mat, tile_mat, tile_mat, tile_mat,
                  pl.BlockSpec((1, LANES), lambda k: (0, k))],
        out_specs=pl.BlockSpec((L, nc, LANES), lambda k: (0, 0, k)),
        out_shape=jax.ShapeDtypeStruct((L, nc, D_MODEL), BF16),
        scratch_shapes=[pltpu.VMEM((L * LANES, L * LANES), BF16),
                        pltpu.VMEM((L * LANES, 2 * P), BF16),
                        pltpu.VMEM((L * LANES, 2 * P), BF16),
                        pltpu.VMEM((nc, 2 * P), F32),
                        pltpu.VMEM((nc, 2 * P), F32)],
        compiler_params=_params("arbitrary"),
    )(u, row(a_re), row(a_im), row(ldt), b_layout(b_re), b_layout(b_im), c_layout(c_re),
      c_layout(c_im), f(d_skip).reshape(1, D_MODEL))


def _glu_epilogue(j, accs, extras, outs):
    outs[0][...] = extras[0][...] + accs[0] * _sigmoid(accs[1])


def _s5_glu(g, w_glu, h):
    L, nc, _ = g.shape
    tn = 512
    nj = D_MODEL // tn
    hv = h.reshape(nc, L * D_MODEL)
    (out,) = _fused_matmul(
        (L, nj), g, pl.BlockSpec((None, nc, D_MODEL), lambda l, j: (l, 0, 0)), nc,
        weights=[(w_glu, pl.BlockSpec((D_MODEL, tn), lambda l, j: (0, j))),
                 (w_glu, pl.BlockSpec((D_MODEL, tn), lambda l, j: (0, j + nj)))],
        extras=[(hv, pl.BlockSpec((nc, tn), lambda l, j: (0, l * nj + j)))],
        outs=[(jax.ShapeDtypeStruct((nc, L * D_MODEL), F32),
               pl.BlockSpec((nc, tn), lambda l, j: (0, l * nj + j)))],
        epilogue=_glu_epilogue)
    return out.reshape(nc * L, D_MODEL)


def _s5_layer(h, mix_norm, w_in, a_re, a_im, log_dt, b_re, b_im, c_re, c_im, d_skip, w_glu):
    u = _s5_in(h, mix_norm, w_in.astype(BF16))
    g = _s5_ssm(u, a_re, a_im, log_dt, b_re, b_im, c_re, c_im, d_skip)
    return _s5_glu(g, w_glu.astype(BF16), h)


def kernel(x, p, positions,
           l0_mix_norm, l0_w_qkv, l0_q_norm, l0_k_norm, l0_sinks, l0_w_o,
           l0_mlp_norm, l0_w_up, l0_w_down, l0_ple_norm, l0_w_ple_gate, l0_w_ple,
           l1_mix_norm, l1_w_in, l1_conv_w, l1_conv_b, l1_w_gate_a, l1_b_gate_a, l1_w_gate_x,
           l1_b_gate_x, l1_lru_lambda, l1_w_o,
           l1_mlp_norm, l1_w_up, l1_w_down, l1_ple_norm, l1_w_ple_gate, l1_w_ple,
           l2_mix_norm, l2_w_in, l2_a_re, l2_a_im, l2_log_dt, l2_b_re, l2_b_im, l2_c_re, l2_c_im,
           l2_d_skip, l2_w_glu,
           l2_mlp_norm, l2_w_up, l2_w_down, l2_ple_norm, l2_w_ple_gate, l2_w_ple,
           l3_mix_norm, l3_w_qkv, l3_q_norm, l3_k_norm, l3_sinks, l3_w_o,
           l3_mlp_norm, l3_w_up, l3_w_down, l3_ple_norm, l3_w_ple_gate, l3_w_ple):
    batch, t, _ = x.shape
    assert batch == 1 and t % (8 * BLOCK) == 0
    pos = positions.reshape(t).astype(jnp.int32)
    tails = [
        (l0_mlp_norm, l0_w_up, l0_w_down, l0_ple_norm, l0_w_ple_gate, l0_w_ple),
        (l1_mlp_norm, l1_w_up, l1_w_down, l1_ple_norm, l1_w_ple_gate, l1_w_ple),
        (l2_mlp_norm, l2_w_up, l2_w_down, l2_ple_norm, l2_w_ple_gate, l2_w_ple),
        (l3_mlp_norm, l3_w_up, l3_w_down, l3_ple_norm, l3_w_ple_gate, l3_w_ple),
    ]
    h = x.reshape(t, D_MODEL).astype(F32)
    for i in range(DEPTH):
        if i == 0:
            h = _attention_layer(h, pos, l0_mix_norm, l0_w_qkv, l0_q_norm, l0_k_norm, l0_sinks, l0_w_o)
        elif i == 1:
            h = _rglru_layer(h, l1_mix_norm, l1_w_in, l1_conv_w, l1_conv_b, l1_w_gate_a, l1_b_gate_a,
                             l1_w_gate_x, l1_b_gate_x, l1_lru_lambda, l1_w_o)
        elif i == 2:
            h = _s5_layer(h, l2_mix_norm, l2_w_in, l2_a_re, l2_a_im, l2_log_dt, l2_b_re, l2_b_im,
                          l2_c_re, l2_c_im, l2_d_skip, l2_w_glu)
        else:
            h = _attention_layer(h, pos, l3_mix_norm, l3_w_qkv, l3_q_norm, l3_k_norm, l3_sinks, l3_w_o)
        mlp_norm, w_up, w_down, ple_norm, w_ple_gate, w_ple = tails[i]
        h = _mlp(h, mlp_norm, w_up.astype(BF16), w_down.astype(BF16))
        h = _ple(h, ple_norm, p[i].reshape(t, PLE_DIM), w_ple.astype(BF16), w_ple_gate.astype(BF16))
    return h.reshape(batch, t, D_MODEL).astype(x.dtype)
```

```python
import functools
import math

import jax
import jax.numpy as jnp
from jax import lax
from jax.experimental import pallas as pl
from jax.experimental.pallas import tpu as pltpu

F32 = jnp.float32
BF16 = jnp.bfloat16

D_MODEL = 2048
DEPTH = 4
N_HEADS = 32
N_KV_HEADS = 4
HEAD_DIM = 64
Q_PER_KV = N_HEADS // N_KV_HEADS
WINDOW = 128
BLOCK = 128
PAD_POS = -(1 << 20)
N_GATE_BLOCKS = 8
GATE_BLOCK = D_MODEL // N_GATE_BLOCKS
CONV_WIDTH = 4
LRU_C = 8.0
SSM_GROUP = 16
N_SSM_GROUPS = D_MODEL // SSM_GROUP
SSM_STATE = 64
D_FF = 4 * D_MODEL
PLE_DIM = 256
EPS = 1e-6

LANES = 128
SUBLANES = 8
MXU_DIM = 256
VMEM_LIMIT_BYTES = 60 * 1024 * 1024
MASK_DIST = 1e33

S5_CHUNK = 16
S5_GROUPS_PER_TILE = LANES // SSM_GROUP
S5_STATE_COLS = S5_GROUPS_PER_TILE * SSM_STATE
N_LANE_TILES = D_MODEL // LANES


def _params(*semantics):
    return pltpu.CompilerParams(dimension_semantics=semantics,
                                vmem_limit_bytes=VMEM_LIMIT_BYTES)


def _rms(x, gain):
    ms = jnp.mean(x * x, axis=-1, keepdims=True)
    return x * lax.rsqrt(ms + EPS) * gain


def _gelu_tanh(x):
    return 0.5 * x * (1.0 + jnp.tanh(math.sqrt(2.0 / math.pi) * (x + 0.044715 * (x * x * x))))


def _sigmoid(x):
    return 0.5 * jnp.tanh(0.5 * x) + 0.5


def _dot(a, b):
    return jnp.dot(a, b, preferred_element_type=F32)


def _dot_nt(a, b):
    return lax.dot_general(a, b, (((1,), (1,)), ((), ())), preferred_element_type=F32)


def _resident(shape):
    zeros = (0,) * len(shape)
    return pl.BlockSpec(shape, lambda i: zeros, pipeline_mode=pl.Buffered(1))


def _row_matmul_kernel(*refs, n_extra, n_out, has_norm, col_offsets, tn, n_chunks, epilogue):
    lhs_ref = refs[0]
    pos = 1
    if has_norm:
        gain_ref = refs[1]
        pos = 2
    w_ref = refs[pos]
    pos += 1
    extra_refs = refs[pos:pos + n_extra]
    pos += n_extra
    out_refs = refs[pos:pos + n_out]
    pos += n_out
    if has_norm:
        src = refs[pos]
        pos += 1
        src[...] = _rms(lhs_ref[...], gain_ref[...]).astype(BF16)
    else:
        src = lhs_ref
    scratch_refs = refs[pos:]
    for c in range(n_chunks):
        cols = slice(c * tn, (c + 1) * tn)
        a = src[...]
        a = a.reshape(-1, a.shape[-1])
        accs = [_dot(a, w_ref[:, off + c * tn:off + (c + 1) * tn]) for off in col_offsets]
        epilogue(c, cols, accs, lhs_ref, extra_refs, out_refs, *scratch_refs)


def _row_matmul(name, n_blocks, lhs, lhs_spec, tm, w, col_offsets, tn, n_chunks, extras, outs,
                epilogue, gain=None, scratch=()):
    has_norm = gain is not None
    k_dim = w.shape[0]
    args = [lhs]
    in_specs = [lhs_spec]
    if has_norm:
        args.append(gain.reshape(1, k_dim).astype(F32))
        in_specs.append(_resident((1, k_dim)))
    args.append(w)
    in_specs.append(_resident(w.shape))
    for a, s in extras:
        args.append(a)
        in_specs.append(s)
    kern = functools.partial(_row_matmul_kernel, n_extra=len(extras), n_out=len(outs),
                             has_norm=has_norm, col_offsets=tuple(col_offsets), tn=tn,
                             n_chunks=n_chunks, epilogue=epilogue)
    scratch = ([pltpu.VMEM((tm, k_dim), BF16)] if has_norm else []) + list(scratch)
    return pl.pallas_call(
        kern,
        grid=(n_blocks,),
        in_specs=in_specs,
        out_specs=[s for _, s in outs],
        out_shape=[o for o, _ in outs],
        scratch_shapes=scratch,
        compiler_params=_params("parallel"),
        name=name,
    )(*args)


def _rows(tm, width):
    return pl.BlockSpec((tm, width), lambda i: (i, 0))


QKV_TN = 512
QKV_COLS = D_MODEL + 2 * N_KV_HEADS * LANES
QKV_NORM_BLOCKS = (D_MODEL + N_KV_HEADS * LANES) // QKV_TN
LOG2E = math.log2(math.e)


def _qkv_epilogue(c, cols, accs, lhs_ref, extras, outs):
    gain_ref, seg_ref = extras
    x = accs[0]
    if c < QKV_NORM_BLOCKS:
        x2 = (x * x).astype(BF16)
        ss = jnp.concatenate(
            [_dot(x2[:, k * MXU_DIM:(k + 1) * MXU_DIM], seg_ref[...])
             for k in range(QKV_TN // MXU_DIM)], axis=1)
        r = lax.rsqrt(ss * (1.0 / HEAD_DIM) + EPS)
        outs[0][:, cols] = (x * r * gain_ref[:, cols]).astype(BF16)
    else:
        outs[0][:, cols] = x.astype(BF16)


def _qkv_proj(h, mix_norm, w_prep, head_gain, seg_ones):
    t = h.shape[0]
    tm = min(1024, t)
    (qkv,) = _row_matmul(
        "qkv_proj", t // tm, h, _rows(tm, D_MODEL), tm, w_prep, (0,), QKV_TN, QKV_COLS // QKV_TN,
        extras=[(head_gain, _resident((1, QKV_COLS))), (seg_ones, _resident((MXU_DIM, MXU_DIM)))],
        outs=[(jax.ShapeDtypeStruct((t, QKV_COLS), BF16), _rows(tm, QKV_COLS))],
        epilogue=_qkv_epilogue, gain=mix_norm)
    return qkv


def _alibi_slopes():
    return [float(2.0 ** (-8.0 * (i + 1) / N_HEADS)) for i in range(N_HEADS)]


def _attn_kernel(q_ref, kc_ref, kp_ref, vc_ref, vp_ref, qpos_ref, kposc_ref, kposp_ref,
                 sinks_ref, o_ref):
    first = pl.program_id(0) == 0
    qpos = qpos_ref[...]
    kpos_prev = jnp.where(first, PAD_POS, kposp_ref[...])
    kpos = jnp.concatenate([kpos_prev, kposc_ref[...]], axis=1)
    dist = qpos - kpos
    valid = (dist >= 0) & (dist < WINDOW)
    distm = jnp.where(valid, dist.astype(F32), MASK_DIST)
    lane = lax.broadcasted_iota(jnp.int32, (BLOCK, LANES), 1)
    lo_f = (lane < HEAD_DIM).astype(F32)
    lo = lo_f.astype(BF16)
    hi = (1.0 - lo_f).astype(BF16)
    lo2 = jnp.concatenate([lo, lo], axis=0)
    hi2 = jnp.concatenate([hi, hi], axis=0)
    keep_prev = (jnp.ones((BLOCK, LANES), F32) * jnp.where(first, 0.0, 1.0)).astype(BF16)
    slopes = _alibi_slopes()
    for h in range(N_KV_HEADS):
        ksl = slice(h * LANES, (h + 1) * LANES)
        kk = jnp.concatenate([kp_ref[:, ksl] * keep_prev, kc_ref[:, ksl]], axis=0)
        vd = jnp.concatenate([vp_ref[:, ksl] * keep_prev, vc_ref[:, ksl]], axis=0)
        qs = []
        for g in range(Q_PER_KV):
            col = h * Q_PER_KV * HEAD_DIM + (g // 2) * LANES
            qp = q_ref[:, col:col + LANES]
            qs.append(qp * (lo if g % 2 == 0 else hi))
        s_all = _dot_nt(jnp.concatenate(qs, axis=0), kk)
        ps = []
        rls = []
        for g in range(Q_PER_KV):
            head = h * Q_PER_KV + g
            sg = s_all[g * BLOCK:(g + 1) * BLOCK] - (slopes[head] * LOG2E) * distm
            sink = sinks_ref[head] * LOG2E
            m = jnp.maximum(jnp.max(sg, axis=-1, keepdims=True), sink)
            p = jnp.exp2(sg - m)
            denom = jnp.sum(p, axis=-1, keepdims=True) + jnp.exp2(sink - m)
            ps.append(p.astype(BF16))
            rls.append(1.0 / denom)
        lhs2 = jnp.concatenate(
            [jnp.concatenate([ps[2 * p], ps[2 * p + 1]], axis=1) for p in range(Q_PER_KV // 2)],
            axis=0)
        rhs2 = jnp.concatenate([vd * lo2, vd * hi2], axis=0)
        o2 = _dot(lhs2, rhs2)
        for p in range(Q_PER_KV // 2):
            scale = jnp.where(lane < HEAD_DIM, rls[2 * p], rls[2 * p + 1])
            col = h * Q_PER_KV * HEAD_DIM + p * LANES
            o_ref[:, col:col + LANES] = (o2[p * BLOCK:(p + 1) * BLOCK] * scale).astype(BF16)


def _attn_core(qkv, positions, sinks):
    t = qkv.shape[0]
    nb = t // BLOCK
    pos_col = positions.reshape(t, 1)
    pos_row = positions.reshape(nb, 1, BLOCK)
    k_blk = D_MODEL // QKV_TN
    v_blk = k_blk + 1
    prev = lambda n: jnp.maximum(n - 1, 0)
    return pl.pallas_call(
        _attn_kernel,
        grid=(nb,),
        in_specs=[
            pl.BlockSpec((BLOCK, D_MODEL), lambda n: (n, 0)),
            pl.BlockSpec((BLOCK, QKV_TN), lambda n: (n, k_blk)),
            pl.BlockSpec((BLOCK, QKV_TN), lambda n: (prev(n), k_blk)),
            pl.BlockSpec((BLOCK, QKV_TN), lambda n: (n, v_blk)),
            pl.BlockSpec((BLOCK, QKV_TN), lambda n: (prev(n), v_blk)),
            pl.BlockSpec((BLOCK, 1), lambda n: (n, 0)),
            pl.BlockSpec((None, 1, BLOCK), lambda n: (n, 0, 0)),
            pl.BlockSpec((None, 1, BLOCK), lambda n: (prev(n), 0, 0)),
            pl.BlockSpec(memory_space=pltpu.SMEM),
        ],
        out_specs=pl.BlockSpec((BLOCK, D_MODEL), lambda n: (n, 0)),
        out_shape=jax.ShapeDtypeStruct((t, D_MODEL), BF16),
        compiler_params=_params("arbitrary"),
        name="attn_core",
    )(qkv, qkv, qkv, qkv, qkv, pos_col, pos_row, pos_row, sinks.astype(F32))


def _residual_epilogue(c, cols, accs, lhs_ref, extras, outs):
    outs[0][:, cols] = extras[0][:, cols] + accs[0]


def _proj_residual(a, w, h):
    t, k_dim = a.shape
    tm = min(1024, t)
    (out,) = _row_matmul(
        "proj_residual", t // tm, a, _rows(tm, k_dim), tm, w, (0,), 512, D_MODEL // 512,
        extras=[(h, _rows(tm, D_MODEL))],
        outs=[(jax.ShapeDtypeStruct((t, D_MODEL), F32), _rows(tm, D_MODEL))],
        epilogue=_residual_epilogue)
    return out


def _prep_qkv_weight(w_qkv):
    wq = w_qkv[:, :D_MODEL]
    wk = w_qkv[:, D_MODEL:D_MODEL + N_KV_HEADS * HEAD_DIM].reshape(D_MODEL, N_KV_HEADS, HEAD_DIM)
    wv = w_qkv[:, D_MODEL + N_KV_HEADS * HEAD_DIM:].reshape(D_MODEL, N_KV_HEADS, HEAD_DIM)
    dup = lambda w: jnp.concatenate([w, w], axis=-1).reshape(D_MODEL, N_KV_HEADS * LANES)
    return jnp.concatenate([wq, dup(wk), dup(wv)], axis=1).astype(BF16)


def _attention_layer(h, positions, mix_norm, w_qkv, q_norm, k_norm, sinks, w_o):
    head_gain = jnp.concatenate([
        jnp.tile(q_norm.astype(F32), N_HEADS) * (HEAD_DIM ** -0.5 * LOG2E),
        jnp.tile(k_norm.astype(F32), 2 * N_KV_HEADS),
        jnp.ones((N_KV_HEADS * LANES,), F32)]).reshape(1, QKV_COLS)
    seg_ones = jnp.kron(jnp.eye(MXU_DIM // HEAD_DIM, dtype=F32),
                        jnp.ones((HEAD_DIM, HEAD_DIM), F32)).astype(BF16)
    qkv = _qkv_proj(h, mix_norm, _prep_qkv_weight(w_qkv), head_gain, seg_ones)
    o = _attn_core(qkv, positions, sinks)
    return _proj_residual(o, w_o.astype(BF16), h)


MLP_TF = 512


def _mlp_kernel(h_ref, gain_ref, wu_ref, wd_ref, o_ref, hn_ref):
    j = pl.program_id(1)

    @pl.when(j == 0)
    def _():
        x = h_ref[...]
        hn_ref[...] = _rms(x, gain_ref[...]).astype(BF16)
        o_ref[...] = x

    act = jnp.maximum(_dot(hn_ref[...], wu_ref[...].astype(BF16)), 0.0)
    act = (act * act).astype(BF16)
    o_ref[...] += _dot(act, wd_ref[...].astype(BF16))


def _mlp(h, gain, w_up, w_down):
    t = h.shape[0]
    tm = min(1024, t)
    tf = MLP_TF
    return pl.pallas_call(
        _mlp_kernel,
        grid=(t // tm, D_FF // tf),
        in_specs=[
            pl.BlockSpec((tm, D_MODEL), lambda i, j: (i, 0), pipeline_mode=pl.Buffered(1)),
            pl.BlockSpec((1, D_MODEL), lambda i, j: (0, 0)),
            pl.BlockSpec((D_MODEL, tf), lambda i, j: (0, j)),
            pl.BlockSpec((tf, D_MODEL), lambda i, j: (j, 0)),
        ],
        out_specs=pl.BlockSpec((tm, D_MODEL), lambda i, j: (i, 0)),
        out_shape=jax.ShapeDtypeStruct((t, D_MODEL), F32),
        scratch_shapes=[pltpu.VMEM((tm, D_MODEL), BF16)],
        compiler_params=_params("parallel", "arbitrary"),
        name="mlp",
    )(h, gain.reshape(1, D_MODEL).astype(F32), w_up.astype(F32), w_down.astype(F32))


def _ple_epilogue(c, cols, accs, lhs_ref, extras, outs):
    p_ref, wple_ref = extras
    emb = _dot(p_ref[...].astype(BF16), wple_ref[:, cols])
    outs[0][:, cols] = lhs_ref[:, cols] + emb * _sigmoid(accs[0])


def _ple(h, gain, p_i, w_ple, w_gate):
    t = h.shape[0]
    tm = min(1024, t)
    (out,) = _row_matmul(
        "ple", t // tm, h, _rows(tm, D_MODEL), tm, w_gate, (0,), 512, D_MODEL // 512,
        extras=[(p_i, _rows(tm, PLE_DIM)), (w_ple, _resident((PLE_DIM, D_MODEL)))],
        outs=[(jax.ShapeDtypeStruct((t, D_MODEL), F32), _rows(tm, D_MODEL))],
        epilogue=_ple_epilogue, gain=gain)
    return out


def _rglru_in_epilogue(c, cols, accs, lhs_ref, extras, outs):
    outs[0][:, cols] = accs[0]
    outs[1][:, cols] = _gelu_tanh(accs[1]).astype(BF16)


def _rglru_in(h, mix_norm, w_in):
    t = h.shape[0]
    tm = min(512, t)
    return _row_matmul(
        "rglru_in", t // tm, h, _rows(tm, D_MODEL), tm, w_in, (0, D_MODEL), 512, D_MODEL // 512,
        extras=[],
        outs=[(jax.ShapeDtypeStruct((t, D_MODEL), F32), _rows(tm, D_MODEL)),
              (jax.ShapeDtypeStruct((t, D_MODEL), BF16), _rows(tm, D_MODEL))],
        epilogue=_rglru_in_epilogue, gain=mix_norm)


RGLRU_TB = 256
HALO = SUBLANES


def _rglru_scan_kernel(xb_ref, gate_ref, cw_ref, cb_ref, wa_ref, ba_ref, wx_ref, bx_ref, lam_ref,
                       y_ref, xs_ref, a_ref, b_ref, carry_ref):
    tb = RGLRU_TB

    @pl.when(pl.program_id(0) == 0)
    def _():
        xs_ref[0:HALO, :] = jnp.zeros((HALO, D_MODEL), F32)
        carry_ref[...] = jnp.zeros((1, D_MODEL), F32)

    xs_ref[HALO:HALO + tb, :] = xb_ref[...]
    xc = cb_ref[...] + cw_ref[CONV_WIDTH - 1:CONV_WIDTH, :] * xs_ref[HALO:HALO + tb, :]
    for k in range(CONV_WIDTH - 1):
        off = HALO - (CONV_WIDTH - 1) + k
        xc = xc + cw_ref[k:k + 1, :] * xs_ref[off:off + tb, :]
    xs_ref[0:HALO, :] = xs_ref[tb:tb + HALO, :]

    lam = lam_ref[...]
    rate = (-LRU_C) * (jnp.log(1.0 + jnp.exp(-jnp.abs(lam))) + jnp.maximum(-lam, 0.0))
    for n in range(N_GATE_BLOCKS):
        sl = slice(n * GATE_BLOCK, (n + 1) * GATE_BLOCK)
        xn = xc[:, sl]
        xn16 = xn.astype(BF16)
        r = _sigmoid(_dot(xn16, wa_ref[n]) + ba_ref[:, sl])
        i_gate = _sigmoid(_dot(xn16, wx_ref[n]) + bx_ref[:, sl])
        a = jnp.exp(rate[:, sl] * r)
        a_ref[:, sl] = a
        b_ref[:, sl] = jnp.sqrt(1.0 - a * a) * (i_gate * xn)

    def step(t, hprev):
        hnew = a_ref[pl.ds(t, 1), :] * hprev + b_ref[pl.ds(t, 1), :]
        b_ref[pl.ds(t, 1), :] = hnew
        return hnew

    carry_ref[...] = lax.fori_loop(0, tb, step, carry_ref[...], unroll=8)
    y_ref[...] = (b_ref[...] * gate_ref[...].astype(F32)).astype(BF16)


def _rglru_scan(xb, gate, conv_w, conv_b, w_gate_a, b_gate_a, w_gate_x, b_gate_x, lru_lambda):
    t = xb.shape[0]
    tb = RGLRU_TB
    row = lambda v: v.reshape(1, D_MODEL).astype(F32)
    return pl.pallas_call(
        _rglru_scan_kernel,
        grid=(t // tb,),
        in_specs=[
            _rows(tb, D_MODEL), _rows(tb, D_MODEL),
            _resident((CONV_WIDTH, D_MODEL)), _resident((1, D_MODEL)),
            _resident((N_GATE_BLOCKS, GATE_BLOCK, GATE_BLOCK)), _resident((1, D_MODEL)),
            _resident((N_GATE_BLOCKS, GATE_BLOCK, GATE_BLOCK)), _resident((1, D_MODEL)),
            _resident((1, D_MODEL)),
        ],
        out_specs=_rows(tb, D_MODEL),
        out_shape=jax.ShapeDtypeStruct((t, D_MODEL), BF16),
        scratch_shapes=[pltpu.VMEM((tb + HALO, D_MODEL), F32), pltpu.VMEM((tb, D_MODEL), F32),
                        pltpu.VMEM((tb, D_MODEL), F32), pltpu.VMEM((1, D_MODEL), F32)],
        compiler_params=_params("arbitrary"),
        name="rglru_scan",
    )(xb, gate, conv_w.astype(F32), row(conv_b), w_gate_a.astype(BF16), row(b_gate_a),
      w_gate_x.astype(BF16), row(b_gate_x), row(lru_lambda))


def _rglru_layer(h, mix_norm, w_in, conv_w, conv_b, w_gate_a, b_gate_a, w_gate_x, b_gate_x,
                 lru_lambda, w_o):
    xb, gate = _rglru_in(h, mix_norm, w_in.astype(BF16))
    y = _rglru_scan(xb, gate, conv_w, conv_b, w_gate_a, b_gate_a, w_gate_x, b_gate_x, lru_lambda)
    return _proj_residual(y, w_o.astype(BF16), h)


S5_TM = 512


def _s5_in_epilogue(c, cols, accs, lhs_ref, extras, outs, acc_ref):
    n = acc_ref.shape[1] // S5_CHUNK
    for k in range(acc_ref.shape[0]):
        acc_ref[k] = accs[0][:, k * LANES:(k + 1) * LANES]
        for l in range(S5_CHUNK):
            outs[0][l, :, cols.start + k * LANES:cols.start + (k + 1) * LANES] = (
                acc_ref[k, pl.ds(l, n, stride=S5_CHUNK), :])


def _s5_in(h, mix_norm, w_in):
    t = h.shape[0]
    nc = t // S5_CHUNK
    tm = min(S5_TM, t)
    tn = 512
    (u,) = _row_matmul(
        "s5_in", t // tm, h, _rows(tm, D_MODEL), tm, w_in, (0,), tn, D_MODEL // tn,
        extras=[],
        outs=[(jax.ShapeDtypeStruct((S5_CHUNK, nc, D_MODEL), F32),
               pl.BlockSpec((S5_CHUNK, tm // S5_CHUNK, D_MODEL), lambda i: (0, i, 0)))],
        epilogue=_s5_in_epilogue, gain=mix_norm,
        scratch=[pltpu.VMEM((tn // LANES, tm, LANES), F32)])
    return u


def _s5_ssm_kernel(u_ref, lr_ref, li_ref, ldt_ref, bre_ref, bim_ref, cre_ref, cim_ref, dsk_ref,
                   g_ref, tbig_ref, wend_ref, wc_ref, x_ref, e_ref, s_ref):
    nc = u_ref.shape[1]
    L = S5_CHUNK
    P = S5_STATE_COLS
    lr = lr_ref[...]
    li = li_ref[...]
    dt = jnp.exp(ldt_ref[...])
    mag = jnp.exp(lr * dt)
    abar_re = mag * jnp.cos(li * dt)
    abar_im = mag * jnp.sin(li * dt)
    n_re = abar_re - 1.0
    n_im = abar_im
    den = lr * lr + li * li
    z_re = (n_re * lr + n_im * li) / den
    z_im = (n_im * lr - n_re * li) / den
    bre = bre_ref[...]
    bim = bim_ref[...]
    bbar_re = z_re * bre - z_im * bim
    bbar_im = z_re * bim + z_im * bre
    cre = cre_ref[...]
    cim = cim_ref[...]

    def power(d):
        m = jnp.exp((d * dt) * lr)
        ang = (d * dt) * li
        return m * jnp.cos(ang), m * jnp.sin(ang)

    def w_in_state(d):
        pr, pi = power(d)
        return jnp.concatenate([pr * bbar_re - pi * bbar_im, pr * bbar_im + pi * bbar_re],
                               axis=1).astype(BF16)

    def w_out_state(d):
        pr, pi = power(d)
        return jnp.concatenate([cre * pr - cim * pi, -(cre * pi + cim * pr)], axis=1).astype(BF16)

    wt0 = w_out_state(0)
    for d in range(L):
        wj = w_in_state(d)
        m_d = _dot_nt(wj, wt0).astype(BF16)
        for jj in range(L - d):
            tt = jj + d
            tbig_ref[jj * LANES:(jj + 1) * LANES, tt * LANES:(tt + 1) * LANES] = m_d
        wend_ref[(L - 1 - d) * LANES:(L - d) * LANES, :] = wj
        wc_ref[d * LANES:(d + 1) * LANES, :] = w_out_state(d + 1)
    for tt in range(0, L, 2):
        tbig_ref[(tt + 1) * LANES:(tt + 2) * LANES, tt * LANES:(tt + 1) * LANES] = (
            jnp.zeros((LANES, LANES), BF16))

    x_ref[...] = jnp.concatenate([u_ref[l] for l in range(L)], axis=1).astype(BF16)
    e_ref[...] = _dot(x_ref[...], wend_ref[...])

    row = lax.broadcasted_iota(jnp.int32, (SUBLANES, P), 0)
    steps = []
    for s in (1, 2, 4):
        pr, pi = power(L * s)
        steps.append((s, row >= s, pr, pi))
    carry_pw = [power(L * (r + 1)) for r in range(SUBLANES)]
    cpw_re = jnp.concatenate([c[0] for c in carry_pw], axis=0)
    cpw_im = jnp.concatenate([c[1] for c in carry_pw], axis=0)

    s_ref[0:SUBLANES, :] = jnp.zeros((SUBLANES, 2 * P), F32)

    def tile_step(i, carry):
        s_re, s_im = carry
        base = pl.multiple_of(i * SUBLANES, SUBLANES)
        e = e_ref[pl.ds(base, SUBLANES), :]
        t_re = e[:, :P]
        t_im = e[:, P:]
        for s, keep, pr, pi in steps:
            sh_re = jnp.where(keep, pltpu.roll(t_re, s, 0), 0.0)
            sh_im = jnp.where(keep, pltpu.roll(t_im, s, 0), 0.0)
            t_re, t_im = t_re + pr * sh_re - pi * sh_im, t_im + pr * sh_im + pi * sh_re
        f_re = t_re + cpw_re * s_re - cpw_im * s_im
        f_im = t_im + cpw_re * s_im + cpw_im * s_re
        s_ref[pl.ds(base + SUBLANES, SUBLANES), :] = jnp.concatenate([f_re, f_im], axis=1)
        return f_re[SUBLANES - 1:SUBLANES, :], f_im[SUBLANES - 1:SUBLANES, :]

    zero = jnp.zeros((1, P), F32)
    lax.fori_loop(0, nc // SUBLANES, tile_step, (zero, zero))

    s_start = s_ref[SUBLANES - 1:SUBLANES - 1 + nc, :].astype(BF16)
    dsk = dsk_ref[...]
    for b in range(L // 2):
        k_rows = (2 * b + 2) * LANES
        cols = slice(2 * b * LANES, (2 * b + 2) * LANES)
        y = _dot(x_ref[:, :k_rows], tbig_ref[:k_rows, cols]) + _dot_nt(s_start, wc_ref[cols, :])
        for l in (2 * b, 2 * b + 1):
            yl = y[:, (l - 2 * b) * LANES:(l - 2 * b + 1) * LANES] + dsk * u_ref[l]
            g_ref[l] = _gelu_tanh(yl).astype(BF16)


def _s5_ssm(u, a_re, a_im, log_dt, b_re, b_im, c_re, c_im, d_skip):
    L, nc, _ = u.shape
    P = S5_STATE_COLS
    gpt = S5_GROUPS_PER_TILE
    nt = N_LANE_TILES
    f = lambda v: v.astype(F32)
    row = lambda v: f(v).reshape(nt, 1, P)
    ldt = jnp.broadcast_to(f(log_dt)[:, None], (N_SSM_GROUPS, SSM_STATE))
    eye = jnp.eye(gpt, dtype=F32)

    def b_layout(b):
        bt = f(b).reshape(nt, gpt, SSM_STATE, SSM_GROUP)
        return jnp.einsum('kgpi,gh->kgihp', bt, eye).reshape(nt, LANES, P)

    def c_layout(c):
        ct = f(c).reshape(nt, gpt, SSM_GROUP, SSM_STATE)
        return jnp.einsum('kgip,gh->kgihp', ct, eye).reshape(nt, LANES, P)

    tile_row = pl.BlockSpec((None, 1, P), lambda k: (k, 0, 0))
    tile_mat = pl.BlockSpec((None, LANES, P), lambda k: (k, 0, 0))
    return pl.pallas_call(
        _s5_ssm_kernel,
        grid=(nt,),
        in_specs=[pl.BlockSpec((L, nc, LANES), lambda k: (0, 0, k)),
                  tile_row, tile_row, tile_row, tile_mat, tile_mat, tile_mat, tile_mat,
                  pl.BlockSpec((1, LANES), lambda k: (0, k))],
        out_specs=pl.BlockSpec((L, nc, LANES), lambda k: (0, 0, k)),
        out_shape=jax.ShapeDtypeStruct((L, nc, D_MODEL), BF16),
        scratch_shapes=[pltpu.VMEM((L * LANES, L * LANES), BF16),
                        pltpu.VMEM((L * LANES, 2 * P), BF16),
                        pltpu.VMEM((L * LANES, 2 * P), BF16),
                        pltpu.VMEM((nc, L * LANES), BF16),
                        pltpu.VMEM((nc, 2 * P), F32),
                        pltpu.VMEM((nc + SUBLANES, 2 * P), F32)],
        compiler_params=_params("arbitrary"),
        name="s5_ssm",
    )(u, row(a_re), row(a_im), row(ldt), b_layout(b_re), b_layout(b_im), c_layout(c_re),
      c_layout(c_im), f(d_skip).reshape(1, D_MODEL))


def _glu_epilogue(c, cols, accs, lhs_ref, extras, outs, y_ref):
    y = accs[0] * _sigmoid(accs[1])
    n = y.shape[0] // S5_CHUNK
    for k in range(y_ref.shape[0]):
        y_ref[k] = y[:, k * LANES:(k + 1) * LANES]
        lanes = slice(cols.start + k * LANES, cols.start + (k + 1) * LANES)
        for ch in range(n):
            rows = slice(ch * S5_CHUNK, (ch + 1) * S5_CHUNK)
            outs[0][rows, lanes] = extras[0][rows, lanes] + y_ref[k, pl.ds(ch, S5_CHUNK, stride=n), :]


def _s5_glu(g, w_glu, h):
    L, nc, _ = g.shape
    t = nc * L
    tm = min(S5_TM, t)
    (out,) = _row_matmul(
        "s5_glu", t // tm, g, pl.BlockSpec((L, tm // L, D_MODEL), lambda i: (0, i, 0)), tm, w_glu,
        (0, D_MODEL), 512, D_MODEL // 512,
        extras=[(h, _rows(tm, D_MODEL))],
        outs=[(jax.ShapeDtypeStruct((t, D_MODEL), F32), _rows(tm, D_MODEL))],
        epilogue=_glu_epilogue, scratch=[pltpu.VMEM((512 // LANES, tm, LANES), F32)])
    return out


def _s5_layer(h, mix_norm, w_in, a_re, a_im, log_dt, b_re, b_im, c_re, c_im, d_skip, w_glu):
    u = _s5_in(h, mix_norm, w_in.astype(BF16))
    g = _s5_ssm(u, a_re, a_im, log_dt, b_re, b_im, c_re, c_im, d_skip)
    return _s5_glu(g, w_glu.astype(BF16), h)


def kernel(x, p, positions,
           l0_mix_norm, l0_w_qkv, l0_q_norm, l0_k_norm, l0_sinks, l0_w_o,
           l0_mlp_norm, l0_w_up, l0_w_down, l0_ple_norm, l0_w_ple_gate, l0_w_ple,
           l1_mix_norm, l1_w_in, l1_conv_w, l1_conv_b, l1_w_gate_a, l1_b_gate_a, l1_w_gate_x,
           l1_b_gate_x, l1_lru_lambda, l1_w_o,
           l1_mlp_norm, l1_w_up, l1_w_down, l1_ple_norm, l1_w_ple_gate, l1_w_ple,
           l2_mix_norm, l2_w_in, l2_a_re, l2_a_im, l2_log_dt, l2_b_re, l2_b_im, l2_c_re, l2_c_im,
           l2_d_skip, l2_w_glu,
           l2_mlp_norm, l2_w_up, l2_w_down, l2_ple_norm, l2_w_ple_gate, l2_w_ple,
           l3_mix_norm, l3_w_qkv, l3_q_norm, l3_k_norm, l3_sinks, l3_w_o,
           l3_mlp_norm, l3_w_up, l3_w_down, l3_ple_norm, l3_w_ple_gate, l3_w_ple):
    batch, t, _ = x.shape
    assert batch == 1 and t % (8 * BLOCK) == 0
    pos = positions.reshape(t).astype(jnp.int32)
    tails = [
        (l0_mlp_norm, l0_w_up, l0_w_down, l0_ple_norm, l0_w_ple_gate, l0_w_ple),
        (l1_mlp_norm, l1_w_up, l1_w_down, l1_ple_norm, l1_w_ple_gate, l1_w_ple),
        (l2_mlp_norm, l2_w_up, l2_w_down, l2_ple_norm, l2_w_ple_gate, l2_w_ple),
        (l3_mlp_norm, l3_w_up, l3_w_down, l3_ple_norm, l3_w_ple_gate, l3_w_ple),
    ]
    h = x.reshape(t, D_MODEL).astype(F32)
    for i in range(DEPTH):
        if i == 0:
            h = _attention_layer(h, pos, l0_mix_norm, l0_w_qkv, l0_q_norm, l0_k_norm, l0_sinks, l0_w_o)
        elif i == 1:
            h = _rglru_layer(h, l1_mix_norm, l1_w_in, l1_conv_w, l1_conv_b, l1_w_gate_a, l1_b_gate_a,
                             l1_w_gate_x, l1_b_gate_x, l1_lru_lambda, l1_w_o)
        elif i == 2:
            h = _s5_layer(h, l2_mix_norm, l2_w_in, l2_a_re, l2_a_im, l2_log_dt, l2_b_re, l2_b_im,
                          l2_c_re, l2_c_im, l2_d_skip, l2_w_glu)
        else:
            h = _attention_layer(h, pos, l3_mix_norm, l3_w_qkv, l3_q_norm, l3_k_norm, l3_sinks, l3_w_o)
        mlp_norm, w_up, w_down, ple_norm, w_ple_gate, w_ple = tails[i]
        h = _mlp(h, mlp_norm, w_up, w_down)
        h = _ple(h, ple_norm, p[i].reshape(t, PLE_DIM), w_ple.astype(BF16), w_ple_gate.astype(BF16))
    return h.reshape(batch, t, D_MODEL).astype(x.dtype)
```

```python
import functools
import math

import jax
import jax.numpy as jnp
from jax import lax
from jax.experimental import pallas as pl
from jax.experimental.pallas import tpu as pltpu

F32 = jnp.float32
BF16 = jnp.bfloat16

D_MODEL = 2048
DEPTH = 4
N_HEADS = 32
N_KV_HEADS = 4
HEAD_DIM = 64
Q_PER_KV = N_HEADS // N_KV_HEADS
WINDOW = 128
BLOCK = 128
PAD_POS = -(1 << 20)
N_GATE_BLOCKS = 8
GATE_BLOCK = D_MODEL // N_GATE_BLOCKS
CONV_WIDTH = 4
LRU_C = 8.0
SSM_GROUP = 16
N_SSM_GROUPS = D_MODEL // SSM_GROUP
SSM_STATE = 64
D_FF = 4 * D_MODEL
PLE_DIM = 256
EPS = 1e-6

LANES = 128
SUBLANES = 8
MXU_DIM = 256
VMEM_LIMIT_BYTES = 60 * 1024 * 1024
MASK_DIST = 1e33

S5_CHUNK = 16
S5_GROUPS_PER_TILE = LANES // SSM_GROUP
S5_STATE_COLS = S5_GROUPS_PER_TILE * SSM_STATE
N_LANE_TILES = D_MODEL // LANES


def _params(*semantics):
    return pltpu.CompilerParams(dimension_semantics=semantics,
                                vmem_limit_bytes=VMEM_LIMIT_BYTES)


def _rms(x, gain):
    ms = jnp.mean(x * x, axis=-1, keepdims=True)
    return x * lax.rsqrt(ms + EPS) * gain


def _gelu_tanh(x):
    return 0.5 * x * (1.0 + jnp.tanh(math.sqrt(2.0 / math.pi) * (x + 0.044715 * (x * x * x))))


def _sigmoid(x):
    return 0.5 * jnp.tanh(0.5 * x) + 0.5


def _dot(a, b):
    return jnp.dot(a, b, preferred_element_type=F32)


def _dot_nt(a, b):
    return lax.dot_general(a, b, (((1,), (1,)), ((), ())), preferred_element_type=F32)


def _resident(shape):
    zeros = (0,) * len(shape)
    return pl.BlockSpec(shape, lambda i: zeros, pipeline_mode=pl.Buffered(1))


def _row_matmul_kernel(*refs, n_extra, n_out, has_norm, col_offsets, tn, n_chunks, epilogue):
    lhs_ref = refs[0]
    pos = 1
    if has_norm:
        gain_ref = refs[1]
        pos = 2
    w_ref = refs[pos]
    pos += 1
    extra_refs = refs[pos:pos + n_extra]
    pos += n_extra
    out_refs = refs[pos:pos + n_out]
    pos += n_out
    if has_norm:
        src = refs[pos]
        pos += 1
        src[...] = _rms(lhs_ref[...], gain_ref[...]).astype(BF16)
    else:
        src = lhs_ref
    scratch_refs = refs[pos:]
    for c in range(n_chunks):
        cols = slice(c * tn, (c + 1) * tn)
        a = src[...]
        a = a.reshape(-1, a.shape[-1])
        accs = [_dot(a, w_ref[:, off + c * tn:off + (c + 1) * tn]) for off in col_offsets]
        epilogue(c, cols, accs, lhs_ref, extra_refs, out_refs, *scratch_refs)


def _row_matmul(name, n_blocks, lhs, lhs_spec, tm, w, col_offsets, tn, n_chunks, extras, outs,
                epilogue, gain=None, scratch=()):
    has_norm = gain is not None
    k_dim = w.shape[0]
    args = [lhs]
    in_specs = [lhs_spec]
    if has_norm:
        args.append(gain.reshape(1, k_dim).astype(F32))
        in_specs.append(_resident((1, k_dim)))
    args.append(w)
    in_specs.append(_resident(w.shape))
    for a, s in extras:
        args.append(a)
        in_specs.append(s)
    kern = functools.partial(_row_matmul_kernel, n_extra=len(extras), n_out=len(outs),
                             has_norm=has_norm, col_offsets=tuple(col_offsets), tn=tn,
                             n_chunks=n_chunks, epilogue=epilogue)
    scratch = ([pltpu.VMEM((tm, k_dim), BF16)] if has_norm else []) + list(scratch)
    return pl.pallas_call(
        kern,
        grid=(n_blocks,),
        in_specs=in_specs,
        out_specs=[s for _, s in outs],
        out_shape=[o for o, _ in outs],
        scratch_shapes=scratch,
        compiler_params=_params("parallel"),
        name=name,
    )(*args)


def _rows(tm, width):
    return pl.BlockSpec((tm, width), lambda i: (i, 0))


QKV_TN = 512
QKV_COLS = D_MODEL + 2 * N_KV_HEADS * LANES
QKV_NORM_BLOCKS = (D_MODEL + N_KV_HEADS * LANES) // QKV_TN
LOG2E = math.log2(math.e)


def _qkv_epilogue(c, cols, accs, lhs_ref, extras, outs):
    gain_ref, seg_ref = extras
    x = accs[0]
    if c < QKV_NORM_BLOCKS:
        x2 = (x * x).astype(BF16)
        ss = jnp.concatenate(
            [_dot(x2[:, k * MXU_DIM:(k + 1) * MXU_DIM], seg_ref[...])
             for k in range(QKV_TN // MXU_DIM)], axis=1)
        r = lax.rsqrt(ss * (1.0 / HEAD_DIM) + EPS)
        outs[0][:, cols] = (x * r * gain_ref[:, cols]).astype(BF16)
    else:
        outs[0][:, cols] = x.astype(BF16)


def _qkv_proj(h, mix_norm, w_prep, head_gain, seg_ones):
    t = h.shape[0]
    tm = min(1024, t)
    (qkv,) = _row_matmul(
        "qkv_proj", t // tm, h, _rows(tm, D_MODEL), tm, w_prep, (0,), QKV_TN, QKV_COLS // QKV_TN,
        extras=[(head_gain, _resident((1, QKV_COLS))), (seg_ones, _resident((MXU_DIM, MXU_DIM)))],
        outs=[(jax.ShapeDtypeStruct((t, QKV_COLS), BF16), _rows(tm, QKV_COLS))],
        epilogue=_qkv_epilogue, gain=mix_norm)
    return qkv


def _alibi_slopes():
    return [float(2.0 ** (-8.0 * (i + 1) / N_HEADS)) for i in range(N_HEADS)]


def _attn_kernel(q_ref, kc_ref, kp_ref, vc_ref, vp_ref, qpos_ref, kposc_ref, kposp_ref,
                 sinks_ref, *rest, cast_mlp):
    if cast_mlp:
        nwu_ref, nwd_ref, o_ref, owu_ref, owd_ref = rest
        owu_ref[...] = nwu_ref[...].astype(BF16)
        owd_ref[...] = nwd_ref[...].astype(BF16)
    else:
        (o_ref,) = rest
    first = pl.program_id(0) == 0
    qpos = qpos_ref[...]
    kpos_prev = jnp.where(first, PAD_POS, kposp_ref[...])
    kpos = jnp.concatenate([kpos_prev, kposc_ref[...]], axis=1)
    dist = qpos - kpos
    valid = (dist >= 0) & (dist < WINDOW)
    distm = jnp.where(valid, dist.astype(F32), MASK_DIST)
    lane = lax.broadcasted_iota(jnp.int32, (BLOCK, LANES), 1)
    lo_f = (lane < HEAD_DIM).astype(F32)
    lo = lo_f.astype(BF16)
    hi = (1.0 - lo_f).astype(BF16)
    lo2 = jnp.concatenate([lo, lo], axis=0)
    hi2 = jnp.concatenate([hi, hi], axis=0)
    keep_prev = (jnp.ones((BLOCK, LANES), F32) * jnp.where(first, 0.0, 1.0)).astype(BF16)
    slopes = _alibi_slopes()
    for h in range(N_KV_HEADS):
        ksl = slice(h * LANES, (h + 1) * LANES)
        kk = jnp.concatenate([kp_ref[:, ksl] * keep_prev, kc_ref[:, ksl]], axis=0)
        vd = jnp.concatenate([vp_ref[:, ksl] * keep_prev, vc_ref[:, ksl]], axis=0)
        qs = []
        for g in range(Q_PER_KV):
            col = h * Q_PER_KV * HEAD_DIM + (g // 2) * LANES
            qp = q_ref[:, col:col + LANES]
            qs.append(qp * (lo if g % 2 == 0 else hi))
        s_all = _dot_nt(jnp.concatenate(qs, axis=0), kk)
        ps = []
        rls = []
        for g in range(Q_PER_KV):
            head = h * Q_PER_KV + g
            sg = s_all[g * BLOCK:(g + 1) * BLOCK] - (slopes[head] * LOG2E) * distm
            sink = sinks_ref[head] * LOG2E
            m = jnp.maximum(jnp.max(sg, axis=-1, keepdims=True), sink)
            p = jnp.exp2(sg - m)
            denom = jnp.sum(p, axis=-1, keepdims=True) + jnp.exp2(sink - m)
            ps.append(p.astype(BF16))
            rls.append(1.0 / denom)
        lhs2 = jnp.concatenate(
            [jnp.concatenate([ps[2 * p], ps[2 * p + 1]], axis=1) for p in range(Q_PER_KV // 2)],
            axis=0)
        rhs2 = jnp.concatenate([vd * lo2, vd * hi2], axis=0)
        o2 = _dot(lhs2, rhs2)
        for p in range(Q_PER_KV // 2):
            scale = jnp.where(lane < HEAD_DIM, rls[2 * p], rls[2 * p + 1])
            col = h * Q_PER_KV * HEAD_DIM + p * LANES
            o_ref[:, col:col + LANES] = (o2[p * BLOCK:(p + 1) * BLOCK] * scale).astype(BF16)


def _attn_core(qkv, positions, sinks, mlp_weights=None):
    t = qkv.shape[0]
    nb = t // BLOCK
    pos_col = positions.reshape(t, 1)
    pos_row = positions.reshape(nb, 1, BLOCK)
    k_blk = D_MODEL // QKV_TN
    v_blk = k_blk + 1
    prev = lambda n: jnp.maximum(n - 1, 0)
    in_specs = [
        pl.BlockSpec((BLOCK, D_MODEL), lambda n: (n, 0)),
        pl.BlockSpec((BLOCK, QKV_TN), lambda n: (n, k_blk)),
        pl.BlockSpec((BLOCK, QKV_TN), lambda n: (prev(n), k_blk)),
        pl.BlockSpec((BLOCK, QKV_TN), lambda n: (n, v_blk)),
        pl.BlockSpec((BLOCK, QKV_TN), lambda n: (prev(n), v_blk)),
        pl.BlockSpec((BLOCK, 1), lambda n: (n, 0)),
        pl.BlockSpec((None, 1, BLOCK), lambda n: (n, 0, 0)),
        pl.BlockSpec((None, 1, BLOCK), lambda n: (prev(n), 0, 0)),
        pl.BlockSpec(memory_space=pltpu.SMEM),
    ]
    out_specs = [pl.BlockSpec((BLOCK, D_MODEL), lambda n: (n, 0))]
    out_shape = [jax.ShapeDtypeStruct((t, D_MODEL), BF16)]
    args = [qkv, qkv, qkv, qkv, qkv, pos_col, pos_row, pos_row, sinks.astype(F32)]
    if mlp_weights is not None:
        specs, shapes = _cast_specs(nb, lambda n: n)
        in_specs += specs
        out_specs += specs
        out_shape += shapes
        args += [w.astype(F32) for w in mlp_weights]
    res = pl.pallas_call(
        functools.partial(_attn_kernel, cast_mlp=mlp_weights is not None),
        grid=(nb,),
        in_specs=in_specs,
        out_specs=out_specs,
        out_shape=out_shape,
        compiler_params=_params("arbitrary"),
        name="attn_core",
    )(*args)
    return res[0], tuple(res[1:])


def _residual_epilogue(c, cols, accs, lhs_ref, extras, outs):
    outs[0][:, cols] = extras[0][:, cols] + accs[0]


def _proj_residual(a, w, h):
    t, k_dim = a.shape
    tm = min(1024, t)
    (out,) = _row_matmul(
        "proj_residual", t // tm, a, _rows(tm, k_dim), tm, w, (0,), 512, D_MODEL // 512,
        extras=[(h, _rows(tm, D_MODEL))],
        outs=[(jax.ShapeDtypeStruct((t, D_MODEL), F32), _rows(tm, D_MODEL))],
        epilogue=_residual_epilogue)
    return out


def _prep_qkv_weight(w_qkv):
    wq = w_qkv[:, :D_MODEL]
    wk = w_qkv[:, D_MODEL:D_MODEL + N_KV_HEADS * HEAD_DIM].reshape(D_MODEL, N_KV_HEADS, HEAD_DIM)
    wv = w_qkv[:, D_MODEL + N_KV_HEADS * HEAD_DIM:].reshape(D_MODEL, N_KV_HEADS, HEAD_DIM)
    dup = lambda w: jnp.concatenate([w, w], axis=-1).reshape(D_MODEL, N_KV_HEADS * LANES)
    return jnp.concatenate([wq, dup(wk), dup(wv)], axis=1).astype(BF16)


def _attention_layer(h, positions, mix_norm, w_qkv, q_norm, k_norm, sinks, w_o, mlp_weights=None):
    head_gain = jnp.concatenate([
        jnp.tile(q_norm.astype(F32), N_HEADS) * (HEAD_DIM ** -0.5 * LOG2E),
        jnp.tile(k_norm.astype(F32), 2 * N_KV_HEADS),
        jnp.ones((N_KV_HEADS * LANES,), F32)]).reshape(1, QKV_COLS)
    seg_ones = jnp.kron(jnp.eye(MXU_DIM // HEAD_DIM, dtype=F32),
                        jnp.ones((HEAD_DIM, HEAD_DIM), F32)).astype(BF16)
    qkv = _qkv_proj(h, mix_norm, _prep_qkv_weight(w_qkv), head_gain, seg_ones)
    o, mlp16 = _attn_core(qkv, positions, sinks, mlp_weights)
    return _proj_residual(o, w_o.astype(BF16), h), mlp16


MLP_TF = 512


def _cast_specs(n_steps, step_index):
    ru = D_MODEL // n_steps
    rd = D_FF // n_steps
    assert ru % 16 == 0 and ru * n_steps == D_MODEL
    specs = [pl.BlockSpec((ru, D_FF), lambda *g: (step_index(*g), 0)),
             pl.BlockSpec((rd, D_MODEL), lambda *g: (step_index(*g), 0))]
    shapes = [jax.ShapeDtypeStruct((D_MODEL, D_FF), BF16), jax.ShapeDtypeStruct((D_FF, D_MODEL), BF16)]
    return specs, shapes


def _mlp_kernel(h_ref, gain_ref, wu_ref, wd_ref, *rest, cast_next):
    if cast_next:
        nwu_ref, nwd_ref, o_ref, owu_ref, owd_ref, hn_ref = rest
        owu_ref[...] = nwu_ref[...].astype(BF16)
        owd_ref[...] = nwd_ref[...].astype(BF16)
    else:
        o_ref, hn_ref = rest
    j = pl.program_id(1)

    @pl.when(j == 0)
    def _():
        x = h_ref[...]
        hn_ref[...] = _rms(x, gain_ref[...]).astype(BF16)
        o_ref[...] = x

    act = jnp.maximum(_dot(hn_ref[...], wu_ref[...]), 0.0)
    act = (act * act).astype(BF16)
    o_ref[...] += _dot(act, wd_ref[...])


def _mlp(h, gain, w_up16, w_down16, next_weights=None):
    t = h.shape[0]
    tm = min(1024, t)
    tf = MLP_TF
    nj = D_FF // tf
    in_specs = [
        pl.BlockSpec((tm, D_MODEL), lambda i, j: (i, 0)),
        pl.BlockSpec((1, D_MODEL), lambda i, j: (0, 0)),
        pl.BlockSpec((D_MODEL, tf), lambda i, j: (0, j)),
        pl.BlockSpec((tf, D_MODEL), lambda i, j: (j, 0)),
    ]
    out_specs = [pl.BlockSpec((tm, D_MODEL), lambda i, j: (i, 0))]
    out_shape = [jax.ShapeDtypeStruct((t, D_MODEL), F32)]
    args = [h, gain.reshape(1, D_MODEL).astype(F32), w_up16, w_down16]
    if next_weights is not None:
        specs, shapes = _cast_specs((t // tm) * nj, lambda i, j: i * nj + j)
        in_specs += specs
        out_specs += specs
        out_shape += shapes
        args += [w.astype(F32) for w in next_weights]
    res = pl.pallas_call(
        functools.partial(_mlp_kernel, cast_next=next_weights is not None),
        grid=(t // tm, nj),
        in_specs=in_specs,
        out_specs=out_specs,
        out_shape=out_shape,
        scratch_shapes=[pltpu.VMEM((tm, D_MODEL), BF16)],
        compiler_params=_params("parallel", "arbitrary"),
        name="mlp",
    )(*args)
    return res[0], tuple(res[1:])


def _ple_epilogue(c, cols, accs, lhs_ref, extras, outs):
    p_ref, wple_ref = extras
    emb = _dot(p_ref[...].astype(BF16), wple_ref[:, cols])
    outs[0][:, cols] = lhs_ref[:, cols] + emb * _sigmoid(accs[0])


def _ple(h, gain, p_all, layer, w_ple, w_gate):
    t = h.shape[0]
    tm = min(1024, t)
    first = layer * (t // tm)
    (out,) = _row_matmul(
        "ple", t // tm, h, _rows(tm, D_MODEL), tm, w_gate, (0,), 512, D_MODEL // 512,
        extras=[(p_all, pl.BlockSpec((tm, PLE_DIM), lambda i: (first + i, 0))),
                (w_ple, _resident((PLE_DIM, D_MODEL)))],
        outs=[(jax.ShapeDtypeStruct((t, D_MODEL), F32), _rows(tm, D_MODEL))],
        epilogue=_ple_epilogue, gain=gain)
    return out


def _rglru_in_epilogue(c, cols, accs, lhs_ref, extras, outs):
    outs[0][:, cols] = accs[0]
    outs[1][:, cols] = _gelu_tanh(accs[1]).astype(BF16)


def _rglru_in(h, mix_norm, w_in):
    t = h.shape[0]
    tm = min(512, t)
    return _row_matmul(
        "rglru_in", t // tm, h, _rows(tm, D_MODEL), tm, w_in, (0, D_MODEL), 512, D_MODEL // 512,
        extras=[],
        outs=[(jax.ShapeDtypeStruct((t, D_MODEL), F32), _rows(tm, D_MODEL)),
              (jax.ShapeDtypeStruct((t, D_MODEL), BF16), _rows(tm, D_MODEL))],
        epilogue=_rglru_in_epilogue, gain=mix_norm)


RGLRU_TB = 256
RGLRU_SEG = RGLRU_TB // SUBLANES
RGLRU_PITCH = 36
TINY = 1e-36


def _rglru_scan_kernel(xb_ref, gate_ref, cw_ref, cb_ref, wa_ref, ba_ref, wx_ref, bx_ref, lam_ref,
                       y_ref, a_ref, b_ref, carry_ref, tail_ref):
    tb = RGLRU_TB
    seg = RGLRU_SEG
    n_slabs = D_MODEL // LANES

    @pl.when(pl.program_id(0) == 0)
    def _():
        carry_ref[...] = jnp.zeros((1, D_MODEL), F32)
        tail_ref[...] = jnp.zeros((CONV_WIDTH - 1, D_MODEL), F32)

    def shift1(v, k):
        prev_last = tail_ref[k:k + 1, :]
        tail_ref[k:k + 1, :] = v[tb - 1:tb, :]
        return jnp.concatenate([prev_last, v[:tb - 1, :]], axis=0)

    x = xb_ref[...]
    acc = cw_ref[0:1, :] * x
    for k in range(1, CONV_WIDTH):
        acc = cw_ref[k:k + 1, :] * x + shift1(acc, k - 1)
    xc = acc + cb_ref[...]

    lam = lam_ref[...]
    rate = (-LRU_C) * (jnp.log(1.0 + jnp.exp(-jnp.abs(lam))) + jnp.maximum(-lam, 0.0))
    for n in range(N_GATE_BLOCKS):
        sl = slice(n * GATE_BLOCK, (n + 1) * GATE_BLOCK)
        xn = xc[:, sl]
        xn16 = xn.astype(BF16)
        r = _sigmoid(_dot(xn16, wa_ref[n]) + ba_ref[:, sl])
        i_gate = _sigmoid(_dot(xn16, wx_ref[n]) + bx_ref[:, sl])
        a = jnp.exp(rate[:, sl] * r)
        v = 1.0 - a * a
        b = (v * lax.rsqrt(jnp.maximum(v, TINY))) * (i_gate * xn)
        for kk in range(GATE_BLOCK // LANES):
            slab = n * (GATE_BLOCK // LANES) + kk
            for s in range(SUBLANES):
                rows = slice(s * RGLRU_PITCH, s * RGLRU_PITCH + seg)
                a_ref[slab, rows, :] = a[s * seg:(s + 1) * seg, kk * LANES:(kk + 1) * LANES]
                b_ref[slab, rows, :] = b[s * seg:(s + 1) * seg, kk * LANES:(kk + 1) * LANES]

    def step(q, carry):
        hs, prods = carry
        rows = pl.ds(q, SUBLANES, stride=RGLRU_PITCH)
        new_hs = []
        new_prods = []
        for k in range(n_slabs):
            av = a_ref[k, rows, :]
            hk = av * hs[k] + b_ref[k, rows, :]
            pk = av * prods[k]
            b_ref[k, rows, :] = hk
            a_ref[k, rows, :] = pk
            new_hs.append(hk)
            new_prods.append(pk)
        return tuple(new_hs), tuple(new_prods)

    zeros = tuple(jnp.zeros((SUBLANES, LANES), F32) for _ in range(n_slabs))
    ones = tuple(jnp.ones((SUBLANES, LANES), F32) for _ in range(n_slabs))
    hs, prods = lax.fori_loop(0, seg, step, (zeros, ones))

    for k in range(n_slabs):
        lanes = slice(k * LANES, (k + 1) * LANES)
        cur = carry_ref[:, lanes]
        for s in range(SUBLANES):
            rows = slice(s * RGLRU_PITCH, s * RGLRU_PITCH + seg)
            hfull = b_ref[k, rows, :] + a_ref[k, rows, :] * cur
            out_rows = slice(s * seg, (s + 1) * seg)
            y_ref[out_rows, lanes] = (hfull * gate_ref[out_rows, lanes].astype(F32)).astype(BF16)
            cur = prods[k][s:s + 1, :] * cur + hs[k][s:s + 1, :]
        carry_ref[:, lanes] = cur


def _rglru_scan(xb, gate, conv_w, conv_b, w_gate_a, b_gate_a, w_gate_x, b_gate_x, lru_lambda):
    t = xb.shape[0]
    tb = RGLRU_TB
    row = lambda v: v.reshape(1, D_MODEL).astype(F32)
    return pl.pallas_call(
        _rglru_scan_kernel,
        grid=(t // tb,),
        in_specs=[
            _rows(tb, D_MODEL), _rows(tb, D_MODEL),
            _resident((CONV_WIDTH, D_MODEL)), _resident((1, D_MODEL)),
            _resident((N_GATE_BLOCKS, GATE_BLOCK, GATE_BLOCK)), _resident((1, D_MODEL)),
            _resident((N_GATE_BLOCKS, GATE_BLOCK, GATE_BLOCK)), _resident((1, D_MODEL)),
            _resident((1, D_MODEL)),
        ],
        out_specs=_rows(tb, D_MODEL),
        out_shape=jax.ShapeDtypeStruct((t, D_MODEL), BF16),
        scratch_shapes=[pltpu.VMEM((D_MODEL // LANES, SUBLANES * RGLRU_PITCH, LANES), F32),
                        pltpu.VMEM((D_MODEL // LANES, SUBLANES * RGLRU_PITCH, LANES), F32),
                        pltpu.VMEM((1, D_MODEL), F32),
                        pltpu.VMEM((CONV_WIDTH - 1, D_MODEL), F32)],
        compiler_params=_params("arbitrary"),
        name="rglru_scan",
    )(xb, gate, conv_w.astype(F32), row(conv_b), w_gate_a.astype(BF16), row(b_gate_a),
      w_gate_x.astype(BF16), row(b_gate_x), row(lru_lambda))


def _rglru_layer(h, mix_norm, w_in, conv_w, conv_b, w_gate_a, b_gate_a, w_gate_x, b_gate_x,
                 lru_lambda, w_o):
    xb, gate = _rglru_in(h, mix_norm, w_in.astype(BF16))
    y = _rglru_scan(xb, gate, conv_w, conv_b, w_gate_a, b_gate_a, w_gate_x, b_gate_x, lru_lambda)
    return _proj_residual(y, w_o.astype(BF16), h)


S5_TM = 512
S5_IN_PITCH = 20
S5_GLU_PITCH = 36


def _s5_in_epilogue(c, cols, accs, lhs_ref, extras, outs, acc_ref):
    n = accs[0].shape[0] // S5_CHUNK
    for k in range(acc_ref.shape[0]):
        for ch in range(n):
            acc_ref[k, ch * S5_IN_PITCH:ch * S5_IN_PITCH + S5_CHUNK, :] = (
                accs[0][ch * S5_CHUNK:(ch + 1) * S5_CHUNK, k * LANES:(k + 1) * LANES])
        for l in range(S5_CHUNK):
            outs[0][l, :, cols.start + k * LANES:cols.start + (k + 1) * LANES] = (
                acc_ref[k, pl.ds(l, n, stride=S5_IN_PITCH), :])


def _s5_in(h, mix_norm, w_in):
    t = h.shape[0]
    nc = t // S5_CHUNK
    tm = min(S5_TM, t)
    tn = 512
    (u,) = _row_matmul(
        "s5_in", t // tm, h, _rows(tm, D_MODEL), tm, w_in, (0,), tn, D_MODEL // tn,
        extras=[],
        outs=[(jax.ShapeDtypeStruct((S5_CHUNK, nc, D_MODEL), F32),
               pl.BlockSpec((S5_CHUNK, tm // S5_CHUNK, D_MODEL), lambda i: (0, i, 0)))],
        epilogue=_s5_in_epilogue, gain=mix_norm,
        scratch=[pltpu.VMEM((tn // LANES, (tm // S5_CHUNK) * S5_IN_PITCH, LANES), F32)])
    return u


def _s5_ssm_kernel(u_ref, lr_ref, li_ref, ldt_ref, bre_ref, bim_ref, cre_ref, cim_ref, dsk_ref,
                   g_ref, tbig_ref, wend_ref, wc_ref, x_ref, e_ref, s_ref):
    nc = u_ref.shape[1]
    L = S5_CHUNK
    P = S5_STATE_COLS
    lr = lr_ref[...]
    li = li_ref[...]
    dt = jnp.exp(ldt_ref[...])
    mag = jnp.exp(lr * dt)
    abar_re = mag * jnp.cos(li * dt)
    abar_im = mag * jnp.sin(li * dt)
    n_re = abar_re - 1.0
    n_im = abar_im
    den = lr * lr + li * li
    z_re = (n_re * lr + n_im * li) / den
    z_im = (n_im * lr - n_re * li) / den
    bre = bre_ref[...]
    bim = bim_ref[...]
    bbar_re = z_re * bre - z_im * bim
    bbar_im = z_re * bim + z_im * bre
    cre = cre_ref[...]
    cim = cim_ref[...]

    def power(d):
        m = jnp.exp((d * dt) * lr)
        ang = (d * dt) * li
        return m * jnp.cos(ang), m * jnp.sin(ang)

    def w_in_state(d):
        pr, pi = power(d)
        return jnp.concatenate([pr * bbar_re - pi * bbar_im, pr * bbar_im + pi * bbar_re],
                               axis=1).astype(BF16)

    def w_out_state(d):
        pr, pi = power(d)
        return jnp.concatenate([cre * pr - cim * pi, -(cre * pi + cim * pr)], axis=1).astype(BF16)

    wt0 = w_out_state(0)
    for d in range(L):
        wj = w_in_state(d)
        m_d = _dot_nt(wj, wt0).astype(BF16)
        for jj in range(L - d):
            tt = jj + d
            tbig_ref[jj * LANES:(jj + 1) * LANES, tt * LANES:(tt + 1) * LANES] = m_d
        wend_ref[(L - 1 - d) * LANES:(L - d) * LANES, :] = wj
        wc_ref[d * LANES:(d + 1) * LANES, :] = w_out_state(d + 1)
    for tt in range(0, L, 2):
        tbig_ref[(tt + 1) * LANES:(tt + 2) * LANES, tt * LANES:(tt + 1) * LANES] = (
            jnp.zeros((LANES, LANES), BF16))

    x_ref[...] = jnp.concatenate([u_ref[l] for l in range(L)], axis=1).astype(BF16)
    e_ref[...] = _dot(x_ref[...], wend_ref[...])

    row = lax.broadcasted_iota(jnp.int32, (SUBLANES, P), 0)
    steps = []
    for s in (1, 2, 4):
        pr, pi = power(L * s)
        steps.append((s, row >= s, pr, pi))
    carry_pw = [power(L * (r + 1)) for r in range(SUBLANES)]
    cpw_re = jnp.concatenate([c[0] for c in carry_pw], axis=0)
    cpw_im = jnp.concatenate([c[1] for c in carry_pw], axis=0)

    s_ref[0:SUBLANES, :] = jnp.zeros((SUBLANES, 2 * P), F32)

    def tile_step(i, carry):
        s_re, s_im = carry
        base = pl.multiple_of(i * SUBLANES, SUBLANES)
        e = e_ref[pl.ds(base, SUBLANES), :]
        t_re = e[:, :P]
        t_im = e[:, P:]
        for s, keep, pr, pi in steps:
            sh_re = jnp.where(keep, pltpu.roll(t_re, s, 0), 0.0)
            sh_im = jnp.where(keep, pltpu.roll(t_im, s, 0), 0.0)
            t_re, t_im = t_re + pr * sh_re - pi * sh_im, t_im + pr * sh_im + pi * sh_re
        f_re = t_re + cpw_re * s_re - cpw_im * s_im
        f_im = t_im + cpw_re * s_im + cpw_im * s_re
        s_ref[pl.ds(base + SUBLANES, SUBLANES), :] = jnp.concatenate([f_re, f_im], axis=1)
        return f_re[SUBLANES - 1:SUBLANES, :], f_im[SUBLANES - 1:SUBLANES, :]

    zero = jnp.zeros((1, P), F32)
    lax.fori_loop(0, nc // SUBLANES, tile_step, (zero, zero))

    s_start = s_ref[SUBLANES - 1:SUBLANES - 1 + nc, :].astype(BF16)
    dsk = dsk_ref[...]
    for b in range(L // 2):
        k_rows = (2 * b + 2) * LANES
        cols = slice(2 * b * LANES, (2 * b + 2) * LANES)
        y = _dot(x_ref[:, :k_rows], tbig_ref[:k_rows, cols]) + _dot_nt(s_start, wc_ref[cols, :])
        for l in (2 * b, 2 * b + 1):
            yl = y[:, (l - 2 * b) * LANES:(l - 2 * b + 1) * LANES] + dsk * u_ref[l]
            g_ref[l] = _gelu_tanh(yl).astype(BF16)


def _s5_ssm(u, a_re, a_im, log_dt, b_re, b_im, c_re, c_im, d_skip):
    L, nc, _ = u.shape
    P = S5_STATE_COLS
    gpt = S5_GROUPS_PER_TILE
    nt = N_LANE_TILES
    f = lambda v: v.astype(F32)
    row = lambda v: f(v).reshape(nt, 1, P)
    ldt = jnp.broadcast_to(f(log_dt)[:, None], (N_SSM_GROUPS, SSM_STATE))
    eye = jnp.eye(gpt, dtype=F32)

    def b_layout(b):
        bt = f(b).reshape(nt, gpt, SSM_STATE, SSM_GROUP)
        return jnp.einsum('kgpi,gh->kgihp', bt, eye).reshape(nt, LANES, P)

    def c_layout(c):
        ct = f(c).reshape(nt, gpt, SSM_GROUP, SSM_STATE)
        return jnp.einsum('kgip,gh->kgihp', ct, eye).reshape(nt, LANES, P)

    tile_row = pl.BlockSpec((None, 1, P), lambda k: (k, 0, 0))
    tile_mat = pl.BlockSpec((None, LANES, P), lambda k: (k, 0, 0))
    return pl.pallas_call(
        _s5_ssm_kernel,
        grid=(nt,),
        in_specs=[pl.BlockSpec((L, nc, LANES), lambda k: (0, 0, k)),
                  tile_row, tile_row, tile_row, tile_mat, tile_mat, tile_mat, tile_mat,
                  pl.BlockSpec((1, LANES), lambda k: (0, k))],
        out_specs=pl.BlockSpec((L, nc, LANES), lambda k: (0, 0, k)),
        out_shape=jax.ShapeDtypeStruct((L, nc, D_MODEL), BF16),
        scratch_shapes=[pltpu.VMEM((L * LANES, L * LANES), BF16),
                        pltpu.VMEM((L * LANES, 2 * P), BF16),
                        pltpu.VMEM((L * LANES, 2 * P), BF16),
                        pltpu.VMEM((nc, L * LANES), BF16),
                        pltpu.VMEM((nc, 2 * P), F32),
                        pltpu.VMEM((nc + SUBLANES, 2 * P), F32)],
        compiler_params=_params("arbitrary"),
        name="s5_ssm",
    )(u, row(a_re), row(a_im), row(ldt), b_layout(b_re), b_layout(b_im), c_layout(c_re),
      c_layout(c_im), f(d_skip).reshape(1, D_MODEL))


def _glu_epilogue(c, cols, accs, lhs_ref, extras, outs, y_ref):
    y = accs[0] * _sigmoid(accs[1])
    n = y.shape[0] // S5_CHUNK
    assert n <= S5_GLU_PITCH
    for k in range(y_ref.shape[0]):
        for l in range(S5_CHUNK):
            y_ref[k, l * S5_GLU_PITCH:l * S5_GLU_PITCH + n, :] = (
                y[l * n:(l + 1) * n, k * LANES:(k + 1) * LANES])
        lanes = slice(cols.start + k * LANES, cols.start + (k + 1) * LANES)
        for ch in range(n):
            rows = slice(ch * S5_CHUNK, (ch + 1) * S5_CHUNK)
            outs[0][rows, lanes] = (extras[0][rows, lanes]
                                    + y_ref[k, pl.ds(ch, S5_CHUNK, stride=S5_GLU_PITCH), :])


def _s5_glu(g, w_glu, h):
    L, nc, _ = g.shape
    t = nc * L
    tm = min(S5_TM, t)
    (out,) = _row_matmul(
        "s5_glu", t // tm, g, pl.BlockSpec((L, tm // L, D_MODEL), lambda i: (0, i, 0)), tm, w_glu,
        (0, D_MODEL), 512, D_MODEL // 512,
        extras=[(h, _rows(tm, D_MODEL))],
        outs=[(jax.ShapeDtypeStruct((t, D_MODEL), F32), _rows(tm, D_MODEL))],
        epilogue=_glu_epilogue,
        scratch=[pltpu.VMEM((512 // LANES, S5_CHUNK * S5_GLU_PITCH, LANES), F32)])
    return out


def _s5_layer(h, mix_norm, w_in, a_re, a_im, log_dt, b_re, b_im, c_re, c_im, d_skip, w_glu):
    u = _s5_in(h, mix_norm, w_in.astype(BF16))
    g = _s5_ssm(u, a_re, a_im, log_dt, b_re, b_im, c_re, c_im, d_skip)
    return _s5_glu(g, w_glu.astype(BF16), h)


def kernel(x, p, positions,
           l0_mix_norm, l0_w_qkv, l0_q_norm, l0_k_norm, l0_sinks, l0_w_o,
           l0_mlp_norm, l0_w_up, l0_w_down, l0_ple_norm, l0_w_ple_gate, l0_w_ple,
           l1_mix_norm, l1_w_in, l1_conv_w, l1_conv_b, l1_w_gate_a, l1_b_gate_a, l1_w_gate_x,
           l1_b_gate_x, l1_lru_lambda, l1_w_o,
           l1_mlp_norm, l1_w_up, l1_w_down, l1_ple_norm, l1_w_ple_gate, l1_w_ple,
           l2_mix_norm, l2_w_in, l2_a_re, l2_a_im, l2_log_dt, l2_b_re, l2_b_im, l2_c_re, l2_c_im,
           l2_d_skip, l2_w_glu,
           l2_mlp_norm, l2_w_up, l2_w_down, l2_ple_norm, l2_w_ple_gate, l2_w_ple,
           l3_mix_norm, l3_w_qkv, l3_q_norm, l3_k_norm, l3_sinks, l3_w_o,
           l3_mlp_norm, l3_w_up, l3_w_down, l3_ple_norm, l3_w_ple_gate, l3_w_ple):
    batch, t, _ = x.shape
    assert batch == 1 and t % (8 * BLOCK) == 0
    pos = positions.reshape(t).astype(jnp.int32)
    tails = [
        (l0_mlp_norm, l0_w_up, l0_w_down, l0_ple_norm, l0_w_ple_gate, l0_w_ple),
        (l1_mlp_norm, l1_w_up, l1_w_down, l1_ple_norm, l1_w_ple_gate, l1_w_ple),
        (l2_mlp_norm, l2_w_up, l2_w_down, l2_ple_norm, l2_w_ple_gate, l2_w_ple),
        (l3_mlp_norm, l3_w_up, l3_w_down, l3_ple_norm, l3_w_ple_gate, l3_w_ple),
    ]
    h = x.reshape(t, D_MODEL).astype(F32)
    p_all = p.reshape(DEPTH * t, PLE_DIM)
    for i in range(DEPTH):
        if i == 0:
            h, mlp16 = _attention_layer(h, pos, l0_mix_norm, l0_w_qkv, l0_q_norm, l0_k_norm, l0_sinks,
                                        l0_w_o, mlp_weights=(l0_w_up, l0_w_down))
        elif i == 1:
            h = _rglru_layer(h, l1_mix_norm, l1_w_in, l1_conv_w, l1_conv_b, l1_w_gate_a, l1_b_gate_a,
                             l1_w_gate_x, l1_b_gate_x, l1_lru_lambda, l1_w_o)
        elif i == 2:
            h = _s5_layer(h, l2_mix_norm, l2_w_in, l2_a_re, l2_a_im, l2_log_dt, l2_b_re, l2_b_im,
                          l2_c_re, l2_c_im, l2_d_skip, l2_w_glu)
        else:
            h, _ = _attention_layer(h, pos, l3_mix_norm, l3_w_qkv, l3_q_norm, l3_k_norm, l3_sinks, l3_w_o)
        mlp_norm, _, _, ple_norm, w_ple_gate, w_ple = tails[i]
        next_weights = tails[i + 1][1:3] if i + 1 < DEPTH else None
        h, mlp16 = _mlp(h, mlp_norm, *mlp16, next_weights=next_weights)
        h = _ple(h, ple_norm, p_all, i, w_ple.astype(BF16), w_ple_gate.astype(BF16))
    return h.reshape(batch, t, D_MODEL).astype(x.dtype)
```

```python
import functools
import math

import jax
import jax.numpy as jnp
from jax import lax
from jax.experimental import pallas as pl
from jax.experimental.pallas import tpu as pltpu

F32 = jnp.float32
BF16 = jnp.bfloat16

D_MODEL = 2048
DEPTH = 4
N_HEADS = 32
N_KV_HEADS = 4
HEAD_DIM = 64
Q_PER_KV = N_HEADS // N_KV_HEADS
WINDOW = 128
BLOCK = 128
PAD_POS = -(1 << 20)
N_GATE_BLOCKS = 8
GATE_BLOCK = D_MODEL // N_GATE_BLOCKS
CONV_WIDTH = 4
LRU_C = 8.0
SSM_GROUP = 16
N_SSM_GROUPS = D_MODEL // SSM_GROUP
SSM_STATE = 64
D_FF = 4 * D_MODEL
PLE_DIM = 256
EPS = 1e-6

LANES = 128
SUBLANES = 8
MXU_DIM = 256
VMEM_LIMIT_BYTES = 60 * 1024 * 1024
MASK_DIST = 1e33

S5_CHUNK = 16
S5_GROUPS_PER_TILE = LANES // SSM_GROUP
S5_STATE_COLS = S5_GROUPS_PER_TILE * SSM_STATE
N_LANE_TILES = D_MODEL // LANES


def _params(*semantics):
    return pltpu.CompilerParams(dimension_semantics=semantics,
                                vmem_limit_bytes=VMEM_LIMIT_BYTES)


def _rms(x, gain):
    ms = jnp.mean(x * x, axis=-1, keepdims=True)
    return x * lax.rsqrt(ms + EPS) * gain


def _gelu_tanh(x):
    return 0.5 * x * (1.0 + jnp.tanh(math.sqrt(2.0 / math.pi) * (x + 0.044715 * (x * x * x))))


def _sigmoid(x):
    return 0.5 * jnp.tanh(0.5 * x) + 0.5


def _dot(a, b):
    return jnp.dot(a, b, preferred_element_type=F32)


def _dot_nt(a, b):
    return lax.dot_general(a, b, (((1,), (1,)), ((), ())), preferred_element_type=F32)


def _resident(shape):
    zeros = (0,) * len(shape)
    return pl.BlockSpec(shape, lambda i: zeros, pipeline_mode=pl.Buffered(1))


def _row_matmul_kernel(*refs, n_extra, n_out, has_norm, col_offsets, tn, n_chunks, epilogue):
    lhs_ref = refs[0]
    pos = 1
    if has_norm:
        gain_ref = refs[1]
        pos = 2
    w_ref = refs[pos]
    pos += 1
    extra_refs = refs[pos:pos + n_extra]
    pos += n_extra
    out_refs = refs[pos:pos + n_out]
    pos += n_out
    if has_norm:
        src = refs[pos]
        pos += 1
        src[...] = _rms(lhs_ref[...], gain_ref[...]).astype(BF16)
    else:
        src = lhs_ref
    scratch_refs = refs[pos:]
    for c in range(n_chunks):
        cols = slice(c * tn, (c + 1) * tn)
        a = src[...]
        a = a.reshape(-1, a.shape[-1])
        accs = [_dot(a, w_ref[:, off + c * tn:off + (c + 1) * tn]) for off in col_offsets]
        epilogue(c, cols, accs, lhs_ref, extra_refs, out_refs, *scratch_refs)


def _row_matmul(name, n_blocks, lhs, lhs_spec, tm, w, col_offsets, tn, n_chunks, extras, outs,
                epilogue, gain=None, scratch=()):
    has_norm = gain is not None
    k_dim = w.shape[0]
    args = [lhs]
    in_specs = [lhs_spec]
    if has_norm:
        args.append(gain.reshape(1, k_dim).astype(F32))
        in_specs.append(_resident((1, k_dim)))
    args.append(w)
    in_specs.append(_resident(w.shape))
    for a, s in extras:
        args.append(a)
        in_specs.append(s)
    kern = functools.partial(_row_matmul_kernel, n_extra=len(extras), n_out=len(outs),
                             has_norm=has_norm, col_offsets=tuple(col_offsets), tn=tn,
                             n_chunks=n_chunks, epilogue=epilogue)
    scratch = ([pltpu.VMEM((tm, k_dim), BF16)] if has_norm else []) + list(scratch)
    return pl.pallas_call(
        kern,
        grid=(n_blocks,),
        in_specs=in_specs,
        out_specs=[s for _, s in outs],
        out_shape=[o for o, _ in outs],
        scratch_shapes=scratch,
        compiler_params=_params("parallel"),
        name=name,
    )(*args)


def _rows(tm, width):
    return pl.BlockSpec((tm, width), lambda i: (i, 0))


QKV_TN = 512
QKV_COLS = D_MODEL + 2 * N_KV_HEADS * LANES
QKV_NORM_BLOCKS = (D_MODEL + N_KV_HEADS * LANES) // QKV_TN
LOG2E = math.log2(math.e)


ATTN_GROUP = 256
ATTN_GROUPS_PER_STEP = 2
K_COL = D_MODEL
V_COL = D_MODEL + N_KV_HEADS * LANES


def _interleave(*task_lists):
    keyed = []
    for n_list, tasks in enumerate(task_lists):
        for k, task in enumerate(tasks):
            keyed.append(((k + 0.5) / len(tasks), n_list, k, task))
    keyed.sort(key=lambda item: item[:3])
    return [item[3] for item in keyed]


def _qkv_tasks(hn_ref, wp_ref, gain_ref, seg_ref, qkv_ref, rows):
    def chunk(c):
        cols = slice(c * QKV_TN, (c + 1) * QKV_TN)
        x = _dot(hn_ref[rows, :], wp_ref[:, cols])
        if c < QKV_NORM_BLOCKS:
            x2 = (x * x).astype(BF16)
            ss = jnp.concatenate(
                [_dot(x2[:, k * MXU_DIM:(k + 1) * MXU_DIM], seg_ref[...])
                 for k in range(QKV_TN // MXU_DIM)], axis=1)
            r = lax.rsqrt(ss * (1.0 / HEAD_DIM) + EPS)
            qkv_ref[rows, cols] = (x * r * gain_ref[:, cols]).astype(BF16)
        else:
            qkv_ref[rows, cols] = x.astype(BF16)

    return [functools.partial(chunk, c) for c in range(QKV_COLS // QKV_TN)]


def _alibi_slopes():
    return [float(2.0 ** (-8.0 * (i + 1) / N_HEADS)) for i in range(N_HEADS)]


def _attend_tasks(qkv_ref, rows, k_prev, v_prev, positions, sinks_ref, o_ref):
    shared = {}

    def prepare():
        qpos, kpos_prev, kpos_cur = positions()
        kpos = jnp.concatenate([kpos_prev, kpos_cur], axis=1)
        dist = qpos - kpos
        valid = (dist >= 0) & (dist < WINDOW)
        lane = lax.broadcasted_iota(jnp.int32, (BLOCK, LANES), 1)
        lo_f = (lane < HEAD_DIM).astype(F32)
        lo = lo_f.astype(BF16)
        hi = (1.0 - lo_f).astype(BF16)
        shared.update(distm=jnp.where(valid, dist.astype(F32), MASK_DIST), lane=lane, lo=lo, hi=hi,
                      lo2=jnp.concatenate([lo, lo], axis=0), hi2=jnp.concatenate([hi, hi], axis=0))

    slopes = _alibi_slopes()

    def head_task(h):
        if not shared:
            prepare()
        distm, lane, lo, hi = shared["distm"], shared["lane"], shared["lo"], shared["hi"]
        lo2, hi2 = shared["lo2"], shared["hi2"]
        kk = jnp.concatenate([k_prev(h), qkv_ref[rows, K_COL + h * LANES:K_COL + (h + 1) * LANES]],
                             axis=0)
        vd = jnp.concatenate([v_prev(h), qkv_ref[rows, V_COL + h * LANES:V_COL + (h + 1) * LANES]],
                             axis=0)
        qs = []
        for g in range(Q_PER_KV):
            col = h * Q_PER_KV * HEAD_DIM + (g // 2) * LANES
            qp = qkv_ref[rows, col:col + LANES]
            qs.append(qp * (lo if g % 2 == 0 else hi))
        s_all = _dot_nt(jnp.concatenate(qs, axis=0), kk)
        ps = []
        rls = []
        for g in range(Q_PER_KV):
            head = h * Q_PER_KV + g
            sg = s_all[g * BLOCK:(g + 1) * BLOCK] - (slopes[head] * LOG2E) * distm
            sink = sinks_ref[head] * LOG2E
            m = jnp.maximum(jnp.max(sg, axis=-1, keepdims=True), sink)
            p = jnp.exp2(sg - m)
            denom = jnp.sum(p, axis=-1, keepdims=True) + jnp.exp2(sink - m)
            ps.append(p.astype(BF16))
            rls.append(1.0 / denom)
        lhs2 = jnp.concatenate(
            [jnp.concatenate([ps[2 * p], ps[2 * p + 1]], axis=1) for p in range(Q_PER_KV // 2)],
            axis=0)
        rhs2 = jnp.concatenate([vd * lo2, vd * hi2], axis=0)
        o2 = _dot(lhs2, rhs2)
        for p in range(Q_PER_KV // 2):
            scale = jnp.where(lane < HEAD_DIM, rls[2 * p], rls[2 * p + 1])
            col = h * Q_PER_KV * HEAD_DIM + p * LANES
            o_ref[rows, col:col + LANES] = (o2[p * BLOCK:(p + 1) * BLOCK] * scale).astype(BF16)

    return [functools.partial(head_task, h) for h in range(N_KV_HEADS)]


def _attn_layer_kernel(h_ref, gain_ref, wp_ref, hg_ref, seg_ref, wo_ref, qpos_ref, kposp_ref,
                       kpos_ref, sinks_ref, *rest, cast_mlp):
    if cast_mlp:
        nwu_ref, nwd_ref, out_ref, owu_ref, owd_ref, hn_ref, qkv_ref, o_ref, tail_ref = rest
        owu_ref[...] = nwu_ref[...].astype(BF16)
        owd_ref[...] = nwd_ref[...].astype(BF16)
    else:
        out_ref, hn_ref, qkv_ref, o_ref, tail_ref = rest
    first = pl.program_id(0) == 0

    @pl.when(first)
    def _():
        tail_ref[...] = jnp.zeros(tail_ref.shape, BF16)

    hn_ref[...] = _rms(h_ref[...], gain_ref[...]).astype(BF16)
    kv_width = N_KV_HEADS * LANES
    step_rows = h_ref.shape[0]
    n_groups = step_rows // ATTN_GROUP
    bpg = ATTN_GROUP // BLOCK
    project, attend, emit = [], [], []
    for g in range(n_groups):
        grows = slice(g * ATTN_GROUP, (g + 1) * ATTN_GROUP)
        project.append(_qkv_tasks(hn_ref, wp_ref, hg_ref, seg_ref, qkv_ref, grows))

        def out_chunk(c, grows=grows):
            cols = slice(c * QKV_TN, (c + 1) * QKV_TN)
            out_ref[grows, cols] = h_ref[grows, cols] + _dot(o_ref[grows, :], wo_ref[:, cols])

        emit.append([functools.partial(out_chunk, c) for c in range(D_MODEL // QKV_TN)])
    for b in range(step_rows // BLOCK):
        rows = slice(b * BLOCK, (b + 1) * BLOCK)
        if b == 0:
            k_prev = lambda h: tail_ref[:, h * LANES:(h + 1) * LANES]
            v_prev = lambda h: tail_ref[:, kv_width + h * LANES:kv_width + (h + 1) * LANES]
            positions = lambda rows=rows: (qpos_ref[rows, :],
                                           jnp.where(first, PAD_POS, kposp_ref[...]), kpos_ref[0])
        else:
            prows = slice((b - 1) * BLOCK, b * BLOCK)
            k_prev = lambda h, prows=prows: qkv_ref[prows, K_COL + h * LANES:K_COL + (h + 1) * LANES]
            v_prev = lambda h, prows=prows: qkv_ref[prows, V_COL + h * LANES:V_COL + (h + 1) * LANES]
            positions = lambda rows=rows, b=b: (qpos_ref[rows, :], kpos_ref[b - 1], kpos_ref[b])
        attend.append(_attend_tasks(qkv_ref, rows, k_prev, v_prev, positions, sinks_ref, o_ref))

    for task in project[0]:
        task()
    for g in range(n_groups):
        softmax = [t for b in range(g * bpg, (g + 1) * bpg) for t in attend[b]]
        nxt = project[g + 1] if g + 1 < n_groups else []
        prv = emit[g - 1] if g > 0 else []
        for task in _interleave(softmax, nxt, prv):
            task()
    last = slice(step_rows - BLOCK, step_rows)
    tail_ref[...] = qkv_ref[last, K_COL:K_COL + 2 * kv_width]
    for task in emit[n_groups - 1]:
        task()


def _residual_epilogue(c, cols, accs, lhs_ref, extras, outs):
    outs[0][:, cols] = extras[0][:, cols] + accs[0]


def _proj_residual(a, w, h):
    t, k_dim = a.shape
    tm = min(1024, t)
    (out,) = _row_matmul(
        "proj_residual", t // tm, a, _rows(tm, k_dim), tm, w, (0,), 512, D_MODEL // 512,
        extras=[(h, _rows(tm, D_MODEL))],
        outs=[(jax.ShapeDtypeStruct((t, D_MODEL), F32), _rows(tm, D_MODEL))],
        epilogue=_residual_epilogue)
    return out


def _prep_qkv_weight(w_qkv):
    wq = w_qkv[:, :D_MODEL]
    wk = w_qkv[:, D_MODEL:D_MODEL + N_KV_HEADS * HEAD_DIM].reshape(D_MODEL, N_KV_HEADS, HEAD_DIM)
    wv = w_qkv[:, D_MODEL + N_KV_HEADS * HEAD_DIM:].reshape(D_MODEL, N_KV_HEADS, HEAD_DIM)
    dup = lambda w: jnp.concatenate([w, w], axis=-1).reshape(D_MODEL, N_KV_HEADS * LANES)
    return jnp.concatenate([wq, dup(wk), dup(wv)], axis=1).astype(BF16)


def _attention_layer(h, positions, mix_norm, w_qkv, q_norm, k_norm, sinks, w_o, mlp_weights=None):
    head_gain = jnp.concatenate([
        jnp.tile(q_norm.astype(F32), N_HEADS) * (HEAD_DIM ** -0.5 * LOG2E),
        jnp.tile(k_norm.astype(F32), 2 * N_KV_HEADS),
        jnp.ones((N_KV_HEADS * LANES,), F32)]).reshape(1, QKV_COLS)
    seg_ones = jnp.kron(jnp.eye(MXU_DIM // HEAD_DIM, dtype=F32),
                        jnp.ones((HEAD_DIM, HEAD_DIM), F32)).astype(BF16)
    t = h.shape[0]
    rows = ATTN_GROUP * (1 if mlp_weights is not None else ATTN_GROUPS_PER_STEP)
    n_steps = t // rows
    bps = rows // BLOCK
    pos_row = positions.reshape(t // BLOCK, 1, BLOCK)
    in_specs = [
        _rows(rows, D_MODEL),
        _resident((1, D_MODEL)),
        _resident((D_MODEL, QKV_COLS)),
        _resident((1, QKV_COLS)),
        _resident((MXU_DIM, MXU_DIM)),
        _resident((D_MODEL, D_MODEL)),
        pl.BlockSpec((rows, 1), lambda i: (i, 0)),
        pl.BlockSpec((None, 1, BLOCK), lambda i: (jnp.maximum(bps * i - 1, 0), 0, 0)),
        pl.BlockSpec((bps, 1, BLOCK), lambda i: (i, 0, 0)),
        pl.BlockSpec(memory_space=pltpu.SMEM),
    ]
    out_specs = [_rows(rows, D_MODEL)]
    out_shape = [jax.ShapeDtypeStruct((t, D_MODEL), F32)]
    args = [h, mix_norm.reshape(1, D_MODEL).astype(F32), _prep_qkv_weight(w_qkv), head_gain, seg_ones,
            w_o.astype(BF16), positions.reshape(t, 1), pos_row, pos_row, sinks.astype(F32)]
    if mlp_weights is not None:
        specs, shapes = _cast_specs(n_steps, lambda i: i)
        in_specs += specs
        out_specs += specs
        out_shape += shapes
        args += [w.astype(F32) for w in mlp_weights]
    res = pl.pallas_call(
        functools.partial(_attn_layer_kernel, cast_mlp=mlp_weights is not None),
        grid=(n_steps,),
        in_specs=in_specs,
        out_specs=out_specs,
        out_shape=out_shape,
        scratch_shapes=[pltpu.VMEM((rows, D_MODEL), BF16), pltpu.VMEM((rows, QKV_COLS), BF16),
                        pltpu.VMEM((rows, D_MODEL), BF16),
                        pltpu.VMEM((BLOCK, 2 * N_KV_HEADS * LANES), BF16)],
        compiler_params=_params("arbitrary"),
        name="attn_layer",
    )(*args)
    return res[0], tuple(res[1:])


MLP_TF = 512


def _cast_specs(n_steps, step_index):
    ru = D_MODEL // n_steps
    rd = D_FF // n_steps
    assert ru % 16 == 0 and ru * n_steps == D_MODEL
    specs = [pl.BlockSpec((ru, D_FF), lambda *g: (step_index(*g), 0)),
             pl.BlockSpec((rd, D_MODEL), lambda *g: (step_index(*g), 0))]
    shapes = [jax.ShapeDtypeStruct((D_MODEL, D_FF), BF16), jax.ShapeDtypeStruct((D_FF, D_MODEL), BF16)]
    return specs, shapes


def _mlp_kernel(h_ref, gain_ref, wu_ref, wd_ref, *rest, cast_next):
    if cast_next:
        nwu_ref, nwd_ref, o_ref, owu_ref, owd_ref, hn_ref = rest
        owu_ref[...] = nwu_ref[...].astype(BF16)
        owd_ref[...] = nwd_ref[...].astype(BF16)
    else:
        o_ref, hn_ref = rest
    j = pl.program_id(1)

    @pl.when(j == 0)
    def _():
        x = h_ref[...]
        hn_ref[...] = _rms(x, gain_ref[...]).astype(BF16)
        o_ref[...] = x

    act = jnp.maximum(_dot(hn_ref[...], wu_ref[...]), 0.0)
    act = (act * act).astype(BF16)
    o_ref[...] += _dot(act, wd_ref[...])


def _mlp(h, gain, w_up16, w_down16, next_weights=None):
    t = h.shape[0]
    tm = min(1024, t)
    tf = MLP_TF
    nj = D_FF // tf
    in_specs = [
        pl.BlockSpec((tm, D_MODEL), lambda i, j: (i, 0)),
        pl.BlockSpec((1, D_MODEL), lambda i, j: (0, 0)),
        pl.BlockSpec((D_MODEL, tf), lambda i, j: (0, j)),
        pl.BlockSpec((tf, D_MODEL), lambda i, j: (j, 0)),
    ]
    out_specs = [pl.BlockSpec((tm, D_MODEL), lambda i, j: (i, 0))]
    out_shape = [jax.ShapeDtypeStruct((t, D_MODEL), F32)]
    args = [h, gain.reshape(1, D_MODEL).astype(F32), w_up16, w_down16]
    if next_weights is not None:
        specs, shapes = _cast_specs((t // tm) * nj, lambda i, j: i * nj + j)
        in_specs += specs
        out_specs += specs
        out_shape += shapes
        args += [w.astype(F32) for w in next_weights]
    res = pl.pallas_call(
        functools.partial(_mlp_kernel, cast_next=next_weights is not None),
        grid=(t // tm, nj),
        in_specs=in_specs,
        out_specs=out_specs,
        out_shape=out_shape,
        scratch_shapes=[pltpu.VMEM((tm, D_MODEL), BF16)],
        compiler_params=_params("parallel", "arbitrary"),
        name="mlp",
    )(*args)
    return res[0], tuple(res[1:])


def _ple_epilogue(c, cols, accs, lhs_ref, extras, outs):
    p_ref, wple_ref = extras
    emb = _dot(p_ref[...].astype(BF16), wple_ref[:, cols])
    outs[0][:, cols] = lhs_ref[:, cols] + emb * _sigmoid(accs[0])


def _ple(h, gain, p_all, layer, w_ple, w_gate):
    t = h.shape[0]
    tm = min(1024, t)
    first = layer * (t // tm)
    (out,) = _row_matmul(
        "ple", t // tm, h, _rows(tm, D_MODEL), tm, w_gate, (0,), 512, D_MODEL // 512,
        extras=[(p_all, pl.BlockSpec((tm, PLE_DIM), lambda i: (first + i, 0))),
                (w_ple, _resident((PLE_DIM, D_MODEL)))],
        outs=[(jax.ShapeDtypeStruct((t, D_MODEL), F32), _rows(tm, D_MODEL))],
        epilogue=_ple_epilogue, gain=gain)
    return out


def _rglru_in_epilogue(c, cols, accs, lhs_ref, extras, outs):
    outs[0][:, cols] = accs[0]
    outs[1][:, cols] = _gelu_tanh(accs[1]).astype(BF16)


def _rglru_in(h, mix_norm, w_in):
    t = h.shape[0]
    tm = min(512, t)
    return _row_matmul(
        "rglru_in", t // tm, h, _rows(tm, D_MODEL), tm, w_in, (0, D_MODEL), 512, D_MODEL // 512,
        extras=[],
        outs=[(jax.ShapeDtypeStruct((t, D_MODEL), F32), _rows(tm, D_MODEL)),
              (jax.ShapeDtypeStruct((t, D_MODEL), BF16), _rows(tm, D_MODEL))],
        epilogue=_rglru_in_epilogue, gain=mix_norm)


RGLRU_TB = 256
RGLRU_SEG = RGLRU_TB // SUBLANES
RGLRU_PITCH = 36
TINY = 1e-36


def _rglru_scan_kernel(xb_ref, gate_ref, cw_ref, cb_ref, wa_ref, ba_ref, wx_ref, bx_ref, lam_ref,
                       y_ref, a_ref, b_ref, carry_ref, tail_ref):
    tb = RGLRU_TB
    seg = RGLRU_SEG
    n_slabs = D_MODEL // LANES

    @pl.when(pl.program_id(0) == 0)
    def _():
        carry_ref[...] = jnp.zeros((1, D_MODEL), F32)
        tail_ref[...] = jnp.zeros((CONV_WIDTH - 1, D_MODEL), F32)

    def shift1(v, k):
        prev_last = tail_ref[k:k + 1, :]
        tail_ref[k:k + 1, :] = v[tb - 1:tb, :]
        return jnp.concatenate([prev_last, v[:tb - 1, :]], axis=0)

    x = xb_ref[...]
    acc = cw_ref[0:1, :] * x
    for k in range(1, CONV_WIDTH):
        acc = cw_ref[k:k + 1, :] * x + shift1(acc, k - 1)
    xc = acc + cb_ref[...]

    lam = lam_ref[...]
    rate = (-LRU_C) * (jnp.log(1.0 + jnp.exp(-jnp.abs(lam))) + jnp.maximum(-lam, 0.0))
    for n in range(N_GATE_BLOCKS):
        sl = slice(n * GATE_BLOCK, (n + 1) * GATE_BLOCK)
        xn = xc[:, sl]
        xn16 = xn.astype(BF16)
        r = _sigmoid(_dot(xn16, wa_ref[n]) + ba_ref[:, sl])
        i_gate = _sigmoid(_dot(xn16, wx_ref[n]) + bx_ref[:, sl])
        a = jnp.exp(rate[:, sl] * r)
        v = 1.0 - a * a
        b = (v * lax.rsqrt(jnp.maximum(v, TINY))) * (i_gate * xn)
        for kk in range(GATE_BLOCK // LANES):
            slab = n * (GATE_BLOCK // LANES) + kk
            for s in range(SUBLANES):
                rows = slice(s * RGLRU_PITCH, s * RGLRU_PITCH + seg)
                a_ref[slab, rows, :] = a[s * seg:(s + 1) * seg, kk * LANES:(kk + 1) * LANES]
                b_ref[slab, rows, :] = b[s * seg:(s + 1) * seg, kk * LANES:(kk + 1) * LANES]

    def step(q, carry):
        hs, prods = carry
        rows = pl.ds(q, SUBLANES, stride=RGLRU_PITCH)
        new_hs = []
        new_prods = []
        for k in range(n_slabs):
            av = a_ref[k, rows, :]
            hk = av * hs[k] + b_ref[k, rows, :]
            pk = av * prods[k]
            b_ref[k, rows, :] = hk
            a_ref[k, rows, :] = pk
            new_hs.append(hk)
            new_prods.append(pk)
        return tuple(new_hs), tuple(new_prods)

    zeros = tuple(jnp.zeros((SUBLANES, LANES), F32) for _ in range(n_slabs))
    ones = tuple(jnp.ones((SUBLANES, LANES), F32) for _ in range(n_slabs))
    hs, prods = lax.fori_loop(0, seg, step, (zeros, ones))

    for k in range(n_slabs):
        lanes = slice(k * LANES, (k + 1) * LANES)
        cur = carry_ref[:, lanes]
        for s in range(SUBLANES):
            rows = slice(s * RGLRU_PITCH, s * RGLRU_PITCH + seg)
            hfull = b_ref[k, rows, :] + a_ref[k, rows, :] * cur
            out_rows = slice(s * seg, (s + 1) * seg)
            y_ref[out_rows, lanes] = (hfull * gate_ref[out_rows, lanes].astype(F32)).astype(BF16)
            cur = prods[k][s:s + 1, :] * cur + hs[k][s:s + 1, :]
        carry_ref[:, lanes] = cur


def _rglru_scan(xb, gate, conv_w, conv_b, w_gate_a, b_gate_a, w_gate_x, b_gate_x, lru_lambda):
    t = xb.shape[0]
    tb = RGLRU_TB
    row = lambda v: v.reshape(1, D_MODEL).astype(F32)
    return pl.pallas_call(
        _rglru_scan_kernel,
        grid=(t // tb,),
        in_specs=[
            _rows(tb, D_MODEL), _rows(tb, D_MODEL),
            _resident((CONV_WIDTH, D_MODEL)), _resident((1, D_MODEL)),
            _resident((N_GATE_BLOCKS, GATE_BLOCK, GATE_BLOCK)), _resident((1, D_MODEL)),
            _resident((N_GATE_BLOCKS, GATE_BLOCK, GATE_BLOCK)), _resident((1, D_MODEL)),
            _resident((1, D_MODEL)),
        ],
        out_specs=_rows(tb, D_MODEL),
        out_shape=jax.ShapeDtypeStruct((t, D_MODEL), BF16),
        scratch_shapes=[pltpu.VMEM((D_MODEL // LANES, SUBLANES * RGLRU_PITCH, LANES), F32),
                        pltpu.VMEM((D_MODEL // LANES, SUBLANES * RGLRU_PITCH, LANES), F32),
                        pltpu.VMEM((1, D_MODEL), F32),
                        pltpu.VMEM((CONV_WIDTH - 1, D_MODEL), F32)],
        compiler_params=_params("arbitrary"),
        name="rglru_scan",
    )(xb, gate, conv_w.astype(F32), row(conv_b), w_gate_a.astype(BF16), row(b_gate_a),
      w_gate_x.astype(BF16), row(b_gate_x), row(lru_lambda))


def _rglru_layer(h, mix_norm, w_in, conv_w, conv_b, w_gate_a, b_gate_a, w_gate_x, b_gate_x,
                 lru_lambda, w_o):
    xb, gate = _rglru_in(h, mix_norm, w_in.astype(BF16))
    y = _rglru_scan(xb, gate, conv_w, conv_b, w_gate_a, b_gate_a, w_gate_x, b_gate_x, lru_lambda)
    return _proj_residual(y, w_o.astype(BF16), h)


S5_TM = 512
S5_IN_PITCH = 20
S5_GLU_PITCH = 36


def _s5_in_epilogue(c, cols, accs, lhs_ref, extras, outs, acc_ref):
    n = accs[0].shape[0] // S5_CHUNK
    for k in range(acc_ref.shape[0]):
        for ch in range(n):
            acc_ref[k, ch * S5_IN_PITCH:ch * S5_IN_PITCH + S5_CHUNK, :] = (
                accs[0][ch * S5_CHUNK:(ch + 1) * S5_CHUNK, k * LANES:(k + 1) * LANES])
        for l in range(S5_CHUNK):
            outs[0][l, :, cols.start + k * LANES:cols.start + (k + 1) * LANES] = (
                acc_ref[k, pl.ds(l, n, stride=S5_IN_PITCH), :])


def _s5_in(h, mix_norm, w_in):
    t = h.shape[0]
    nc = t // S5_CHUNK
    tm = min(S5_TM, t)
    tn = 512
    (u,) = _row_matmul(
        "s5_in", t // tm, h, _rows(tm, D_MODEL), tm, w_in, (0,), tn, D_MODEL // tn,
        extras=[],
        outs=[(jax.ShapeDtypeStruct((S5_CHUNK, nc, D_MODEL), F32),
               pl.BlockSpec((S5_CHUNK, tm // S5_CHUNK, D_MODEL), lambda i: (0, i, 0)))],
        epilogue=_s5_in_epilogue, gain=mix_norm,
        scratch=[pltpu.VMEM((tn // LANES, (tm // S5_CHUNK) * S5_IN_PITCH, LANES), F32)])
    return u


def _s5_ssm_kernel(u_ref, lr_ref, li_ref, ldt_ref, bre_ref, bim_ref, cre_ref, cim_ref, dsk_ref,
                   g_ref, tbig_ref, wend_ref, wc_ref, x_ref, e_ref, s_ref):
    nc = u_ref.shape[1]
    L = S5_CHUNK
    P = S5_STATE_COLS
    lr = lr_ref[...]
    li = li_ref[...]
    dt = jnp.exp(ldt_ref[...])
    mag = jnp.exp(lr * dt)
    abar_re = mag * jnp.cos(li * dt)
    abar_im = mag * jnp.sin(li * dt)
    n_re = abar_re - 1.0
    n_im = abar_im
    den = lr * lr + li * li
    z_re = (n_re * lr + n_im * li) / den
    z_im = (n_im * lr - n_re * li) / den
    bre = bre_ref[...]
    bim = bim_ref[...]
    bbar_re = z_re * bre - z_im * bim
    bbar_im = z_re * bim + z_im * bre
    cre = cre_ref[...]
    cim = cim_ref[...]

    def power(d):
        m = jnp.exp((d * dt) * lr)
        ang = (d * dt) * li
        return m * jnp.cos(ang), m * jnp.sin(ang)

    def w_in_state(d):
        pr, pi = power(d)
        return jnp.concatenate([pr * bbar_re - pi * bbar_im, pr * bbar_im + pi * bbar_re],
                               axis=1).astype(BF16)

    def w_out_state(d):
        pr, pi = power(d)
        return jnp.concatenate([cre * pr - cim * pi, -(cre * pi + cim * pr)], axis=1).astype(BF16)

    wt0 = w_out_state(0)
    for d in range(L):
        wj = w_in_state(d)
        m_d = _dot_nt(wj, wt0).astype(BF16)
        for jj in range(L - d):
            tt = jj + d
            tbig_ref[jj * LANES:(jj + 1) * LANES, tt * LANES:(tt + 1) * LANES] = m_d
        wend_ref[(L - 1 - d) * LANES:(L - d) * LANES, :] = wj
        wc_ref[d * LANES:(d + 1) * LANES, :] = w_out_state(d + 1)
    for tt in range(0, L, 2):
        tbig_ref[(tt + 1) * LANES:(tt + 2) * LANES, tt * LANES:(tt + 1) * LANES] = (
            jnp.zeros((LANES, LANES), BF16))

    x_ref[...] = jnp.concatenate([u_ref[l] for l in range(L)], axis=1).astype(BF16)
    e_ref[...] = _dot(x_ref[...], wend_ref[...])

    row = lax.broadcasted_iota(jnp.int32, (SUBLANES, P), 0)
    steps = []
    for s in (1, 2, 4):
        pr, pi = power(L * s)
        steps.append((s, row >= s, pr, pi))
    carry_pw = [power(L * (r + 1)) for r in range(SUBLANES)]
    cpw_re = jnp.concatenate([c[0] for c in carry_pw], axis=0)
    cpw_im = jnp.concatenate([c[1] for c in carry_pw], axis=0)

    s_ref[0:SUBLANES, :] = jnp.zeros((SUBLANES, 2 * P), F32)

    def tile_step(i, carry):
        s_re, s_im = carry
        base = pl.multiple_of(i * SUBLANES, SUBLANES)
        e = e_ref[pl.ds(base, SUBLANES), :]
        t_re = e[:, :P]
        t_im = e[:, P:]
        for s, keep, pr, pi in steps:
            sh_re = jnp.where(keep, pltpu.roll(t_re, s, 0), 0.0)
            sh_im = jnp.where(keep, pltpu.roll(t_im, s, 0), 0.0)
            t_re, t_im = t_re + pr * sh_re - pi * sh_im, t_im + pr * sh_im + pi * sh_re
        f_re = t_re + cpw_re * s_re - cpw_im * s_im
        f_im = t_im + cpw_re * s_im + cpw_im * s_re
        s_ref[pl.ds(base + SUBLANES, SUBLANES), :] = jnp.concatenate([f_re, f_im], axis=1)
        return f_re[SUBLANES - 1:SUBLANES, :], f_im[SUBLANES - 1:SUBLANES, :]

    zero = jnp.zeros((1, P), F32)
    lax.fori_loop(0, nc // SUBLANES, tile_step, (zero, zero))

    s_start = s_ref[SUBLANES - 1:SUBLANES - 1 + nc, :].astype(BF16)
    dsk = dsk_ref[...]
    for b in range(L // 2):
        k_rows = (2 * b + 2) * LANES
        cols = slice(2 * b * LANES, (2 * b + 2) * LANES)
        y = _dot(x_ref[:, :k_rows], tbig_ref[:k_rows, cols]) + _dot_nt(s_start, wc_ref[cols, :])
        for l in (2 * b, 2 * b + 1):
            yl = y[:, (l - 2 * b) * LANES:(l - 2 * b + 1) * LANES] + dsk * u_ref[l]
            g_ref[l] = _gelu_tanh(yl).astype(BF16)


def _s5_ssm(u, a_re, a_im, log_dt, b_re, b_im, c_re, c_im, d_skip):
    L, nc, _ = u.shape
    P = S5_STATE_COLS
    gpt = S5_GROUPS_PER_TILE
    nt = N_LANE_TILES
    f = lambda v: v.astype(F32)
    row = lambda v: f(v).reshape(nt, 1, P)
    ldt = jnp.broadcast_to(f(log_dt)[:, None], (N_SSM_GROUPS, SSM_STATE))
    eye = jnp.eye(gpt, dtype=F32)

    def b_layout(b):
        bt = f(b).reshape(nt, gpt, SSM_STATE, SSM_GROUP)
        return jnp.einsum('kgpi,gh->kgihp', bt, eye).reshape(nt, LANES, P)

    def c_layout(c):
        ct = f(c).reshape(nt, gpt, SSM_GROUP, SSM_STATE)
        return jnp.einsum('kgip,gh->kgihp', ct, eye).reshape(nt, LANES, P)

    tile_row = pl.BlockSpec((None, 1, P), lambda k: (k, 0, 0))
    tile_mat = pl.BlockSpec((None, LANES, P), lambda k: (k, 0, 0))
    return pl.pallas_call(
        _s5_ssm_kernel,
        grid=(nt,),
        in_specs=[pl.BlockSpec((L, nc, LANES), lambda k: (0, 0, k)),
                  tile_row, tile_row, tile_row, tile_mat, tile_mat, tile_mat, tile_mat,
                  pl.BlockSpec((1, LANES), lambda k: (0, k))],
        out_specs=pl.BlockSpec((L, nc, LANES), lambda k: (0, 0, k)),
        out_shape=jax.ShapeDtypeStruct((L, nc, D_MODEL), BF16),
        scratch_shapes=[pltpu.VMEM((L * LANES, L * LANES), BF16),
                        pltpu.VMEM((L * LANES, 2 * P), BF16),
                        pltpu.VMEM((L * LANES, 2 * P), BF16),
                        pltpu.VMEM((nc, L * LANES), BF16),
                        pltpu.VMEM((nc, 2 * P), F32),
                        pltpu.VMEM((nc + SUBLANES, 2 * P), F32)],
        compiler_params=_params("arbitrary"),
        name="s5_ssm",
    )(u, row(a_re), row(a_im), row(ldt), b_layout(b_re), b_layout(b_im), c_layout(c_re),
      c_layout(c_im), f(d_skip).reshape(1, D_MODEL))


def _glu_epilogue(c, cols, accs, lhs_ref, extras, outs, y_ref):
    y = accs[0] * _sigmoid(accs[1])
    n = y.shape[0] // S5_CHUNK
    assert n <= S5_GLU_PITCH
    for k in range(y_ref.shape[0]):
        for l in range(S5_CHUNK):
            y_ref[k, l * S5_GLU_PITCH:l * S5_GLU_PITCH + n, :] = (
                y[l * n:(l + 1) * n, k * LANES:(k + 1) * LANES])
        lanes = slice(cols.start + k * LANES, cols.start + (k + 1) * LANES)
        for ch in range(n):
            rows = slice(ch * S5_CHUNK, (ch + 1) * S5_CHUNK)
            outs[0][rows, lanes] = (extras[0][rows, lanes]
                                    + y_ref[k, pl.ds(ch, S5_CHUNK, stride=S5_GLU_PITCH), :])


def _s5_glu(g, w_glu, h):
    L, nc, _ = g.shape
    t = nc * L
    tm = min(S5_TM, t)
    (out,) = _row_matmul(
        "s5_glu", t // tm, g, pl.BlockSpec((L, tm // L, D_MODEL), lambda i: (0, i, 0)), tm, w_glu,
        (0, D_MODEL), 512, D_MODEL // 512,
        extras=[(h, _rows(tm, D_MODEL))],
        outs=[(jax.ShapeDtypeStruct((t, D_MODEL), F32), _rows(tm, D_MODEL))],
        epilogue=_glu_epilogue,
        scratch=[pltpu.VMEM((512 // LANES, S5_CHUNK * S5_GLU_PITCH, LANES), F32)])
    return out


def _s5_layer(h, mix_norm, w_in, a_re, a_im, log_dt, b_re, b_im, c_re, c_im, d_skip, w_glu):
    u = _s5_in(h, mix_norm, w_in.astype(BF16))
    g = _s5_ssm(u, a_re, a_im, log_dt, b_re, b_im, c_re, c_im, d_skip)
    return _s5_glu(g, w_glu.astype(BF16), h)


def kernel(x, p, positions,
           l0_mix_norm, l0_w_qkv, l0_q_norm, l0_k_norm, l0_sinks, l0_w_o,
           l0_mlp_norm, l0_w_up, l0_w_down, l0_ple_norm, l0_w_ple_gate, l0_w_ple,
           l1_mix_norm, l1_w_in, l1_conv_w, l1_conv_b, l1_w_gate_a, l1_b_gate_a, l1_w_gate_x,
           l1_b_gate_x, l1_lru_lambda, l1_w_o,
           l1_mlp_norm, l1_w_up, l1_w_down, l1_ple_norm, l1_w_ple_gate, l1_w_ple,
           l2_mix_norm, l2_w_in, l2_a_re, l2_a_im, l2_log_dt, l2_b_re, l2_b_im, l2_c_re, l2_c_im,
           l2_d_skip, l2_w_glu,
           l2_mlp_norm, l2_w_up, l2_w_down, l2_ple_norm, l2_w_ple_gate, l2_w_ple,
           l3_mix_norm, l3_w_qkv, l3_q_norm, l3_k_norm, l3_sinks, l3_w_o,
           l3_mlp_norm, l3_w_up, l3_w_down, l3_ple_norm, l3_w_ple_gate, l3_w_ple):
    batch, t, _ = x.shape
    assert batch == 1 and t % (8 * BLOCK) == 0
    pos = positions.reshape(t).astype(jnp.int32)
    tails = [
        (l0_mlp_norm, l0_w_up, l0_w_down, l0_ple_norm, l0_w_ple_gate, l0_w_ple),
        (l1_mlp_norm, l1_w_up, l1_w_down, l1_ple_norm, l1_w_ple_gate, l1_w_ple),
        (l2_mlp_norm, l2_w_up, l2_w_down, l2_ple_norm, l2_w_ple_gate, l2_w_ple),
        (l3_mlp_norm, l3_w_up, l3_w_down, l3_ple_norm, l3_w_ple_gate, l3_w_ple),
    ]
    h = x.reshape(t, D_MODEL).astype(F32)
    p_all = p.reshape(DEPTH * t, PLE_DIM)
    for i in range(DEPTH):
        if i == 0:
            h, mlp16 = _attention_layer(h, pos, l0_mix_norm, l0_w_qkv, l0_q_norm, l0_k_norm, l0_sinks,
                                        l0_w_o, mlp_weights=(l0_w_up, l0_w_down))
        elif i == 1:
            h = _rglru_layer(h, l1_mix_norm, l1_w_in, l1_conv_w, l1_conv_b, l1_w_gate_a, l1_b_gate_a,
                             l1_w_gate_x, l1_b_gate_x, l1_lru_lambda, l1_w_o)
        elif i == 2:
            h = _s5_layer(h, l2_mix_norm, l2_w_in, l2_a_re, l2_a_im, l2_log_dt, l2_b_re, l2_b_im,
                          l2_c_re, l2_c_im, l2_d_skip, l2_w_glu)
        else:
            h, _ = _attention_layer(h, pos, l3_mix_norm, l3_w_qkv, l3_q_norm, l3_k_norm, l3_sinks, l3_w_o)
        mlp_norm, _, _, ple_norm, w_ple_gate, w_ple = tails[i]
        next_weights = tails[i + 1][1:3] if i + 1 < DEPTH else None
        h, mlp16 = _mlp(h, mlp_norm, *mlp16, next_weights=next_weights)
        h = _ple(h, ple_norm, p_all, i, w_ple.astype(BF16), w_ple_gate.astype(BF16))
    return h.reshape(batch, t, D_MODEL).astype(x.dtype)
```

```python
import functools
import math

import jax
import jax.numpy as jnp
from jax import lax
from jax.experimental import pallas as pl
from jax.experimental.pallas import tpu as pltpu

F32 = jnp.float32
BF16 = jnp.bfloat16

D_MODEL = 2048
DEPTH = 4
N_HEADS = 32
N_KV_HEADS = 4
HEAD_DIM = 64
Q_PER_KV = N_HEADS // N_KV_HEADS
WINDOW = 128
BLOCK = 128
PAD_POS = -(1 << 20)
N_GATE_BLOCKS = 8
GATE_BLOCK = D_MODEL // N_GATE_BLOCKS
CONV_WIDTH = 4
LRU_C = 8.0
SSM_GROUP = 16
N_SSM_GROUPS = D_MODEL // SSM_GROUP
SSM_STATE = 64
D_FF = 4 * D_MODEL
PLE_DIM = 256
EPS = 1e-6

LANES = 128
SUBLANES = 8
MXU_DIM = 256
VMEM_LIMIT_BYTES = 60 * 1024 * 1024
MASK_DIST = 1e33

S5_CHUNK = 16
S5_GROUPS_PER_TILE = LANES // SSM_GROUP
S5_STATE_COLS = S5_GROUPS_PER_TILE * SSM_STATE
N_LANE_TILES = D_MODEL // LANES


def _params(*semantics):
    return pltpu.CompilerParams(dimension_semantics=semantics,
                                vmem_limit_bytes=VMEM_LIMIT_BYTES)


def _rms(x, gain):
    ms = jnp.mean(x * x, axis=-1, keepdims=True)
    return x * lax.rsqrt(ms + EPS) * gain


def _gelu_tanh(x):
    return 0.5 * x * (1.0 + jnp.tanh(math.sqrt(2.0 / math.pi) * (x + 0.044715 * (x * x * x))))


def _sigmoid(x):
    return 0.5 * jnp.tanh(0.5 * x) + 0.5


def _dot(a, b):
    return jnp.dot(a, b, preferred_element_type=F32)


def _dot_nt(a, b):
    return lax.dot_general(a, b, (((1,), (1,)), ((), ())), preferred_element_type=F32)


def _resident(shape):
    zeros = (0,) * len(shape)
    return pl.BlockSpec(shape, lambda i: zeros, pipeline_mode=pl.Buffered(1))


def _row_matmul_kernel(*refs, n_extra, n_out, has_norm, col_offsets, tn, n_chunks, epilogue):
    lhs_ref = refs[0]
    pos = 1
    if has_norm:
        gain_ref = refs[1]
        pos = 2
    w_ref = refs[pos]
    pos += 1
    extra_refs = refs[pos:pos + n_extra]
    pos += n_extra
    out_refs = refs[pos:pos + n_out]
    pos += n_out
    if has_norm:
        src = refs[pos]
        pos += 1
        src[...] = _rms(lhs_ref[...], gain_ref[...]).astype(BF16)
    else:
        src = lhs_ref
    scratch_refs = refs[pos:]
    for c in range(n_chunks):
        cols = slice(c * tn, (c + 1) * tn)
        a = src[...]
        a = a.reshape(-1, a.shape[-1])
        accs = [_dot(a, w_ref[:, off + c * tn:off + (c + 1) * tn]) for off in col_offsets]
        epilogue(c, cols, accs, lhs_ref, extra_refs, out_refs, *scratch_refs)


def _row_matmul(name, n_blocks, lhs, lhs_spec, tm, w, col_offsets, tn, n_chunks, extras, outs,
                epilogue, gain=None, scratch=()):
    has_norm = gain is not None
    k_dim = w.shape[0]
    args = [lhs]
    in_specs = [lhs_spec]
    if has_norm:
        args.append(gain.reshape(1, k_dim).astype(F32))
        in_specs.append(_resident((1, k_dim)))
    args.append(w)
    in_specs.append(_resident(w.shape))
    for a, s in extras:
        args.append(a)
        in_specs.append(s)
    kern = functools.partial(_row_matmul_kernel, n_extra=len(extras), n_out=len(outs),
                             has_norm=has_norm, col_offsets=tuple(col_offsets), tn=tn,
                             n_chunks=n_chunks, epilogue=epilogue)
    scratch = ([pltpu.VMEM((tm, k_dim), BF16)] if has_norm else []) + list(scratch)
    return pl.pallas_call(
        kern,
        grid=(n_blocks,),
        in_specs=in_specs,
        out_specs=[s for _, s in outs],
        out_shape=[o for o, _ in outs],
        scratch_shapes=scratch,
        compiler_params=_params("parallel"),
        name=name,
    )(*args)


def _rows(tm, width):
    return pl.BlockSpec((tm, width), lambda i: (i, 0))


QKV_TN = 512
QKV_COLS = D_MODEL + 2 * N_KV_HEADS * LANES
QKV_NORM_BLOCKS = (D_MODEL + N_KV_HEADS * LANES) // QKV_TN
LOG2E = math.log2(math.e)


ATTN_GROUP = 256
ATTN_GROUPS_PER_STEP = 2
K_COL = D_MODEL
V_COL = D_MODEL + N_KV_HEADS * LANES


def _interleave(*task_lists):
    keyed = []
    for n_list, tasks in enumerate(task_lists):
        for k, task in enumerate(tasks):
            keyed.append(((k + 0.5) / len(tasks), n_list, k, task))
    keyed.sort(key=lambda item: item[:3])
    return [item[3] for item in keyed]


def _qkv_tasks(hn_ref, wp_ref, gain_ref, seg_ref, qkv_ref, rows):
    def chunk(c):
        cols = slice(c * QKV_TN, (c + 1) * QKV_TN)
        x = _dot(hn_ref[rows, :], wp_ref[:, cols])
        if c < QKV_NORM_BLOCKS:
            x2 = (x * x).astype(BF16)
            ss = jnp.concatenate(
                [_dot(x2[:, k * MXU_DIM:(k + 1) * MXU_DIM], seg_ref[...])
                 for k in range(QKV_TN // MXU_DIM)], axis=1)
            r = lax.rsqrt(ss * (1.0 / HEAD_DIM) + EPS)
            qkv_ref[rows, cols] = (x * r * gain_ref[:, cols]).astype(BF16)
        else:
            qkv_ref[rows, cols] = x.astype(BF16)

    return [functools.partial(chunk, c) for c in range(QKV_COLS // QKV_TN)]


def _alibi_slopes():
    return [float(2.0 ** (-8.0 * (i + 1) / N_HEADS)) for i in range(N_HEADS)]


def _attend_tasks(qkv_ref, rows, k_prev, v_prev, positions, sinks_ref, o_ref):
    shared = {}

    def prepare():
        qpos, kpos_prev, kpos_cur = positions()
        kpos = jnp.concatenate([kpos_prev, kpos_cur], axis=1)
        dist = qpos - kpos
        valid = (dist >= 0) & (dist < WINDOW)
        lane = lax.broadcasted_iota(jnp.int32, (BLOCK, LANES), 1)
        lo_f = (lane < HEAD_DIM).astype(F32)
        lo = lo_f.astype(BF16)
        hi = (1.0 - lo_f).astype(BF16)
        shared.update(distm=jnp.where(valid, dist.astype(F32), MASK_DIST), lane=lane, lo=lo, hi=hi,
                      lo2=jnp.concatenate([lo, lo], axis=0), hi2=jnp.concatenate([hi, hi], axis=0))

    slopes = _alibi_slopes()

    def head_task(h):
        if not shared:
            prepare()
        distm, lane, lo, hi = shared["distm"], shared["lane"], shared["lo"], shared["hi"]
        lo2, hi2 = shared["lo2"], shared["hi2"]
        kk = jnp.concatenate([k_prev(h), qkv_ref[rows, K_COL + h * LANES:K_COL + (h + 1) * LANES]],
                             axis=0)
        vd = jnp.concatenate([v_prev(h), qkv_ref[rows, V_COL + h * LANES:V_COL + (h + 1) * LANES]],
                             axis=0)
        qs = []
        for g in range(Q_PER_KV):
            col = h * Q_PER_KV * HEAD_DIM + (g // 2) * LANES
            qp = qkv_ref[rows, col:col + LANES]
            qs.append(qp * (lo if g % 2 == 0 else hi))
        s_all = _dot_nt(jnp.concatenate(qs, axis=0), kk)
        ps = []
        rls = []
        for g in range(Q_PER_KV):
            head = h * Q_PER_KV + g
            sg = s_all[g * BLOCK:(g + 1) * BLOCK] - (slopes[head] * LOG2E) * distm
            sink = sinks_ref[head] * LOG2E
            m = jnp.maximum(jnp.max(sg, axis=-1, keepdims=True), sink)
            p = jnp.exp2(sg - m)
            denom = jnp.sum(p, axis=-1, keepdims=True) + jnp.exp2(sink - m)
            ps.append(p.astype(BF16))
            rls.append(1.0 / denom)
        lhs2 = jnp.concatenate(
            [jnp.concatenate([ps[2 * p], ps[2 * p + 1]], axis=1) for p in range(Q_PER_KV // 2)],
            axis=0)
        rhs2 = jnp.concatenate([vd * lo2, vd * hi2], axis=0)
        o2 = _dot(lhs2, rhs2)
        for p in range(Q_PER_KV // 2):
            scale = jnp.where(lane < HEAD_DIM, rls[2 * p], rls[2 * p + 1])
            col = h * Q_PER_KV * HEAD_DIM + p * LANES
            o_ref[rows, col:col + LANES] = (o2[p * BLOCK:(p + 1) * BLOCK] * scale).astype(BF16)

    return [functools.partial(head_task, h) for h in range(N_KV_HEADS)]


def _attn_layer_kernel(h_ref, gain_ref, wp_ref, hg_ref, seg_ref, wo_ref, qpos_ref, kposp_ref,
                       kpos_ref, sinks_ref, *rest, cast_mlp):
    if cast_mlp:
        nwu_ref, nwd_ref, out_ref, owu_ref, owd_ref, hn_ref, qkv_ref, o_ref, tail_ref = rest
        owu_ref[...] = nwu_ref[...].astype(BF16)
        owd_ref[...] = nwd_ref[...].astype(BF16)
    else:
        out_ref, hn_ref, qkv_ref, o_ref, tail_ref = rest
    first = pl.program_id(0) == 0

    @pl.when(first)
    def _():
        tail_ref[...] = jnp.zeros(tail_ref.shape, BF16)

    hn_ref[...] = _rms(h_ref[...], gain_ref[...]).astype(BF16)
    kv_width = N_KV_HEADS * LANES
    step_rows = h_ref.shape[0]
    n_groups = step_rows // ATTN_GROUP
    bpg = ATTN_GROUP // BLOCK
    project, attend, emit = [], [], []
    for g in range(n_groups):
        grows = slice(g * ATTN_GROUP, (g + 1) * ATTN_GROUP)
        project.append(_qkv_tasks(hn_ref, wp_ref, hg_ref, seg_ref, qkv_ref, grows))

        def out_chunk(c, grows=grows):
            cols = slice(c * QKV_TN, (c + 1) * QKV_TN)
            out_ref[grows, cols] = h_ref[grows, cols] + _dot(o_ref[grows, :], wo_ref[:, cols])

        emit.append([functools.partial(out_chunk, c) for c in range(D_MODEL // QKV_TN)])
    for b in range(step_rows // BLOCK):
        rows = slice(b * BLOCK, (b + 1) * BLOCK)
        if b == 0:
            k_prev = lambda h: tail_ref[:, h * LANES:(h + 1) * LANES]
            v_prev = lambda h: tail_ref[:, kv_width + h * LANES:kv_width + (h + 1) * LANES]
            positions = lambda rows=rows: (qpos_ref[rows, :],
                                           jnp.where(first, PAD_POS, kposp_ref[...]), kpos_ref[0])
        else:
            prows = slice((b - 1) * BLOCK, b * BLOCK)
            k_prev = lambda h, prows=prows: qkv_ref[prows, K_COL + h * LANES:K_COL + (h + 1) * LANES]
            v_prev = lambda h, prows=prows: qkv_ref[prows, V_COL + h * LANES:V_COL + (h + 1) * LANES]
            positions = lambda rows=rows, b=b: (qpos_ref[rows, :], kpos_ref[b - 1], kpos_ref[b])
        attend.append(_attend_tasks(qkv_ref, rows, k_prev, v_prev, positions, sinks_ref, o_ref))

    for task in project[0]:
        task()
    for g in range(n_groups):
        softmax = [t for b in range(g * bpg, (g + 1) * bpg) for t in attend[b]]
        nxt = project[g + 1] if g + 1 < n_groups else []
        prv = emit[g - 1] if g > 0 else []
        for task in _interleave(softmax, nxt, prv):
            task()
    last = slice(step_rows - BLOCK, step_rows)
    tail_ref[...] = qkv_ref[last, K_COL:K_COL + 2 * kv_width]
    for task in emit[n_groups - 1]:
        task()


def _residual_epilogue(c, cols, accs, lhs_ref, extras, outs):
    outs[0][:, cols] = extras[0][:, cols] + accs[0]


def _proj_residual(a, w, h):
    t, k_dim = a.shape
    tm = min(1024, t)
    (out,) = _row_matmul(
        "proj_residual", t // tm, a, _rows(tm, k_dim), tm, w, (0,), 512, D_MODEL // 512,
        extras=[(h, _rows(tm, D_MODEL))],
        outs=[(jax.ShapeDtypeStruct((t, D_MODEL), F32), _rows(tm, D_MODEL))],
        epilogue=_residual_epilogue)
    return out


def _prep_qkv_weight(w_qkv):
    w16 = w_qkv.astype(BF16)
    wq = w16[:, :D_MODEL]
    wk = w16[:, D_MODEL:D_MODEL + N_KV_HEADS * HEAD_DIM].reshape(D_MODEL, N_KV_HEADS, HEAD_DIM)
    wv = w16[:, D_MODEL + N_KV_HEADS * HEAD_DIM:].reshape(D_MODEL, N_KV_HEADS, HEAD_DIM)
    dup = lambda w: jnp.concatenate([w, w], axis=-1).reshape(D_MODEL, N_KV_HEADS * LANES)
    return jnp.concatenate([wq, dup(wk), dup(wv)], axis=1)


def _attention_layer(h, positions, mix_norm, w_qkv, q_norm, k_norm, sinks, w_o, mlp_weights=None):
    head_gain = jnp.concatenate([
        jnp.tile(q_norm.astype(F32), N_HEADS) * (HEAD_DIM ** -0.5 * LOG2E),
        jnp.tile(k_norm.astype(F32), 2 * N_KV_HEADS),
        jnp.ones((N_KV_HEADS * LANES,), F32)]).reshape(1, QKV_COLS)
    seg_ones = jnp.kron(jnp.eye(MXU_DIM // HEAD_DIM, dtype=F32),
                        jnp.ones((HEAD_DIM, HEAD_DIM), F32)).astype(BF16)
    t = h.shape[0]
    rows = ATTN_GROUP * (1 if mlp_weights is not None else ATTN_GROUPS_PER_STEP)
    n_steps = t // rows
    bps = rows // BLOCK
    pos_row = positions.reshape(t // BLOCK, 1, BLOCK)
    in_specs = [
        _rows(rows, D_MODEL),
        _resident((1, D_MODEL)),
        _resident((D_MODEL, QKV_COLS)),
        _resident((1, QKV_COLS)),
        _resident((MXU_DIM, MXU_DIM)),
        _resident((D_MODEL, D_MODEL)),
        pl.BlockSpec((rows, 1), lambda i: (i, 0)),
        pl.BlockSpec((None, 1, BLOCK), lambda i: (jnp.maximum(bps * i - 1, 0), 0, 0)),
        pl.BlockSpec((bps, 1, BLOCK), lambda i: (i, 0, 0)),
        pl.BlockSpec(memory_space=pltpu.SMEM),
    ]
    out_specs = [_rows(rows, D_MODEL)]
    out_shape = [jax.ShapeDtypeStruct((t, D_MODEL), F32)]
    args = [h, mix_norm.reshape(1, D_MODEL).astype(F32), _prep_qkv_weight(w_qkv), head_gain, seg_ones,
            w_o.astype(BF16), positions.reshape(t, 1), pos_row, pos_row, sinks.astype(F32)]
    if mlp_weights is not None:
        specs, shapes = _cast_specs(n_steps, lambda i: i)
        in_specs += specs
        out_specs += specs
        out_shape += shapes
        args += [w.astype(F32) for w in mlp_weights]
    res = pl.pallas_call(
        functools.partial(_attn_layer_kernel, cast_mlp=mlp_weights is not None),
        grid=(n_steps,),
        in_specs=in_specs,
        out_specs=out_specs,
        out_shape=out_shape,
        scratch_shapes=[pltpu.VMEM((rows, D_MODEL), BF16), pltpu.VMEM((rows, QKV_COLS), BF16),
                        pltpu.VMEM((rows, D_MODEL), BF16),
                        pltpu.VMEM((BLOCK, 2 * N_KV_HEADS * LANES), BF16)],
        compiler_params=_params("arbitrary"),
        name="attn_layer",
    )(*args)
    return res[0], tuple(res[1:])


MLP_TF = 512


def _cast_specs(n_steps, step_index):
    ru = D_MODEL // n_steps
    rd = D_FF // n_steps
    assert ru % 16 == 0 and ru * n_steps == D_MODEL
    specs = [pl.BlockSpec((ru, D_FF), lambda *g: (step_index(*g), 0)),
             pl.BlockSpec((rd, D_MODEL), lambda *g: (step_index(*g), 0))]
    shapes = [jax.ShapeDtypeStruct((D_MODEL, D_FF), BF16), jax.ShapeDtypeStruct((D_FF, D_MODEL), BF16)]
    return specs, shapes


def _mlp_kernel(h_ref, gain_ref, wu_ref, wd_ref, *rest, cast_next):
    if cast_next:
        nwu_ref, nwd_ref, o_ref, owu_ref, owd_ref, hn_ref = rest
        owu_ref[...] = nwu_ref[...].astype(BF16)
        owd_ref[...] = nwd_ref[...].astype(BF16)
    else:
        o_ref, hn_ref = rest
    j = pl.program_id(1)

    @pl.when(j == 0)
    def _():
        x = h_ref[...]
        hn_ref[...] = _rms(x, gain_ref[...]).astype(BF16)
        o_ref[...] = x

    act = jnp.maximum(_dot(hn_ref[...], wu_ref[...]), 0.0)
    act = (act * act).astype(BF16)
    o_ref[...] += _dot(act, wd_ref[...])


def _mlp(h, gain, w_up16, w_down16, next_weights=None):
    t = h.shape[0]
    tm = min(1024, t)
    tf = MLP_TF
    nj = D_FF // tf
    in_specs = [
        pl.BlockSpec((tm, D_MODEL), lambda i, j: (i, 0)),
        pl.BlockSpec((1, D_MODEL), lambda i, j: (0, 0)),
        pl.BlockSpec((D_MODEL, tf), lambda i, j: (0, j)),
        pl.BlockSpec((tf, D_MODEL), lambda i, j: (j, 0)),
    ]
    out_specs = [pl.BlockSpec((tm, D_MODEL), lambda i, j: (i, 0))]
    out_shape = [jax.ShapeDtypeStruct((t, D_MODEL), F32)]
    args = [h, gain.reshape(1, D_MODEL).astype(F32), w_up16, w_down16]
    if next_weights is not None:
        specs, shapes = _cast_specs((t // tm) * nj, lambda i, j: i * nj + j)
        in_specs += specs
        out_specs += specs
        out_shape += shapes
        args += [w.astype(F32) for w in next_weights]
    res = pl.pallas_call(
        functools.partial(_mlp_kernel, cast_next=next_weights is not None),
        grid=(t // tm, nj),
        in_specs=in_specs,
        out_specs=out_specs,
        out_shape=out_shape,
        scratch_shapes=[pltpu.VMEM((tm, D_MODEL), BF16)],
        compiler_params=_params("parallel", "arbitrary"),
        name="mlp",
    )(*args)
    return res[0], tuple(res[1:])


def _ple_epilogue(c, cols, accs, lhs_ref, extras, outs):
    p_ref, wple_ref = extras
    emb = _dot(p_ref[...].astype(BF16), wple_ref[:, cols])
    outs[0][:, cols] = lhs_ref[:, cols] + emb * _sigmoid(accs[0])


def _ple(h, gain, p_all, layer, w_ple, w_gate):
    t = h.shape[0]
    tm = min(1024, t)
    first = layer * (t // tm)
    (out,) = _row_matmul(
        "ple", t // tm, h, _rows(tm, D_MODEL), tm, w_gate, (0,), 512, D_MODEL // 512,
        extras=[(p_all, pl.BlockSpec((tm, PLE_DIM), lambda i: (first + i, 0))),
                (w_ple, _resident((PLE_DIM, D_MODEL)))],
        outs=[(jax.ShapeDtypeStruct((t, D_MODEL), F32), _rows(tm, D_MODEL))],
        epilogue=_ple_epilogue, gain=gain)
    return out


def _rglru_in_epilogue(c, cols, accs, lhs_ref, extras, outs):
    outs[0][:, cols] = accs[0]
    outs[1][:, cols] = _gelu_tanh(accs[1]).astype(BF16)


def _rglru_in(h, mix_norm, w_in):
    t = h.shape[0]
    tm = min(512, t)
    return _row_matmul(
        "rglru_in", t // tm, h, _rows(tm, D_MODEL), tm, w_in, (0, D_MODEL), 512, D_MODEL // 512,
        extras=[],
        outs=[(jax.ShapeDtypeStruct((t, D_MODEL), F32), _rows(tm, D_MODEL)),
              (jax.ShapeDtypeStruct((t, D_MODEL), BF16), _rows(tm, D_MODEL))],
        epilogue=_rglru_in_epilogue, gain=mix_norm)


RGLRU_TB = 256
RGLRU_SEG = RGLRU_TB // SUBLANES
RGLRU_PITCH = 36
TINY = 1e-36


def _rglru_scan_kernel(xb_ref, gate_ref, cw_ref, cb_ref, wa_ref, ba_ref, wx_ref, bx_ref, lam_ref,
                       y_ref, a_ref, b_ref, carry_ref, tail_ref):
    tb = RGLRU_TB
    seg = RGLRU_SEG
    n_slabs = D_MODEL // LANES

    @pl.when(pl.program_id(0) == 0)
    def _():
        carry_ref[...] = jnp.zeros((1, D_MODEL), F32)
        tail_ref[...] = jnp.zeros((CONV_WIDTH - 1, D_MODEL), F32)

    def shift1(v, k):
        prev_last = tail_ref[k:k + 1, :]
        tail_ref[k:k + 1, :] = v[tb - 1:tb, :]
        return jnp.concatenate([prev_last, v[:tb - 1, :]], axis=0)

    x = xb_ref[...]
    acc = cw_ref[0:1, :] * x
    for k in range(1, CONV_WIDTH):
        acc = cw_ref[k:k + 1, :] * x + shift1(acc, k - 1)
    xc = acc + cb_ref[...]

    lam = lam_ref[...]
    rate = (-LRU_C) * (jnp.log(1.0 + jnp.exp(-jnp.abs(lam))) + jnp.maximum(-lam, 0.0))
    for n in range(N_GATE_BLOCKS):
        sl = slice(n * GATE_BLOCK, (n + 1) * GATE_BLOCK)
        xn = xc[:, sl]
        xn16 = xn.astype(BF16)
        r = _sigmoid(_dot(xn16, wa_ref[n]) + ba_ref[:, sl])
        i_gate = _sigmoid(_dot(xn16, wx_ref[n]) + bx_ref[:, sl])
        a = jnp.exp(rate[:, sl] * r)
        v = 1.0 - a * a
        b = (v * lax.rsqrt(jnp.maximum(v, TINY))) * (i_gate * xn)
        for kk in range(GATE_BLOCK // LANES):
            slab = n * (GATE_BLOCK // LANES) + kk
            for s in range(SUBLANES):
                rows = slice(s * RGLRU_PITCH, s * RGLRU_PITCH + seg)
                a_ref[slab, rows, :] = a[s * seg:(s + 1) * seg, kk * LANES:(kk + 1) * LANES]
                b_ref[slab, rows, :] = b[s * seg:(s + 1) * seg, kk * LANES:(kk + 1) * LANES]

    def step(q, carry):
        hs, prods = carry
        rows = pl.ds(q, SUBLANES, stride=RGLRU_PITCH)
        new_hs = []
        new_prods = []
        for k in range(n_slabs):
            av = a_ref[k, rows, :]
            hk = av * hs[k] + b_ref[k, rows, :]
            pk = av * prods[k]
            b_ref[k, rows, :] = hk
            a_ref[k, rows, :] = pk
            new_hs.append(hk)
            new_prods.append(pk)
        return tuple(new_hs), tuple(new_prods)

    zeros = tuple(jnp.zeros((SUBLANES, LANES), F32) for _ in range(n_slabs))
    ones = tuple(jnp.ones((SUBLANES, LANES), F32) for _ in range(n_slabs))
    hs, prods = lax.fori_loop(0, seg, step, (zeros, ones))

    for k in range(n_slabs):
        lanes = slice(k * LANES, (k + 1) * LANES)
        cur = carry_ref[:, lanes]
        for s in range(SUBLANES):
            rows = slice(s * RGLRU_PITCH, s * RGLRU_PITCH + seg)
            hfull = b_ref[k, rows, :] + a_ref[k, rows, :] * cur
            out_rows = slice(s * seg, (s + 1) * seg)
            y_ref[out_rows, lanes] = (hfull * gate_ref[out_rows, lanes].astype(F32)).astype(BF16)
            cur = prods[k][s:s + 1, :] * cur + hs[k][s:s + 1, :]
        carry_ref[:, lanes] = cur


def _rglru_scan(xb, gate, conv_w, conv_b, w_gate_a, b_gate_a, w_gate_x, b_gate_x, lru_lambda):
    t = xb.shape[0]
    tb = RGLRU_TB
    row = lambda v: v.reshape(1, D_MODEL).astype(F32)
    return pl.pallas_call(
        _rglru_scan_kernel,
        grid=(t // tb,),
        in_specs=[
            _rows(tb, D_MODEL), _rows(tb, D_MODEL),
            _resident((CONV_WIDTH, D_MODEL)), _resident((1, D_MODEL)),
            _resident((N_GATE_BLOCKS, GATE_BLOCK, GATE_BLOCK)), _resident((1, D_MODEL)),
            _resident((N_GATE_BLOCKS, GATE_BLOCK, GATE_BLOCK)), _resident((1, D_MODEL)),
            _resident((1, D_MODEL)),
        ],
        out_specs=_rows(tb, D_MODEL),
        out_shape=jax.ShapeDtypeStruct((t, D_MODEL), BF16),
        scratch_shapes=[pltpu.VMEM((D_MODEL // LANES, SUBLANES * RGLRU_PITCH, LANES), F32),
                        pltpu.VMEM((D_MODEL // LANES, SUBLANES * RGLRU_PITCH, LANES), F32),
                        pltpu.VMEM((1, D_MODEL), F32),
                        pltpu.VMEM((CONV_WIDTH - 1, D_MODEL), F32)],
        compiler_params=_params("arbitrary"),
        name="rglru_scan",
    )(xb, gate, conv_w.astype(F32), row(conv_b), w_gate_a.astype(BF16), row(b_gate_a),
      w_gate_x.astype(BF16), row(b_gate_x), row(lru_lambda))


def _rglru_layer_kernel(h_ref, gain_ref, win_ref, cw_ref, cb_ref, wa_ref, ba_ref, wx_ref, bx_ref,
                        lam_ref, wo_ref, out_ref, hn_ref, xb_ref, gate_ref, y_ref, a_ref, b_ref,
                        carry_ref, tail_ref):
    tb = RGLRU_TB
    seg = RGLRU_SEG
    n_slabs = D_MODEL // LANES
    spb = GATE_BLOCK // LANES

    @pl.when(pl.program_id(0) == 0)
    def _():
        carry_ref[...] = jnp.zeros((1, D_MODEL), F32)
        tail_ref[...] = jnp.zeros((CONV_WIDTH - 1, D_MODEL), F32)

    hn_ref[...] = _rms(h_ref[...], gain_ref[...]).astype(BF16)
    lam = lam_ref[...]
    rate = (-LRU_C) * (jnp.log(1.0 + jnp.exp(-jnp.abs(lam))) + jnp.maximum(-lam, 0.0))

    def project(n):
        sl = slice(n * GATE_BLOCK, (n + 1) * GATE_BLOCK)
        xb_ref[:, sl] = _dot(hn_ref[...], win_ref[:, sl])
        gate_ref[:, sl] = _gelu_tanh(
            _dot(hn_ref[...], win_ref[:, D_MODEL + n * GATE_BLOCK:D_MODEL + (n + 1) * GATE_BLOCK])
        ).astype(BF16)

    def gates(n):
        sl = slice(n * GATE_BLOCK, (n + 1) * GATE_BLOCK)

        def shift1(v, k):
            prev_last = tail_ref[k:k + 1, sl]
            tail_ref[k:k + 1, sl] = v[tb - 1:tb, :]
            return jnp.concatenate([prev_last, v[:tb - 1, :]], axis=0)

        x = xb_ref[:, sl]
        acc = cw_ref[0:1, sl] * x
        for k in range(1, CONV_WIDTH):
            acc = cw_ref[k:k + 1, sl] * x + shift1(acc, k - 1)
        xn = acc + cb_ref[:, sl]
        xn16 = xn.astype(BF16)
        r = _sigmoid(_dot(xn16, wa_ref[n]) + ba_ref[:, sl])
        i_gate = _sigmoid(_dot(xn16, wx_ref[n]) + bx_ref[:, sl])
        a = jnp.exp(rate[:, sl] * r)
        v = 1.0 - a * a
        b = (v * lax.rsqrt(jnp.maximum(v, TINY))) * (i_gate * xn)
        for kk in range(spb):
            slab = n * spb + kk
            for s in range(SUBLANES):
                rows = slice(s * RGLRU_PITCH, s * RGLRU_PITCH + seg)
                a_ref[slab, rows, :] = a[s * seg:(s + 1) * seg, kk * LANES:(kk + 1) * LANES]
                b_ref[slab, rows, :] = b[s * seg:(s + 1) * seg, kk * LANES:(kk + 1) * LANES]

    project(0)
    for n in range(N_GATE_BLOCKS):
        if n + 1 < N_GATE_BLOCKS:
            project(n + 1)
        gates(n)

    def step(q, carry):
        hs, prods = carry
        rows = pl.ds(q, SUBLANES, stride=RGLRU_PITCH)
        new_hs = []
        new_prods = []
        for k in range(n_slabs):
            av = a_ref[k, rows, :]
            hk = av * hs[k] + b_ref[k, rows, :]
            pk = av * prods[k]
            b_ref[k, rows, :] = hk
            a_ref[k, rows, :] = pk
            new_hs.append(hk)
            new_prods.append(pk)
        return tuple(new_hs), tuple(new_prods)

    zeros = tuple(jnp.zeros((SUBLANES, LANES), F32) for _ in range(n_slabs))
    ones = tuple(jnp.ones((SUBLANES, LANES), F32) for _ in range(n_slabs))
    hs, prods = lax.fori_loop(0, seg, step, (zeros, ones))

    for k in range(n_slabs):
        lanes = slice(k * LANES, (k + 1) * LANES)
        cur = carry_ref[:, lanes]
        for s in range(SUBLANES):
            rows = slice(s * RGLRU_PITCH, s * RGLRU_PITCH + seg)
            hfull = b_ref[k, rows, :] + a_ref[k, rows, :] * cur
            out_rows = slice(s * seg, (s + 1) * seg)
            y_ref[out_rows, lanes] = (hfull * gate_ref[out_rows, lanes].astype(F32)).astype(BF16)
            cur = prods[k][s:s + 1, :] * cur + hs[k][s:s + 1, :]
        carry_ref[:, lanes] = cur

    for c in range(D_MODEL // 512):
        cols = slice(c * 512, (c + 1) * 512)
        out_ref[:, cols] = h_ref[:, cols] + _dot(y_ref[...], wo_ref[:, cols])


def _rglru_layer(h, mix_norm, w_in, conv_w, conv_b, w_gate_a, b_gate_a, w_gate_x, b_gate_x,
                 lru_lambda, w_o):
    t = h.shape[0]
    tb = RGLRU_TB
    row = lambda v: v.reshape(1, D_MODEL).astype(F32)
    scan_scratch = pltpu.VMEM((D_MODEL // LANES, SUBLANES * RGLRU_PITCH, LANES), F32)
    return pl.pallas_call(
        _rglru_layer_kernel,
        grid=(t // tb,),
        in_specs=[
            _rows(tb, D_MODEL), _resident((1, D_MODEL)), _resident((D_MODEL, 2 * D_MODEL)),
            _resident((CONV_WIDTH, D_MODEL)), _resident((1, D_MODEL)),
            _resident((N_GATE_BLOCKS, GATE_BLOCK, GATE_BLOCK)), _resident((1, D_MODEL)),
            _resident((N_GATE_BLOCKS, GATE_BLOCK, GATE_BLOCK)), _resident((1, D_MODEL)),
            _resident((1, D_MODEL)), _resident((D_MODEL, D_MODEL)),
        ],
        out_specs=_rows(tb, D_MODEL),
        out_shape=jax.ShapeDtypeStruct((t, D_MODEL), F32),
        scratch_shapes=[pltpu.VMEM((tb, D_MODEL), BF16), pltpu.VMEM((tb, D_MODEL), F32),
                        pltpu.VMEM((tb, D_MODEL), BF16), pltpu.VMEM((tb, D_MODEL), BF16),
                        scan_scratch, scan_scratch,
                        pltpu.VMEM((1, D_MODEL), F32), pltpu.VMEM((CONV_WIDTH - 1, D_MODEL), F32)],
        compiler_params=_params("arbitrary"),
        name="rglru_layer",
    )(h, row(mix_norm), w_in.astype(BF16), conv_w.astype(F32), row(conv_b), w_gate_a.astype(BF16),
      row(b_gate_a), w_gate_x.astype(BF16), row(b_gate_x), row(lru_lambda), w_o.astype(BF16))


S5_TM = 512
S5_IN_PITCH = 20
S5_GLU_PITCH = 36


def _s5_in_epilogue(c, cols, accs, lhs_ref, extras, outs, acc_ref):
    n = accs[0].shape[0] // S5_CHUNK
    for k in range(acc_ref.shape[0]):
        for ch in range(n):
            acc_ref[k, ch * S5_IN_PITCH:ch * S5_IN_PITCH + S5_CHUNK, :] = (
                accs[0][ch * S5_CHUNK:(ch + 1) * S5_CHUNK, k * LANES:(k + 1) * LANES])
        for l in range(S5_CHUNK):
            outs[0][l, :, cols.start + k * LANES:cols.start + (k + 1) * LANES] = (
                acc_ref[k, pl.ds(l, n, stride=S5_IN_PITCH), :])


def _s5_in(h, mix_norm, w_in):
    t = h.shape[0]
    nc = t // S5_CHUNK
    tm = min(S5_TM, t)
    tn = 512
    (u,) = _row_matmul(
        "s5_in", t // tm, h, _rows(tm, D_MODEL), tm, w_in, (0,), tn, D_MODEL // tn,
        extras=[],
        outs=[(jax.ShapeDtypeStruct((S5_CHUNK, nc, D_MODEL), F32),
               pl.BlockSpec((S5_CHUNK, tm // S5_CHUNK, D_MODEL), lambda i: (0, i, 0)))],
        epilogue=_s5_in_epilogue, gain=mix_norm,
        scratch=[pltpu.VMEM((tn // LANES, (tm // S5_CHUNK) * S5_IN_PITCH, LANES), F32)])
    return u


def _s5_ssm_kernel(u_ref, lr_ref, li_ref, ldt_ref, bre_ref, bim_ref, cre_ref, cim_ref, dsk_ref,
                   g_ref, tbig_ref, wend_ref, wc_ref, x_ref, e_ref, s_ref):
    nc = u_ref.shape[1]
    L = S5_CHUNK
    P = S5_STATE_COLS
    lr = lr_ref[...]
    li = li_ref[...]
    dt = jnp.exp(ldt_ref[...])
    mag = jnp.exp(lr * dt)
    abar_re = mag * jnp.cos(li * dt)
    abar_im = mag * jnp.sin(li * dt)
    n_re = abar_re - 1.0
    n_im = abar_im
    den = lr * lr + li * li
    z_re = (n_re * lr + n_im * li) / den
    z_im = (n_im * lr - n_re * li) / den
    bre = bre_ref[...]
    bim = bim_ref[...]
    bbar_re = z_re * bre - z_im * bim
    bbar_im = z_re * bim + z_im * bre
    cre = cre_ref[...]
    cim = cim_ref[...]

    def power(d):
        m = jnp.exp((d * dt) * lr)
        ang = (d * dt) * li
        return m * jnp.cos(ang), m * jnp.sin(ang)

    def w_in_state(d):
        pr, pi = power(d)
        return jnp.concatenate([pr * bbar_re - pi * bbar_im, pr * bbar_im + pi * bbar_re],
                               axis=1).astype(BF16)

    def w_out_state(d):
        pr, pi = power(d)
        return jnp.concatenate([cre * pr - cim * pi, -(cre * pi + cim * pr)], axis=1).astype(BF16)

    wt0 = w_out_state(0)
    for d in range(L):
        wj = w_in_state(d)
        m_d = _dot_nt(wj, wt0).astype(BF16)
        for jj in range(L - d):
            tt = jj + d
            tbig_ref[jj * LANES:(jj + 1) * LANES, tt * LANES:(tt + 1) * LANES] = m_d
        wend_ref[(L - 1 - d) * LANES:(L - d) * LANES, :] = wj
        wc_ref[d * LANES:(d + 1) * LANES, :] = w_out_state(d + 1)
    for tt in range(0, L, 2):
        tbig_ref[(tt + 1) * LANES:(tt + 2) * LANES, tt * LANES:(tt + 1) * LANES] = (
            jnp.zeros((LANES, LANES), BF16))

    x_ref[...] = jnp.concatenate([u_ref[l] for l in range(L)], axis=1).astype(BF16)
    e_ref[...] = _dot(x_ref[...], wend_ref[...])

    row = lax.broadcasted_iota(jnp.int32, (SUBLANES, P), 0)
    steps = []
    for s in (1, 2, 4):
        pr, pi = power(L * s)
        steps.append((s, row >= s, pr, pi))
    carry_pw = [power(L * (r + 1)) for r in range(SUBLANES)]
    cpw_re = jnp.concatenate([c[0] for c in carry_pw], axis=0)
    cpw_im = jnp.concatenate([c[1] for c in carry_pw], axis=0)

    s_ref[0:SUBLANES, :] = jnp.zeros((SUBLANES, 2 * P), F32)

    def tile_step(i, carry):
        s_re, s_im = carry
        base = pl.multiple_of(i * SUBLANES, SUBLANES)
        e = e_ref[pl.ds(base, SUBLANES), :]
        t_re = e[:, :P]
        t_im = e[:, P:]
        for s, keep, pr, pi in steps:
            sh_re = jnp.where(keep, pltpu.roll(t_re, s, 0), 0.0)
            sh_im = jnp.where(keep, pltpu.roll(t_im, s, 0), 0.0)
            t_re, t_im = t_re + pr * sh_re - pi * sh_im, t_im + pr * sh_im + pi * sh_re
        f_re = t_re + cpw_re * s_re - cpw_im * s_im
        f_im = t_im + cpw_re * s_im + cpw_im * s_re
        s_ref[pl.ds(base + SUBLANES, SUBLANES), :] = jnp.concatenate([f_re, f_im], axis=1)
        return f_re[SUBLANES - 1:SUBLANES, :], f_im[SUBLANES - 1:SUBLANES, :]

    zero = jnp.zeros((1, P), F32)
    lax.fori_loop(0, nc // SUBLANES, tile_step, (zero, zero))

    s_start = s_ref[SUBLANES - 1:SUBLANES - 1 + nc, :].astype(BF16)
    dsk = dsk_ref[...]
    for b in range(L // 2):
        k_rows = (2 * b + 2) * LANES
        cols = slice(2 * b * LANES, (2 * b + 2) * LANES)
        y = _dot(x_ref[:, :k_rows], tbig_ref[:k_rows, cols]) + _dot_nt(s_start, wc_ref[cols, :])
        for l in (2 * b, 2 * b + 1):
            yl = y[:, (l - 2 * b) * LANES:(l - 2 * b + 1) * LANES] + dsk * u_ref[l]
            g_ref[l] = _gelu_tanh(yl).astype(BF16)


def _s5_ssm(u, a_re, a_im, log_dt, b_re, b_im, c_re, c_im, d_skip):
    L, nc, _ = u.shape
    P = S5_STATE_COLS
    gpt = S5_GROUPS_PER_TILE
    nt = N_LANE_TILES
    f = lambda v: v.astype(F32)
    row = lambda v: f(v).reshape(nt, 1, P)
    ldt = jnp.broadcast_to(f(log_dt)[:, None], (N_SSM_GROUPS, SSM_STATE))
    same_group = (jnp.arange(LANES)[:, None] // SSM_GROUP) == (jnp.arange(P)[None, :] // SSM_STATE)

    def block_diag(x):
        return jnp.where(same_group, jnp.tile(x, (1, 1, gpt)), 0.0)

    def b_layout(b):
        bt = f(b).reshape(nt, gpt, SSM_STATE, SSM_GROUP).transpose(0, 1, 3, 2)
        return block_diag(bt.reshape(nt, LANES, SSM_STATE))

    def c_layout(c):
        return block_diag(f(c).reshape(nt, LANES, SSM_STATE))

    tile_row = pl.BlockSpec((None, 1, P), lambda k: (k, 0, 0))
    tile_mat = pl.BlockSpec((None, LANES, P), lambda k: (k, 0, 0))
    return pl.pallas_call(
        _s5_ssm_kernel,
        grid=(nt,),
        in_specs=[pl.BlockSpec((L, nc, LANES), lambda k: (0, 0, k)),
                  tile_row, tile_row, tile_row, tile_mat, tile_mat, tile_mat, tile_mat,
                  pl.BlockSpec((1, LANES), lambda k: (0, k))],
        out_specs=pl.BlockSpec((L, nc, LANES), lambda k: (0, 0, k)),
        out_shape=jax.ShapeDtypeStruct((L, nc, D_MODEL), BF16),
        scratch_shapes=[pltpu.VMEM((L * LANES, L * LANES), BF16),
                        pltpu.VMEM((L * LANES, 2 * P), BF16),
                        pltpu.VMEM((L * LANES, 2 * P), BF16),
                        pltpu.VMEM((nc, L * LANES), BF16),
                        pltpu.VMEM((nc, 2 * P), F32),
                        pltpu.VMEM((nc + SUBLANES, 2 * P), F32)],
        compiler_params=_params("arbitrary"),
        name="s5_ssm",
    )(u, row(a_re), row(a_im), row(ldt), b_layout(b_re), b_layout(b_im), c_layout(c_re),
      c_layout(c_im), f(d_skip).reshape(1, D_MODEL))


def _glu_epilogue(c, cols, accs, lhs_ref, extras, outs, y_ref):
    y = accs[0] * _sigmoid(accs[1])
    n = y.shape[0] // S5_CHUNK
    assert n <= S5_GLU_PITCH
    for k in range(y_ref.shape[0]):
        for l in range(S5_CHUNK):
            y_ref[k, l * S5_GLU_PITCH:l * S5_GLU_PITCH + n, :] = (
                y[l * n:(l + 1) * n, k * LANES:(k + 1) * LANES])
        lanes = slice(cols.start + k * LANES, cols.start + (k + 1) * LANES)
        for ch in range(n):
            rows = slice(ch * S5_CHUNK, (ch + 1) * S5_CHUNK)
            outs[0][rows, lanes] = (extras[0][rows, lanes]
                                    + y_ref[k, pl.ds(ch, S5_CHUNK, stride=S5_GLU_PITCH), :])


def _s5_glu(g, w_glu, h):
    L, nc, _ = g.shape
    t = nc * L
    tm = min(S5_TM, t)
    (out,) = _row_matmul(
        "s5_glu", t // tm, g, pl.BlockSpec((L, tm // L, D_MODEL), lambda i: (0, i, 0)), tm, w_glu,
        (0, D_MODEL), 512, D_MODEL // 512,
        extras=[(h, _rows(tm, D_MODEL))],
        outs=[(jax.ShapeDtypeStruct((t, D_MODEL), F32), _rows(tm, D_MODEL))],
        epilogue=_glu_epilogue,
        scratch=[pltpu.VMEM((512 // LANES, S5_CHUNK * S5_GLU_PITCH, LANES), F32)])
    return out


def _s5_layer(h, mix_norm, w_in, a_re, a_im, log_dt, b_re, b_im, c_re, c_im, d_skip, w_glu):
    u = _s5_in(h, mix_norm, w_in.astype(BF16))
    g = _s5_ssm(u, a_re, a_im, log_dt, b_re, b_im, c_re, c_im, d_skip)
    return _s5_glu(g, w_glu.astype(BF16), h)


def kernel(x, p, positions,
           l0_mix_norm, l0_w_qkv, l0_q_norm, l0_k_norm, l0_sinks, l0_w_o,
           l0_mlp_norm, l0_w_up, l0_w_down, l0_ple_norm, l0_w_ple_gate, l0_w_ple,
           l1_mix_norm, l1_w_in, l1_conv_w, l1_conv_b, l1_w_gate_a, l1_b_gate_a, l1_w_gate_x,
           l1_b_gate_x, l1_lru_lambda, l1_w_o,
           l1_mlp_norm, l1_w_up, l1_w_down, l1_ple_norm, l1_w_ple_gate, l1_w_ple,
           l2_mix_norm, l2_w_in, l2_a_re, l2_a_im, l2_log_dt, l2_b_re, l2_b_im, l2_c_re, l2_c_im,
           l2_d_skip, l2_w_glu,
           l2_mlp_norm, l2_w_up, l2_w_down, l2_ple_norm, l2_w_ple_gate, l2_w_ple,
           l3_mix_norm, l3_w_qkv, l3_q_norm, l3_k_norm, l3_sinks, l3_w_o,
           l3_mlp_norm, l3_w_up, l3_w_down, l3_ple_norm, l3_w_ple_gate, l3_w_ple):
    batch, t, _ = x.shape
    assert batch == 1 and t % (8 * BLOCK) == 0
    pos = positions.reshape(t).astype(jnp.int32)
    tails = [
        (l0_mlp_norm, l0_w_up, l0_w_down, l0_ple_norm, l0_w_ple_gate, l0_w_ple),
        (l1_mlp_norm, l1_w_up, l1_w_down, l1_ple_norm, l1_w_ple_gate, l1_w_ple),
        (l2_mlp_norm, l2_w_up, l2_w_down, l2_ple_norm, l2_w_ple_gate, l2_w_ple),
        (l3_mlp_norm, l3_w_up, l3_w_down, l3_ple_norm, l3_w_ple_gate, l3_w_ple),
    ]
    h = x.reshape(t, D_MODEL).astype(F32)
    p_all = p.reshape(DEPTH * t, PLE_DIM)
    for i in range(DEPTH):
        if i == 0:
            h, mlp16 = _attention_layer(h, pos, l0_mix_norm, l0_w_qkv, l0_q_norm, l0_k_norm, l0_sinks,
                                        l0_w_o, mlp_weights=(l0_w_up, l0_w_down))
        elif i == 1:
            h = _rglru_layer(h, l1_mix_norm, l1_w_in, l1_conv_w, l1_conv_b, l1_w_gate_a, l1_b_gate_a,
                             l1_w_gate_x, l1_b_gate_x, l1_lru_lambda, l1_w_o)
        elif i == 2:
            h = _s5_layer(h, l2_mix_norm, l2_w_in, l2_a_re, l2_a_im, l2_log_dt, l2_b_re, l2_b_im,
                          l2_c_re, l2_c_im, l2_d_skip, l2_w_glu)
        else:
            h, _ = _attention_layer(h, pos, l3_mix_norm, l3_w_qkv, l3_q_norm, l3_k_norm, l3_sinks, l3_w_o)
        mlp_norm, _, _, ple_norm, w_ple_gate, w_ple = tails[i]
        next_weights = tails[i + 1][1:3] if i + 1 < DEPTH else None
        h, mlp16 = _mlp(h, mlp_norm, *mlp16, next_weights=next_weights)
        h = _ple(h, ple_norm, p_all, i, w_ple.astype(BF16), w_ple_gate.astype(BF16))
    return h.reshape(batch, t, D_MODEL).astype(x.dtype)
```

```python
import functools
import math

import jax
import jax.numpy as jnp
from jax import lax
from jax.experimental import pallas as pl
from jax.experimental.pallas import tpu as pltpu

F32 = jnp.float32
BF16 = jnp.bfloat16

D_MODEL = 2048
DEPTH = 4
N_HEADS = 32
N_KV_HEADS = 4
HEAD_DIM = 64
Q_PER_KV = N_HEADS // N_KV_HEADS
WINDOW = 128
BLOCK = 128
PAD_POS = -(1 << 20)
N_GATE_BLOCKS = 8
GATE_BLOCK = D_MODEL // N_GATE_BLOCKS
CONV_WIDTH = 4
LRU_C = 8.0
SSM_GROUP = 16
N_SSM_GROUPS = D_MODEL // SSM_GROUP
SSM_STATE = 64
D_FF = 4 * D_MODEL
PLE_DIM = 256
EPS = 1e-6

LANES = 128
SUBLANES = 8
MXU_DIM = 256
VMEM_LIMIT_BYTES = 60 * 1024 * 1024
MASK_DIST = 1e33

S5_CHUNK = 16
S5_GROUPS_PER_TILE = LANES // SSM_GROUP
S5_STATE_COLS = S5_GROUPS_PER_TILE * SSM_STATE
N_LANE_TILES = D_MODEL // LANES


def _params(*semantics):
    return pltpu.CompilerParams(dimension_semantics=semantics,
                                vmem_limit_bytes=VMEM_LIMIT_BYTES)


def _rms(x, gain):
    ms = jnp.mean(x * x, axis=-1, keepdims=True)
    return x * lax.rsqrt(ms + EPS) * gain


def _gelu_tanh(x):
    return 0.5 * x * (1.0 + jnp.tanh(math.sqrt(2.0 / math.pi) * (x + 0.044715 * (x * x * x))))


def _sigmoid(x):
    return 0.5 * jnp.tanh(0.5 * x) + 0.5


def _dot(a, b):
    return jnp.dot(a, b, preferred_element_type=F32)


def _dot_nt(a, b):
    return lax.dot_general(a, b, (((1,), (1,)), ((), ())), preferred_element_type=F32)


def _resident(shape):
    zeros = (0,) * len(shape)
    return pl.BlockSpec(shape, lambda i: zeros, pipeline_mode=pl.Buffered(1))


def _with_side_casts(kernel, n_in, n_out, n_cast):
    def wrapped(*refs):
        ins = refs[:n_in]
        cast_ins = refs[n_in:n_in + n_cast]
        outs = refs[n_in + n_cast:n_in + n_cast + n_out]
        cast_outs = refs[n_in + n_cast + n_out:n_in + 2 * n_cast + n_out]
        scratch = refs[n_in + 2 * n_cast + n_out:]
        for src, dst in zip(cast_ins, cast_outs):
            dst[...] = src[...].astype(BF16)
        kernel(*ins, *outs, *scratch)

    return wrapped


def _call(kernel, name, grid, semantics, in_specs, args, out_specs, out_shape, scratch=(),
          side_casts=(), step_index=None):
    in_specs = list(in_specs)
    out_specs = list(out_specs)
    out_shape = list(out_shape)
    args = list(args)
    n_in, n_out, n_cast = len(in_specs), len(out_specs), len(side_casts)
    if n_cast:
        n_steps = math.prod(grid)
        if step_index is None:
            step_index = lambda i: i
        cast_specs = []
        for w in side_casts:
            rows = w.shape[0] // n_steps
            assert rows % 16 == 0 and rows * n_steps == w.shape[0]
            cast_specs.append(pl.BlockSpec((rows, w.shape[1]), lambda *g: (step_index(*g), 0)))
        in_specs += cast_specs
        out_specs += cast_specs
        out_shape += [jax.ShapeDtypeStruct(w.shape, BF16) for w in side_casts]
        args += [w.astype(F32) for w in side_casts]
        kernel = _with_side_casts(kernel, n_in, n_out, n_cast)
    res = pl.pallas_call(
        kernel,
        grid=grid,
        in_specs=in_specs,
        out_specs=out_specs,
        out_shape=out_shape,
        scratch_shapes=list(scratch),
        compiler_params=_params(*semantics),
        name=name,
    )(*args)
    return list(res[:n_out]), list(res[n_out:])


def _row_matmul_kernel(*refs, n_extra, n_out, has_norm, col_offsets, tn, n_chunks, epilogue):
    lhs_ref = refs[0]
    pos = 1
    if has_norm:
        gain_ref = refs[1]
        pos = 2
    w_ref = refs[pos]
    pos += 1
    extra_refs = refs[pos:pos + n_extra]
    pos += n_extra
    out_refs = refs[pos:pos + n_out]
    pos += n_out
    if has_norm:
        src = refs[pos]
        pos += 1
        src[...] = _rms(lhs_ref[...], gain_ref[...]).astype(BF16)
    else:
        src = lhs_ref
    scratch_refs = refs[pos:]
    for c in range(n_chunks):
        cols = slice(c * tn, (c + 1) * tn)
        a = src[...]
        a = a.reshape(-1, a.shape[-1])
        accs = [_dot(a, w_ref[:, off + c * tn:off + (c + 1) * tn]) for off in col_offsets]
        epilogue(c, cols, accs, lhs_ref, extra_refs, out_refs, *scratch_refs)


def _row_matmul(name, n_blocks, lhs, lhs_spec, tm, w, col_offsets, tn, n_chunks, extras, outs,
                epilogue, gain=None, scratch=(), side_casts=()):
    has_norm = gain is not None
    k_dim = w.shape[0]
    args = [lhs]
    in_specs = [lhs_spec]
    if has_norm:
        args.append(gain.reshape(1, k_dim).astype(F32))
        in_specs.append(_resident((1, k_dim)))
    args.append(w)
    in_specs.append(_resident(w.shape))
    for a, s in extras:
        args.append(a)
        in_specs.append(s)
    kern = functools.partial(_row_matmul_kernel, n_extra=len(extras), n_out=len(outs),
                             has_norm=has_norm, col_offsets=tuple(col_offsets), tn=tn,
                             n_chunks=n_chunks, epilogue=epilogue)
    scratch = ([pltpu.VMEM((tm, k_dim), BF16)] if has_norm else []) + list(scratch)
    return _call(kern, name, (n_blocks,), ("parallel",), in_specs, args, [s for _, s in outs],
                 [o for o, _ in outs], scratch, side_casts)


def _rows(tm, width):
    return pl.BlockSpec((tm, width), lambda i: (i, 0))


QKV_TN = 512
QKV_COLS = D_MODEL + 2 * N_KV_HEADS * LANES
QKV_NORM_BLOCKS = (D_MODEL + N_KV_HEADS * LANES) // QKV_TN
LOG2E = math.log2(math.e)


ATTN_GROUP = 256
ATTN_GROUPS_PER_STEP = 2
K_COL = D_MODEL
V_COL = D_MODEL + N_KV_HEADS * LANES


def _interleave(*task_lists):
    keyed = []
    for n_list, tasks in enumerate(task_lists):
        for k, task in enumerate(tasks):
            keyed.append(((k + 0.5) / len(tasks), n_list, k, task))
    keyed.sort(key=lambda item: item[:3])
    return [item[3] for item in keyed]


def _qkv_tasks(hn_ref, wp_ref, gain_ref, seg_ref, qkv_ref, rows):
    def chunk(c):
        cols = slice(c * QKV_TN, (c + 1) * QKV_TN)
        x = _dot(hn_ref[rows, :], wp_ref[:, cols])
        if c < QKV_NORM_BLOCKS:
            x2 = (x * x).astype(BF16)
            ss = jnp.concatenate(
                [_dot(x2[:, k * MXU_DIM:(k + 1) * MXU_DIM], seg_ref[...])
                 for k in range(QKV_TN // MXU_DIM)], axis=1)
            r = lax.rsqrt(ss * (1.0 / HEAD_DIM) + EPS)
            qkv_ref[rows, cols] = (x * r * gain_ref[:, cols]).astype(BF16)
        else:
            qkv_ref[rows, cols] = x.astype(BF16)

    return [functools.partial(chunk, c) for c in range(QKV_COLS // QKV_TN)]


def _alibi_slopes():
    return [float(2.0 ** (-8.0 * (i + 1) / N_HEADS)) for i in range(N_HEADS)]


def _attend_tasks(qkv_ref, rows, k_prev, v_prev, positions, sinks_ref, o_ref):
    shared = {}

    def prepare():
        qpos, kpos_prev, kpos_cur = positions()
        kpos = jnp.concatenate([kpos_prev, kpos_cur], axis=1)
        dist = qpos - kpos
        valid = (dist >= 0) & (dist < WINDOW)
        lane = lax.broadcasted_iota(jnp.int32, (BLOCK, LANES), 1)
        lo_f = (lane < HEAD_DIM).astype(F32)
        lo = lo_f.astype(BF16)
        hi = (1.0 - lo_f).astype(BF16)
        shared.update(distm=jnp.where(valid, dist.astype(F32), MASK_DIST), lane=lane, lo=lo, hi=hi,
                      lo2=jnp.concatenate([lo, lo], axis=0), hi2=jnp.concatenate([hi, hi], axis=0))

    slopes = _alibi_slopes()

    def head_task(h):
        if not shared:
            prepare()
        distm, lane, lo, hi = shared["distm"], shared["lane"], shared["lo"], shared["hi"]
        lo2, hi2 = shared["lo2"], shared["hi2"]
        kk = jnp.concatenate([k_prev(h), qkv_ref[rows, K_COL + h * LANES:K_COL + (h + 1) * LANES]],
                             axis=0)
        vd = jnp.concatenate([v_prev(h), qkv_ref[rows, V_COL + h * LANES:V_COL + (h + 1) * LANES]],
                             axis=0)
        qs = []
        for g in range(Q_PER_KV):
            col = h * Q_PER_KV * HEAD_DIM + (g // 2) * LANES
            qp = qkv_ref[rows, col:col + LANES]
            qs.append(qp * (lo if g % 2 == 0 else hi))
        s_all = _dot_nt(jnp.concatenate(qs, axis=0), kk)
        ps = []
        rls = []
        for g in range(Q_PER_KV):
            head = h * Q_PER_KV + g
            sg = s_all[g * BLOCK:(g + 1) * BLOCK] - (slopes[head] * LOG2E) * distm
            sink = sinks_ref[head] * LOG2E
            m = jnp.maximum(jnp.max(sg, axis=-1, keepdims=True), sink)
            p = jnp.exp2(sg - m)
            denom = jnp.sum(p, axis=-1, keepdims=True) + jnp.exp2(sink - m)
            ps.append(p.astype(BF16))
            rls.append(1.0 / denom)
        lhs2 = jnp.concatenate(
            [jnp.concatenate([ps[2 * p], ps[2 * p + 1]], axis=1) for p in range(Q_PER_KV // 2)],
            axis=0)
        rhs2 = jnp.concatenate([vd * lo2, vd * hi2], axis=0)
        o2 = _dot(lhs2, rhs2)
        for p in range(Q_PER_KV // 2):
            scale = jnp.where(lane < HEAD_DIM, rls[2 * p], rls[2 * p + 1])
            col = h * Q_PER_KV * HEAD_DIM + p * LANES
            o_ref[rows, col:col + LANES] = (o2[p * BLOCK:(p + 1) * BLOCK] * scale).astype(BF16)

    return [functools.partial(head_task, h) for h in range(N_KV_HEADS)]


def _attn_layer_kernel(h_ref, gain_ref, wp_ref, hg_ref, seg_ref, wo_ref, qpos_ref, kposp_ref,
                       kpos_ref, sinks_ref, out_ref, hn_ref, qkv_ref, o_ref, tail_ref):
    first = pl.program_id(0) == 0

    @pl.when(first)
    def _():
        tail_ref[...] = jnp.zeros(tail_ref.shape, BF16)

    hn_ref[...] = _rms(h_ref[...], gain_ref[...]).astype(BF16)
    kv_width = N_KV_HEADS * LANES
    step_rows = h_ref.shape[0]
    n_groups = step_rows // ATTN_GROUP
    bpg = ATTN_GROUP // BLOCK
    project, attend, emit = [], [], []
    for g in range(n_groups):
        grows = slice(g * ATTN_GROUP, (g + 1) * ATTN_GROUP)
        project.append(_qkv_tasks(hn_ref, wp_ref, hg_ref, seg_ref, qkv_ref, grows))

        def out_chunk(c, grows=grows):
            cols = slice(c * QKV_TN, (c + 1) * QKV_TN)
            out_ref[grows, cols] = h_ref[grows, cols] + _dot(o_ref[grows, :], wo_ref[:, cols])

        emit.append([functools.partial(out_chunk, c) for c in range(D_MODEL // QKV_TN)])
    for b in range(step_rows // BLOCK):
        rows = slice(b * BLOCK, (b + 1) * BLOCK)
        if b == 0:
            k_prev = lambda h: tail_ref[:, h * LANES:(h + 1) * LANES]
            v_prev = lambda h: tail_ref[:, kv_width + h * LANES:kv_width + (h + 1) * LANES]
            positions = lambda rows=rows: (qpos_ref[rows, :],
                                           jnp.where(first, PAD_POS, kposp_ref[...]), kpos_ref[0])
        else:
            prows = slice((b - 1) * BLOCK, b * BLOCK)
            k_prev = lambda h, prows=prows: qkv_ref[prows, K_COL + h * LANES:K_COL + (h + 1) * LANES]
            v_prev = lambda h, prows=prows: qkv_ref[prows, V_COL + h * LANES:V_COL + (h + 1) * LANES]
            positions = lambda rows=rows, b=b: (qpos_ref[rows, :], kpos_ref[b - 1], kpos_ref[b])
        attend.append(_attend_tasks(qkv_ref, rows, k_prev, v_prev, positions, sinks_ref, o_ref))

    for task in project[0]:
        task()
    for g in range(n_groups):
        softmax = [t for b in range(g * bpg, (g + 1) * bpg) for t in attend[b]]
        nxt = project[g + 1] if g + 1 < n_groups else []
        prv = emit[g - 1] if g > 0 else []
        for task in _interleave(softmax, nxt, prv):
            task()
    last = slice(step_rows - BLOCK, step_rows)
    tail_ref[...] = qkv_ref[last, K_COL:K_COL + 2 * kv_width]
    for task in emit[n_groups - 1]:
        task()


def _residual_epilogue(c, cols, accs, lhs_ref, extras, outs):
    outs[0][:, cols] = extras[0][:, cols] + accs[0]


def _proj_residual(a, w, h):
    t, k_dim = a.shape
    tm = min(1024, t)
    (out,) = _row_matmul(
        "proj_residual", t // tm, a, _rows(tm, k_dim), tm, w, (0,), 512, D_MODEL // 512,
        extras=[(h, _rows(tm, D_MODEL))],
        outs=[(jax.ShapeDtypeStruct((t, D_MODEL), F32), _rows(tm, D_MODEL))],
        epilogue=_residual_epilogue)
    return out


def _prep_qkv_weight(w_qkv):
    w16 = w_qkv.astype(BF16)
    wq = w16[:, :D_MODEL]
    wk = w16[:, D_MODEL:D_MODEL + N_KV_HEADS * HEAD_DIM].reshape(D_MODEL, N_KV_HEADS, HEAD_DIM)
    wv = w16[:, D_MODEL + N_KV_HEADS * HEAD_DIM:].reshape(D_MODEL, N_KV_HEADS, HEAD_DIM)
    dup = lambda w: jnp.concatenate([w, w], axis=-1).reshape(D_MODEL, N_KV_HEADS * LANES)
    return jnp.concatenate([wq, dup(wk), dup(wv)], axis=1)


def _attention_layer(h, positions, mix_norm, w_qkv, q_norm, k_norm, sinks, w_o, side_casts=()):
    head_gain = jnp.concatenate([
        jnp.tile(q_norm.astype(F32), N_HEADS) * (HEAD_DIM ** -0.5 * LOG2E),
        jnp.tile(k_norm.astype(F32), 2 * N_KV_HEADS),
        jnp.ones((N_KV_HEADS * LANES,), F32)]).reshape(1, QKV_COLS)
    seg_ones = jnp.kron(jnp.eye(MXU_DIM // HEAD_DIM, dtype=F32),
                        jnp.ones((HEAD_DIM, HEAD_DIM), F32)).astype(BF16)
    t = h.shape[0]
    rows = ATTN_GROUP * (1 if side_casts else ATTN_GROUPS_PER_STEP)
    n_steps = t // rows
    bps = rows // BLOCK
    pos_row = positions.reshape(t // BLOCK, 1, BLOCK)
    in_specs = [
        _rows(rows, D_MODEL),
        _resident((1, D_MODEL)),
        _resident((D_MODEL, QKV_COLS)),
        _resident((1, QKV_COLS)),
        _resident((MXU_DIM, MXU_DIM)),
        _resident((D_MODEL, D_MODEL)),
        pl.BlockSpec((rows, 1), lambda i: (i, 0)),
        pl.BlockSpec((None, 1, BLOCK), lambda i: (jnp.maximum(bps * i - 1, 0), 0, 0)),
        pl.BlockSpec((bps, 1, BLOCK), lambda i: (i, 0, 0)),
        pl.BlockSpec(memory_space=pltpu.SMEM),
    ]
    args = [h, mix_norm.reshape(1, D_MODEL).astype(F32), _prep_qkv_weight(w_qkv), head_gain, seg_ones,
            w_o.astype(BF16), positions.reshape(t, 1), pos_row, pos_row, sinks.astype(F32)]
    scratch = [pltpu.VMEM((rows, D_MODEL), BF16), pltpu.VMEM((rows, QKV_COLS), BF16),
               pltpu.VMEM((rows, D_MODEL), BF16), pltpu.VMEM((BLOCK, 2 * N_KV_HEADS * LANES), BF16)]
    (out,), casts = _call(_attn_layer_kernel, "attn_layer", (n_steps,), ("arbitrary",), in_specs, args,
                          [_rows(rows, D_MODEL)], [jax.ShapeDtypeStruct((t, D_MODEL), F32)], scratch,
                          side_casts)
    return out, casts


MLP_TF = 512


def _mlp_kernel(h_ref, gain_ref, wu_ref, wd_ref, o_ref, hn_ref):
    j = pl.program_id(1)

    @pl.when(j == 0)
    def _():
        x = h_ref[...]
        hn_ref[...] = _rms(x, gain_ref[...]).astype(BF16)
        o_ref[...] = x

    act = jnp.maximum(_dot(hn_ref[...], wu_ref[...]), 0.0)
    act = (act * act).astype(BF16)
    o_ref[...] += _dot(act, wd_ref[...])


def _mlp(h, gain, w_up16, w_down16, side_casts=()):
    t = h.shape[0]
    tm = min(1024, t)
    tf = MLP_TF
    nj = D_FF // tf
    in_specs = [
        pl.BlockSpec((tm, D_MODEL), lambda i, j: (i, 0)),
        pl.BlockSpec((1, D_MODEL), lambda i, j: (0, 0)),
        pl.BlockSpec((D_MODEL, tf), lambda i, j: (0, j)),
        pl.BlockSpec((tf, D_MODEL), lambda i, j: (j, 0)),
    ]
    args = [h, gain.reshape(1, D_MODEL).astype(F32), w_up16, w_down16]
    (out,), casts = _call(_mlp_kernel, "mlp", (t // tm, nj), ("parallel", "arbitrary"), in_specs, args,
                          [pl.BlockSpec((tm, D_MODEL), lambda i, j: (i, 0))],
                          [jax.ShapeDtypeStruct((t, D_MODEL), F32)],
                          [pltpu.VMEM((tm, D_MODEL), BF16)], side_casts,
                          step_index=lambda i, j: i * nj + j)
    return out, casts


def _ple_epilogue(c, cols, accs, lhs_ref, extras, outs):
    p_ref, wple_ref = extras
    emb = _dot(p_ref[...].astype(BF16), wple_ref[:, cols])
    outs[0][:, cols] = lhs_ref[:, cols] + emb * _sigmoid(accs[0])


def _ple(h, gain, p_all, layer, w_ple, w_gate):
    t = h.shape[0]
    tm = min(1024, t)
    first = layer * (t // tm)
    (out,), _ = _row_matmul(
        "ple", t // tm, h, _rows(tm, D_MODEL), tm, w_gate, (0,), 512, D_MODEL // 512,
        extras=[(p_all, pl.BlockSpec((tm, PLE_DIM), lambda i: (first + i, 0))),
                (w_ple, _resident((PLE_DIM, D_MODEL)))],
        outs=[(jax.ShapeDtypeStruct((t, D_MODEL), F32), _rows(tm, D_MODEL))],
        epilogue=_ple_epilogue, gain=gain)
    return out


def _rglru_in_epilogue(c, cols, accs, lhs_ref, extras, outs):
    outs[0][:, cols] = accs[0]
    outs[1][:, cols] = _gelu_tanh(accs[1]).astype(BF16)


def _rglru_in(h, mix_norm, w_in):
    t = h.shape[0]
    tm = min(512, t)
    return _row_matmul(
        "rglru_in", t // tm, h, _rows(tm, D_MODEL), tm, w_in, (0, D_MODEL), 512, D_MODEL // 512,
        extras=[],
        outs=[(jax.ShapeDtypeStruct((t, D_MODEL), F32), _rows(tm, D_MODEL)),
              (jax.ShapeDtypeStruct((t, D_MODEL), BF16), _rows(tm, D_MODEL))],
        epilogue=_rglru_in_epilogue, gain=mix_norm)


RGLRU_TB = 256
RGLRU_SEG = RGLRU_TB // SUBLANES
RGLRU_PITCH = 36
TINY = 1e-36


def _rglru_scan_kernel(xb_ref, gate_ref, cw_ref, cb_ref, wa_ref, ba_ref, wx_ref, bx_ref, lam_ref,
                       y_ref, a_ref, b_ref, carry_ref, tail_ref):
    tb = RGLRU_TB
    seg = RGLRU_SEG
    n_slabs = D_MODEL // LANES

    @pl.when(pl.program_id(0) == 0)
    def _():
        carry_ref[...] = jnp.zeros((1, D_MODEL), F32)
        tail_ref[...] = jnp.zeros((CONV_WIDTH - 1, D_MODEL), F32)

    def shift1(v, k):
        prev_last = tail_ref[k:k + 1, :]
        tail_ref[k:k + 1, :] = v[tb - 1:tb, :]
        return jnp.concatenate([prev_last, v[:tb - 1, :]], axis=0)

    x = xb_ref[...]
    acc = cw_ref[0:1, :] * x
    for k in range(1, CONV_WIDTH):
        acc = cw_ref[k:k + 1, :] * x + shift1(acc, k - 1)
    xc = acc + cb_ref[...]

    lam = lam_ref[...]
    rate = (-LRU_C) * (jnp.log(1.0 + jnp.exp(-jnp.abs(lam))) + jnp.maximum(-lam, 0.0))
    for n in range(N_GATE_BLOCKS):
        sl = slice(n * GATE_BLOCK, (n + 1) * GATE_BLOCK)
        xn = xc[:, sl]
        xn16 = xn.astype(BF16)
        r = _sigmoid(_dot(xn16, wa_ref[n]) + ba_ref[:, sl])
        i_gate = _sigmoid(_dot(xn16, wx_ref[n]) + bx_ref[:, sl])
        a = jnp.exp(rate[:, sl] * r)
        v = 1.0 - a * a
        b = (v * lax.rsqrt(jnp.maximum(v, TINY))) * (i_gate * xn)
        for kk in range(GATE_BLOCK // LANES):
            slab = n * (GATE_BLOCK // LANES) + kk
            for s in range(SUBLANES):
                rows = slice(s * RGLRU_PITCH, s * RGLRU_PITCH + seg)
                a_ref[slab, rows, :] = a[s * seg:(s + 1) * seg, kk * LANES:(kk + 1) * LANES]
                b_ref[slab, rows, :] = b[s * seg:(s + 1) * seg, kk * LANES:(kk + 1) * LANES]

    def step(q, carry):
        hs, prods = carry
        rows = pl.ds(q, SUBLANES, stride=RGLRU_PITCH)
        new_hs = []
        new_prods = []
        for k in range(n_slabs):
            av = a_ref[k, rows, :]
            hk = av * hs[k] + b_ref[k, rows, :]
            pk = av * prods[k]
            b_ref[k, rows, :] = hk
            a_ref[k, rows, :] = pk
            new_hs.append(hk)
            new_prods.append(pk)
        return tuple(new_hs), tuple(new_prods)

    zeros = tuple(jnp.zeros((SUBLANES, LANES), F32) for _ in range(n_slabs))
    ones = tuple(jnp.ones((SUBLANES, LANES), F32) for _ in range(n_slabs))
    hs, prods = lax.fori_loop(0, seg, step, (zeros, ones))

    for k in range(n_slabs):
        lanes = slice(k * LANES, (k + 1) * LANES)
        cur = carry_ref[:, lanes]
        for s in range(SUBLANES):
            rows = slice(s * RGLRU_PITCH, s * RGLRU_PITCH + seg)
            hfull = b_ref[k, rows, :] + a_ref[k, rows, :] * cur
            out_rows = slice(s * seg, (s + 1) * seg)
            y_ref[out_rows, lanes] = (hfull * gate_ref[out_rows, lanes].astype(F32)).astype(BF16)
            cur = prods[k][s:s + 1, :] * cur + hs[k][s:s + 1, :]
        carry_ref[:, lanes] = cur


def _rglru_scan(xb, gate, conv_w, conv_b, w_gate_a, b_gate_a, w_gate_x, b_gate_x, lru_lambda):
    t = xb.shape[0]
    tb = RGLRU_TB
    row = lambda v: v.reshape(1, D_MODEL).astype(F32)
    return pl.pallas_call(
        _rglru_scan_kernel,
        grid=(t // tb,),
        in_specs=[
            _rows(tb, D_MODEL), _rows(tb, D_MODEL),
            _resident((CONV_WIDTH, D_MODEL)), _resident((1, D_MODEL)),
            _resident((N_GATE_BLOCKS, GATE_BLOCK, GATE_BLOCK)), _resident((1, D_MODEL)),
            _resident((N_GATE_BLOCKS, GATE_BLOCK, GATE_BLOCK)), _resident((1, D_MODEL)),
            _resident((1, D_MODEL)),
        ],
        out_specs=_rows(tb, D_MODEL),
        out_shape=jax.ShapeDtypeStruct((t, D_MODEL), BF16),
        scratch_shapes=[pltpu.VMEM((D_MODEL // LANES, SUBLANES * RGLRU_PITCH, LANES), F32),
                        pltpu.VMEM((D_MODEL // LANES, SUBLANES * RGLRU_PITCH, LANES), F32),
                        pltpu.VMEM((1, D_MODEL), F32),
                        pltpu.VMEM((CONV_WIDTH - 1, D_MODEL), F32)],
        compiler_params=_params("arbitrary"),
        name="rglru_scan",
    )(xb, gate, conv_w.astype(F32), row(conv_b), w_gate_a.astype(BF16), row(b_gate_a),
      w_gate_x.astype(BF16), row(b_gate_x), row(lru_lambda))


def _rglru_layer_kernel(h_ref, gain_ref, win_ref, cw_ref, cb_ref, wa_ref, ba_ref, wx_ref, bx_ref,
                        lam_ref, wo_ref, out_ref, hn_ref, xb_ref, gate_ref, y_ref, a_ref, b_ref,
                        carry_ref, tail_ref):
    tb = RGLRU_TB
    seg = RGLRU_SEG
    n_slabs = D_MODEL // LANES
    spb = GATE_BLOCK // LANES

    @pl.when(pl.program_id(0) == 0)
    def _():
        carry_ref[...] = jnp.zeros((1, D_MODEL), F32)
        tail_ref[...] = jnp.zeros((CONV_WIDTH - 1, D_MODEL), F32)

    hn_ref[...] = _rms(h_ref[...], gain_ref[...]).astype(BF16)
    lam = lam_ref[...]
    rate = (-LRU_C) * (jnp.log(1.0 + jnp.exp(-jnp.abs(lam))) + jnp.maximum(-lam, 0.0))

    def project(n):
        sl = slice(n * GATE_BLOCK, (n + 1) * GATE_BLOCK)
        xb_ref[:, sl] = _dot(hn_ref[...], win_ref[:, sl])
        gate_ref[:, sl] = _gelu_tanh(
            _dot(hn_ref[...], win_ref[:, D_MODEL + n * GATE_BLOCK:D_MODEL + (n + 1) * GATE_BLOCK])
        ).astype(BF16)

    def gates(n):
        sl = slice(n * GATE_BLOCK, (n + 1) * GATE_BLOCK)

        def shift1(v, k):
            prev_last = tail_ref[k:k + 1, sl]
            tail_ref[k:k + 1, sl] = v[tb - 1:tb, :]
            return jnp.concatenate([prev_last, v[:tb - 1, :]], axis=0)

        x = xb_ref[:, sl]
        acc = cw_ref[0:1, sl] * x
        for k in range(1, CONV_WIDTH):
            acc = cw_ref[k:k + 1, sl] * x + shift1(acc, k - 1)
        xn = acc + cb_ref[:, sl]
        xn16 = xn.astype(BF16)
        r = _sigmoid(_dot(xn16, wa_ref[n]) + ba_ref[:, sl])
        i_gate = _sigmoid(_dot(xn16, wx_ref[n]) + bx_ref[:, sl])
        a = jnp.exp(rate[:, sl] * r)
        v = 1.0 - a * a
        b = (v * lax.rsqrt(jnp.maximum(v, TINY))) * (i_gate * xn)
        for kk in range(spb):
            slab = n * spb + kk
            for s in range(SUBLANES):
                rows = slice(s * RGLRU_PITCH, s * RGLRU_PITCH + seg)
                a_ref[slab, rows, :] = a[s * seg:(s + 1) * seg, kk * LANES:(kk + 1) * LANES]
                b_ref[slab, rows, :] = b[s * seg:(s + 1) * seg, kk * LANES:(kk + 1) * LANES]

    project(0)
    for n in range(N_GATE_BLOCKS):
        if n + 1 < N_GATE_BLOCKS:
            project(n + 1)
        gates(n)

    def step(q, carry):
        hs, prods = carry
        rows = pl.ds(q, SUBLANES, stride=RGLRU_PITCH)
        new_hs = []
        new_prods = []
        for k in range(n_slabs):
            av = a_ref[k, rows, :]
            hk = av * hs[k] + b_ref[k, rows, :]
            pk = av * prods[k]
            b_ref[k, rows, :] = hk
            a_ref[k, rows, :] = pk
            new_hs.append(hk)
            new_prods.append(pk)
        return tuple(new_hs), tuple(new_prods)

    zeros = tuple(jnp.zeros((SUBLANES, LANES), F32) for _ in range(n_slabs))
    ones = tuple(jnp.ones((SUBLANES, LANES), F32) for _ in range(n_slabs))
    hs, prods = lax.fori_loop(0, seg, step, (zeros, ones))

    for k in range(n_slabs):
        lanes = slice(k * LANES, (k + 1) * LANES)
        cur = carry_ref[:, lanes]
        for s in range(SUBLANES):
            rows = slice(s * RGLRU_PITCH, s * RGLRU_PITCH + seg)
            hfull = b_ref[k, rows, :] + a_ref[k, rows, :] * cur
            out_rows = slice(s * seg, (s + 1) * seg)
            y_ref[out_rows, lanes] = (hfull * gate_ref[out_rows, lanes].astype(F32)).astype(BF16)
            cur = prods[k][s:s + 1, :] * cur + hs[k][s:s + 1, :]
        carry_ref[:, lanes] = cur

    for c in range(D_MODEL // 512):
        cols = slice(c * 512, (c + 1) * 512)
        out_ref[:, cols] = h_ref[:, cols] + _dot(y_ref[...], wo_ref[:, cols])


def _rglru_layer(h, mix_norm, w_in, conv_w, conv_b, w_gate_a, b_gate_a, w_gate_x, b_gate_x,
                 lru_lambda, w_o, side_casts=()):
    t = h.shape[0]
    tb = RGLRU_TB
    row = lambda v: v.reshape(1, D_MODEL).astype(F32)
    scan_scratch = pltpu.VMEM((D_MODEL // LANES, SUBLANES * RGLRU_PITCH, LANES), F32)
    in_specs = [
        _rows(tb, D_MODEL), _resident((1, D_MODEL)), _resident((D_MODEL, 2 * D_MODEL)),
        _resident((CONV_WIDTH, D_MODEL)), _resident((1, D_MODEL)),
        _resident((N_GATE_BLOCKS, GATE_BLOCK, GATE_BLOCK)), _resident((1, D_MODEL)),
        _resident((N_GATE_BLOCKS, GATE_BLOCK, GATE_BLOCK)), _resident((1, D_MODEL)),
        _resident((1, D_MODEL)), _resident((D_MODEL, D_MODEL)),
    ]
    args = [h, row(mix_norm), w_in.astype(BF16), conv_w.astype(F32), row(conv_b),
            w_gate_a.astype(BF16), row(b_gate_a), w_gate_x.astype(BF16), row(b_gate_x),
            row(lru_lambda), w_o.astype(BF16)]
    scratch = [pltpu.VMEM((tb, D_MODEL), BF16), pltpu.VMEM((tb, D_MODEL), F32),
               pltpu.VMEM((tb, D_MODEL), BF16), pltpu.VMEM((tb, D_MODEL), BF16),
               scan_scratch, scan_scratch,
               pltpu.VMEM((1, D_MODEL), F32), pltpu.VMEM((CONV_WIDTH - 1, D_MODEL), F32)]
    (out,), casts = _call(_rglru_layer_kernel, "rglru_layer", (t // tb,), ("arbitrary",), in_specs,
                          args, [_rows(tb, D_MODEL)], [jax.ShapeDtypeStruct((t, D_MODEL), F32)],
                          scratch, side_casts)
    return out, casts


S5_TM = 512
S5_IN_PITCH = 20
S5_GLU_PITCH = 36


def _s5_in_epilogue(c, cols, accs, lhs_ref, extras, outs, acc_ref):
    n = accs[0].shape[0] // S5_CHUNK
    for k in range(acc_ref.shape[0]):
        for ch in range(n):
            acc_ref[k, ch * S5_IN_PITCH:ch * S5_IN_PITCH + S5_CHUNK, :] = (
                accs[0][ch * S5_CHUNK:(ch + 1) * S5_CHUNK, k * LANES:(k + 1) * LANES])
        for l in range(S5_CHUNK):
            outs[0][l, :, cols.start + k * LANES:cols.start + (k + 1) * LANES] = (
                acc_ref[k, pl.ds(l, n, stride=S5_IN_PITCH), :])


def _s5_in(h, mix_norm, w_in):
    t = h.shape[0]
    nc = t // S5_CHUNK
    tm = min(S5_TM, t)
    tn = 512
    (u,), _ = _row_matmul(
        "s5_in", t // tm, h, _rows(tm, D_MODEL), tm, w_in, (0,), tn, D_MODEL // tn,
        extras=[],
        outs=[(jax.ShapeDtypeStruct((S5_CHUNK, nc, D_MODEL), F32),
               pl.BlockSpec((S5_CHUNK, tm // S5_CHUNK, D_MODEL), lambda i: (0, i, 0)))],
        epilogue=_s5_in_epilogue, gain=mix_norm,
        scratch=[pltpu.VMEM((tn // LANES, (tm // S5_CHUNK) * S5_IN_PITCH, LANES), F32)])
    return u


def _s5_ssm_kernel(u_ref, lr_ref, li_ref, ldt_ref, bre_ref, bim_ref, cre_ref, cim_ref, dsk_ref,
                   g_ref, tbig_ref, wend_ref, wc_ref, x_ref, e_ref, s_ref):
    nc = u_ref.shape[1]
    L = S5_CHUNK
    P = S5_STATE_COLS
    lr = lr_ref[...]
    li = li_ref[...]
    dt = jnp.exp(ldt_ref[...])
    mag = jnp.exp(lr * dt)
    abar_re = mag * jnp.cos(li * dt)
    abar_im = mag * jnp.sin(li * dt)
    n_re = abar_re - 1.0
    n_im = abar_im
    den = lr * lr + li * li
    z_re = (n_re * lr + n_im * li) / den
    z_im = (n_im * lr - n_re * li) / den
    bre = bre_ref[...]
    bim = bim_ref[...]
    bbar_re = z_re * bre - z_im * bim
    bbar_im = z_re * bim + z_im * bre
    cre = cre_ref[...]
    cim = cim_ref[...]

    def power(d):
        m = jnp.exp((d * dt) * lr)
        ang = (d * dt) * li
        return m * jnp.cos(ang), m * jnp.sin(ang)

    def w_in_state(d):
        pr, pi = power(d)
        return jnp.concatenate([pr * bbar_re - pi * bbar_im, pr * bbar_im + pi * bbar_re],
                               axis=1).astype(BF16)

    def w_out_state(d):
        pr, pi = power(d)
        return jnp.concatenate([cre * pr - cim * pi, -(cre * pi + cim * pr)], axis=1).astype(BF16)

    wt0 = w_out_state(0)
    for d in range(L):
        wj = w_in_state(d)
        m_d = _dot_nt(wj, wt0).astype(BF16)
        for jj in range(L - d):
            tt = jj + d
            tbig_ref[jj * LANES:(jj + 1) * LANES, tt * LANES:(tt + 1) * LANES] = m_d
        wend_ref[(L - 1 - d) * LANES:(L - d) * LANES, :] = wj
        wc_ref[d * LANES:(d + 1) * LANES, :] = w_out_state(d + 1)
    for tt in range(0, L, 2):
        tbig_ref[(tt + 1) * LANES:(tt + 2) * LANES, tt * LANES:(tt + 1) * LANES] = (
            jnp.zeros((LANES, LANES), BF16))

    x_ref[...] = jnp.concatenate([u_ref[l] for l in range(L)], axis=1).astype(BF16)
    e_ref[...] = _dot(x_ref[...], wend_ref[...])

    row = lax.broadcasted_iota(jnp.int32, (SUBLANES, P), 0)
    steps = []
    for s in (1, 2, 4):
        pr, pi = power(L * s)
        steps.append((s, row >= s, pr, pi))
    carry_pw = [power(L * (r + 1)) for r in range(SUBLANES)]
    cpw_re = jnp.concatenate([c[0] for c in carry_pw], axis=0)
    cpw_im = jnp.concatenate([c[1] for c in carry_pw], axis=0)

    s_ref[0:SUBLANES, :] = jnp.zeros((SUBLANES, 2 * P), F32)

    def tile_step(i, carry):
        s_re, s_im = carry
        base = pl.multiple_of(i * SUBLANES, SUBLANES)
        e = e_ref[pl.ds(base, SUBLANES), :]
        t_re = e[:, :P]
        t_im = e[:, P:]
        for s, keep, pr, pi in steps:
            sh_re = jnp.where(keep, pltpu.roll(t_re, s, 0), 0.0)
            sh_im = jnp.where(keep, pltpu.roll(t_im, s, 0), 0.0)
            t_re, t_im = t_re + pr * sh_re - pi * sh_im, t_im + pr * sh_im + pi * sh_re
        f_re = t_re + cpw_re * s_re - cpw_im * s_im
        f_im = t_im + cpw_re * s_im + cpw_im * s_re
        s_ref[pl.ds(base + SUBLANES, SUBLANES), :] = jnp.concatenate([f_re, f_im], axis=1)
        return f_re[SUBLANES - 1:SUBLANES, :], f_im[SUBLANES - 1:SUBLANES, :]

    zero = jnp.zeros((1, P), F32)
    lax.fori_loop(0, nc // SUBLANES, tile_step, (zero, zero))

    s_start = s_ref[SUBLANES - 1:SUBLANES - 1 + nc, :].astype(BF16)
    dsk = dsk_ref[...]
    for b in range(L // 2):
        k_rows = (2 * b + 2) * LANES
        cols = slice(2 * b * LANES, (2 * b + 2) * LANES)
        y = _dot(x_ref[:, :k_rows], tbig_ref[:k_rows, cols]) + _dot_nt(s_start, wc_ref[cols, :])
        for l in (2 * b, 2 * b + 1):
            yl = y[:, (l - 2 * b) * LANES:(l - 2 * b + 1) * LANES] + dsk * u_ref[l]
            g_ref[l] = _gelu_tanh(yl).astype(BF16)


def _s5_ssm(u, a_re, a_im, log_dt, b_re, b_im, c_re, c_im, d_skip):
    L, nc, _ = u.shape
    P = S5_STATE_COLS
    gpt = S5_GROUPS_PER_TILE
    nt = N_LANE_TILES
    f = lambda v: v.astype(F32)
    row = lambda v: f(v).reshape(nt, 1, P)
    ldt = jnp.broadcast_to(f(log_dt)[:, None], (N_SSM_GROUPS, SSM_STATE))
    same_group = (jnp.arange(LANES)[:, None] // SSM_GROUP) == (jnp.arange(P)[None, :] // SSM_STATE)

    def block_diag(x):
        return jnp.where(same_group, jnp.tile(x, (1, 1, gpt)), 0.0)

    def b_layout(b):
        bt = f(b).reshape(nt, gpt, SSM_STATE, SSM_GROUP).transpose(0, 1, 3, 2)
        return block_diag(bt.reshape(nt, LANES, SSM_STATE))

    def c_layout(c):
        return block_diag(f(c).reshape(nt, LANES, SSM_STATE))

    tile_row = pl.BlockSpec((None, 1, P), lambda k: (k, 0, 0))
    tile_mat = pl.BlockSpec((None, LANES, P), lambda k: (k, 0, 0))
    return pl.pallas_call(
        _s5_ssm_kernel,
        grid=(nt,),
        in_specs=[pl.BlockSpec((L, nc, LANES), lambda k: (0, 0, k)),
                  tile_row, tile_row, tile_row, tile_mat, tile_mat, tile_mat, tile_mat,
                  pl.BlockSpec((1, LANES), lambda k: (0, k))],
        out_specs=pl.BlockSpec((L, nc, LANES), lambda k: (0, 0, k)),
        out_shape=jax.ShapeDtypeStruct((L, nc, D_MODEL), BF16),
        scratch_shapes=[pltpu.VMEM((L * LANES, L * LANES), BF16),
                        pltpu.VMEM((L * LANES, 2 * P), BF16),
                        pltpu.VMEM((L * LANES, 2 * P), BF16),
                        pltpu.VMEM((nc, L * LANES), BF16),
                        pltpu.VMEM((nc, 2 * P), F32),
                        pltpu.VMEM((nc + SUBLANES, 2 * P), F32)],
        compiler_params=_params("arbitrary"),
        name="s5_ssm",
    )(u, row(a_re), row(a_im), row(ldt), b_layout(b_re), b_layout(b_im), c_layout(c_re),
      c_layout(c_im), f(d_skip).reshape(1, D_MODEL))


def _glu_epilogue(c, cols, accs, lhs_ref, extras, outs, y_ref):
    y = accs[0] * _sigmoid(accs[1])
    n = y.shape[0] // S5_CHUNK
    assert n <= S5_GLU_PITCH
    for k in range(y_ref.shape[0]):
        for l in range(S5_CHUNK):
            y_ref[k, l * S5_GLU_PITCH:l * S5_GLU_PITCH + n, :] = (
                y[l * n:(l + 1) * n, k * LANES:(k + 1) * LANES])
        lanes = slice(cols.start + k * LANES, cols.start + (k + 1) * LANES)
        for ch in range(n):
            rows = slice(ch * S5_CHUNK, (ch + 1) * S5_CHUNK)
            outs[0][rows, lanes] = (extras[0][rows, lanes]
                                    + y_ref[k, pl.ds(ch, S5_CHUNK, stride=S5_GLU_PITCH), :])


def _s5_glu(g, w_glu, h, side_casts=()):
    L, nc, _ = g.shape
    t = nc * L
    tm = min(S5_TM, t)
    (out,), casts = _row_matmul(
        "s5_glu", t // tm, g, pl.BlockSpec((L, tm // L, D_MODEL), lambda i: (0, i, 0)), tm, w_glu,
        (0, D_MODEL), 512, D_MODEL // 512,
        extras=[(h, _rows(tm, D_MODEL))],
        outs=[(jax.ShapeDtypeStruct((t, D_MODEL), F32), _rows(tm, D_MODEL))],
        epilogue=_glu_epilogue,
        scratch=[pltpu.VMEM((512 // LANES, S5_CHUNK * S5_GLU_PITCH, LANES), F32)],
        side_casts=side_casts)
    return out, casts


def _s5_layer(h, mix_norm, w_in, a_re, a_im, log_dt, b_re, b_im, c_re, c_im, d_skip, w_glu,
              side_casts=()):
    u = _s5_in(h, mix_norm, w_in.astype(BF16))
    g = _s5_ssm(u, a_re, a_im, log_dt, b_re, b_im, c_re, c_im, d_skip)
    return _s5_glu(g, w_glu.astype(BF16), h, side_casts)


def kernel(x, p, positions,
           l0_mix_norm, l0_w_qkv, l0_q_norm, l0_k_norm, l0_sinks, l0_w_o,
           l0_mlp_norm, l0_w_up, l0_w_down, l0_ple_norm, l0_w_ple_gate, l0_w_ple,
           l1_mix_norm, l1_w_in, l1_conv_w, l1_conv_b, l1_w_gate_a, l1_b_gate_a, l1_w_gate_x,
           l1_b_gate_x, l1_lru_lambda, l1_w_o,
           l1_mlp_norm, l1_w_up, l1_w_down, l1_ple_norm, l1_w_ple_gate, l1_w_ple,
           l2_mix_norm, l2_w_in, l2_a_re, l2_a_im, l2_log_dt, l2_b_re, l2_b_im, l2_c_re, l2_c_im,
           l2_d_skip, l2_w_glu,
           l2_mlp_norm, l2_w_up, l2_w_down, l2_ple_norm, l2_w_ple_gate, l2_w_ple,
           l3_mix_norm, l3_w_qkv, l3_q_norm, l3_k_norm, l3_sinks, l3_w_o,
           l3_mlp_norm, l3_w_up, l3_w_down, l3_ple_norm, l3_w_ple_gate, l3_w_ple):
    batch, t, _ = x.shape
    assert batch == 1 and t % (8 * BLOCK) == 0
    pos = positions.reshape(t).astype(jnp.int32)
    tails = [
        (l0_mlp_norm, l0_w_up, l0_w_down, l0_ple_norm, l0_w_ple_gate, l0_w_ple),
        (l1_mlp_norm, l1_w_up, l1_w_down, l1_ple_norm, l1_w_ple_gate, l1_w_ple),
        (l2_mlp_norm, l2_w_up, l2_w_down, l2_ple_norm, l2_w_ple_gate, l2_w_ple),
        (l3_mlp_norm, l3_w_up, l3_w_down, l3_ple_norm, l3_w_ple_gate, l3_w_ple),
    ]
    h = x.reshape(t, D_MODEL).astype(F32)
    p_all = p.reshape(DEPTH * t, PLE_DIM)
    for i in range(DEPTH):
        if i == 0:
            h, (up16, down16, gate16, l1_w_in16, l1_w_o16) = _attention_layer(
                h, pos, l0_mix_norm, l0_w_qkv, l0_q_norm, l0_k_norm, l0_sinks, l0_w_o,
                side_casts=(l0_w_up, l0_w_down, l0_w_ple_gate, l1_w_in, l1_w_o))
        elif i == 1:
            h, (gate16, l2_w_in16, l2_w_glu16, l2_gate16) = _rglru_layer(
                h, l1_mix_norm, l1_w_in16, l1_conv_w, l1_conv_b, l1_w_gate_a, l1_b_gate_a,
                l1_w_gate_x, l1_b_gate_x, l1_lru_lambda, l1_w_o16,
                side_casts=(l1_w_ple_gate, l2_w_in, l2_w_glu, l2_w_ple_gate))
        elif i == 2:
            gate16 = l2_gate16
            h, (l3_w_qkv16, l3_w_o16, l3_gate16) = _s5_layer(
                h, l2_mix_norm, l2_w_in16, l2_a_re, l2_a_im, l2_log_dt, l2_b_re, l2_b_im,
                l2_c_re, l2_c_im, l2_d_skip, l2_w_glu16,
                side_casts=(l3_w_qkv, l3_w_o, l3_w_ple_gate))
        else:
            gate16 = l3_gate16
            h, _ = _attention_layer(h, pos, l3_mix_norm, l3_w_qkv16, l3_q_norm, l3_k_norm, l3_sinks,
                                    l3_w_o16)
        mlp_norm, _, _, ple_norm, _, w_ple = tails[i]
        next_mlp = tails[i + 1][1:3] if i + 1 < DEPTH else ()
        h, next16 = _mlp(h, mlp_norm, up16, down16, side_casts=next_mlp)
        if next16:
            up16, down16 = next16
        h = _ple(h, ple_norm, p_all, i, w_ple.astype(BF16), gate16)
    return h.reshape(batch, t, D_MODEL).astype(x.dtype)
```

```python
import functools
import math

import jax
import jax.numpy as jnp
from jax import lax
from jax.experimental import pallas as pl
from jax.experimental.pallas import tpu as pltpu

F32 = jnp.float32
BF16 = jnp.bfloat16

D_MODEL = 2048
DEPTH = 4
N_HEADS = 32
N_KV_HEADS = 4
HEAD_DIM = 64
Q_PER_KV = N_HEADS // N_KV_HEADS
WINDOW = 128
BLOCK = 128
PAD_POS = -(1 << 20)
N_GATE_BLOCKS = 8
GATE_BLOCK = D_MODEL // N_GATE_BLOCKS
CONV_WIDTH = 4
LRU_C = 8.0
SSM_GROUP = 16
N_SSM_GROUPS = D_MODEL // SSM_GROUP
SSM_STATE = 64
D_FF = 4 * D_MODEL
PLE_DIM = 256
EPS = 1e-6

LANES = 128
SUBLANES = 8
MXU_DIM = 256
VMEM_LIMIT_BYTES = 60 * 1024 * 1024
MASK_DIST = 1e33

S5_CHUNK = 16
S5_GROUPS_PER_TILE = LANES // SSM_GROUP
S5_STATE_COLS = S5_GROUPS_PER_TILE * SSM_STATE
N_LANE_TILES = D_MODEL // LANES


def _params(*semantics):
    return pltpu.CompilerParams(dimension_semantics=semantics,
                                vmem_limit_bytes=VMEM_LIMIT_BYTES)


def _rms(x, gain):
    ms = jnp.mean(x * x, axis=-1, keepdims=True)
    return x * lax.rsqrt(ms + EPS) * gain


def _gelu_tanh(x):
    return 0.5 * x * (1.0 + jnp.tanh(math.sqrt(2.0 / math.pi) * (x + 0.044715 * (x * x * x))))


def _sigmoid(x):
    return 0.5 * jnp.tanh(0.5 * x) + 0.5


def _dot(a, b):
    return jnp.dot(a, b, preferred_element_type=F32)


def _dot_nt(a, b):
    return lax.dot_general(a, b, (((1,), (1,)), ((), ())), preferred_element_type=F32)


def _resident(shape):
    zeros = (0,) * len(shape)
    return pl.BlockSpec(shape, lambda i: zeros, pipeline_mode=pl.Buffered(1))


def _with_side_casts(kernel, n_in, n_out, n_cast):
    def wrapped(*refs):
        ins = refs[:n_in]
        cast_ins = refs[n_in:n_in + n_cast]
        outs = refs[n_in + n_cast:n_in + n_cast + n_out]
        cast_outs = refs[n_in + n_cast + n_out:n_in + 2 * n_cast + n_out]
        scratch = refs[n_in + 2 * n_cast + n_out:]
        for src, dst in zip(cast_ins, cast_outs):
            if dst.shape == src.shape:
                dst[...] = src[...].astype(BF16)
            else:
                _qkv_rows_to_bf16(src, dst)
        kernel(*ins, *outs, *scratch)

    return wrapped


def _qkv_rows_to_bf16(src_ref, dst_ref):
    dst_ref[:, :D_MODEL] = src_ref[:, :D_MODEL].astype(BF16)
    for j in range(2 * N_KV_HEADS):
        head = src_ref[:, D_MODEL + j * HEAD_DIM:D_MODEL + (j + 1) * HEAD_DIM]
        dst_ref[:, D_MODEL + j * LANES:D_MODEL + (j + 1) * LANES] = (
            jnp.concatenate([head, head], axis=1).astype(BF16))


def _call(kernel, name, grid, semantics, in_specs, args, out_specs, out_shape, scratch=(),
          side_casts=(), step_index=None):
    in_specs = list(in_specs)
    out_specs = list(out_specs)
    out_shape = list(out_shape)
    args = list(args)
    n_in, n_out, n_cast = len(in_specs), len(out_specs), len(side_casts)
    if n_cast:
        n_steps = math.prod(grid)
        if step_index is None:
            step_index = lambda i: i
        for w in side_casts:
            out_cols = QKV_COLS if w.shape[1] == QKV_WIDTH else w.shape[1]
            rows = w.shape[0] // n_steps
            assert rows % 16 == 0 and rows * n_steps == w.shape[0]
            in_specs.append(pl.BlockSpec((rows, w.shape[1]), lambda *g: (step_index(*g), 0)))
            out_specs.append(pl.BlockSpec((rows, out_cols), lambda *g: (step_index(*g), 0)))
            out_shape.append(jax.ShapeDtypeStruct((w.shape[0], out_cols), BF16))
            args.append(w.astype(F32))
        kernel = _with_side_casts(kernel, n_in, n_out, n_cast)
    res = pl.pallas_call(
        kernel,
        grid=grid,
        in_specs=in_specs,
        out_specs=out_specs,
        out_shape=out_shape,
        scratch_shapes=list(scratch),
        compiler_params=_params(*semantics),
        name=name,
    )(*args)
    return list(res[:n_out]), list(res[n_out:])


def _row_matmul_kernel(*refs, n_extra, n_out, has_norm, col_offsets, tn, n_chunks, epilogue):
    lhs_ref = refs[0]
    pos = 1
    if has_norm:
        gain_ref = refs[1]
        pos = 2
    w_ref = refs[pos]
    pos += 1
    extra_refs = refs[pos:pos + n_extra]
    pos += n_extra
    out_refs = refs[pos:pos + n_out]
    pos += n_out
    if has_norm:
        src = refs[pos]
        pos += 1
        src[...] = _rms(lhs_ref[...], gain_ref[...]).astype(BF16)
    else:
        src = lhs_ref
    scratch_refs = refs[pos:]
    for c in range(n_chunks):
        cols = slice(c * tn, (c + 1) * tn)
        a = src[...]
        a = a.reshape(-1, a.shape[-1])
        accs = [_dot(a, w_ref[:, off + c * tn:off + (c + 1) * tn]) for off in col_offsets]
        epilogue(c, cols, accs, lhs_ref, extra_refs, out_refs, *scratch_refs)


def _row_matmul(name, n_blocks, lhs, lhs_spec, tm, w, col_offsets, tn, n_chunks, extras, outs,
                epilogue, gain=None, scratch=(), side_casts=()):
    has_norm = gain is not None
    k_dim = w.shape[0]
    args = [lhs]
    in_specs = [lhs_spec]
    if has_norm:
        args.append(gain.reshape(1, k_dim).astype(F32))
        in_specs.append(_resident((1, k_dim)))
    args.append(w)
    in_specs.append(_resident(w.shape))
    for a, s in extras:
        args.append(a)
        in_specs.append(s)
    kern = functools.partial(_row_matmul_kernel, n_extra=len(extras), n_out=len(outs),
                             has_norm=has_norm, col_offsets=tuple(col_offsets), tn=tn,
                             n_chunks=n_chunks, epilogue=epilogue)
    scratch = ([pltpu.VMEM((tm, k_dim), BF16)] if has_norm else []) + list(scratch)
    return _call(kern, name, (n_blocks,), ("parallel",), in_specs, args, [s for _, s in outs],
                 [o for o, _ in outs], scratch, side_casts)


def _rows(tm, width):
    return pl.BlockSpec((tm, width), lambda i: (i, 0))


QKV_TN = 512
QKV_WIDTH = (N_HEADS + 2 * N_KV_HEADS) * HEAD_DIM
QKV_COLS = D_MODEL + 2 * N_KV_HEADS * LANES
QKV_NORM_BLOCKS = (D_MODEL + N_KV_HEADS * LANES) // QKV_TN
LOG2E = math.log2(math.e)


ATTN_GROUP = 256
ATTN_GROUPS_PER_STEP = 2
K_COL = D_MODEL
V_COL = D_MODEL + N_KV_HEADS * LANES


def _interleave(*task_lists):
    keyed = []
    for n_list, tasks in enumerate(task_lists):
        for k, task in enumerate(tasks):
            keyed.append(((k + 0.5) / len(tasks), n_list, k, task))
    keyed.sort(key=lambda item: item[:3])
    return [item[3] for item in keyed]


def _qkv_tasks(hn_ref, wp_ref, gain_ref, seg_ref, qkv_ref, rows):
    def chunk(c):
        cols = slice(c * QKV_TN, (c + 1) * QKV_TN)
        x = _dot(hn_ref[rows, :], wp_ref[:, cols])
        if c < QKV_NORM_BLOCKS:
            x2 = (x * x).astype(BF16)
            ss = jnp.concatenate(
                [_dot(x2[:, k * MXU_DIM:(k + 1) * MXU_DIM], seg_ref[...])
                 for k in range(QKV_TN // MXU_DIM)], axis=1)
            r = lax.rsqrt(ss * (1.0 / HEAD_DIM) + EPS)
            qkv_ref[rows, cols] = (x * r * gain_ref[:, cols]).astype(BF16)
        else:
            qkv_ref[rows, cols] = x.astype(BF16)

    return [functools.partial(chunk, c) for c in range(QKV_COLS // QKV_TN)]


def _alibi_slopes():
    return [float(2.0 ** (-8.0 * (i + 1) / N_HEADS)) for i in range(N_HEADS)]


def _attend_tasks(qkv_ref, rows, k_prev, v_prev, positions, sinks_ref, o_ref):
    shared = {}

    def prepare():
        qpos, kpos_prev, kpos_cur = positions()
        kpos = jnp.concatenate([kpos_prev, kpos_cur], axis=1)
        dist = qpos - kpos
        valid = (dist >= 0) & (dist < WINDOW)
        lane = lax.broadcasted_iota(jnp.int32, (BLOCK, LANES), 1)
        lo_f = (lane < HEAD_DIM).astype(F32)
        lo = lo_f.astype(BF16)
        hi = (1.0 - lo_f).astype(BF16)
        shared.update(distm=jnp.where(valid, dist.astype(F32), MASK_DIST), lane=lane, lo=lo, hi=hi,
                      lo2=jnp.concatenate([lo, lo], axis=0), hi2=jnp.concatenate([hi, hi], axis=0))

    slopes = _alibi_slopes()

    def head_task(h):
        if not shared:
            prepare()
        distm, lane, lo, hi = shared["distm"], shared["lane"], shared["lo"], shared["hi"]
        lo2, hi2 = shared["lo2"], shared["hi2"]
        kk = jnp.concatenate([k_prev(h), qkv_ref[rows, K_COL + h * LANES:K_COL + (h + 1) * LANES]],
                             axis=0)
        vd = jnp.concatenate([v_prev(h), qkv_ref[rows, V_COL + h * LANES:V_COL + (h + 1) * LANES]],
                             axis=0)
        qs = []
        for g in range(Q_PER_KV):
            col = h * Q_PER_KV * HEAD_DIM + (g // 2) * LANES
            qp = qkv_ref[rows, col:col + LANES]
            qs.append(qp * (lo if g % 2 == 0 else hi))
        s_all = _dot_nt(jnp.concatenate(qs, axis=0), kk)
        ps = []
        rls = []
        for g in range(Q_PER_KV):
            head = h * Q_PER_KV + g
            sg = s_all[g * BLOCK:(g + 1) * BLOCK] - (slopes[head] * LOG2E) * distm
            sink = sinks_ref[head] * LOG2E
            m = jnp.maximum(jnp.max(sg, axis=-1, keepdims=True), sink)
            p = jnp.exp2(sg - m)
            denom = jnp.sum(p, axis=-1, keepdims=True) + jnp.exp2(sink - m)
            ps.append(p.astype(BF16))
            rls.append(1.0 / denom)
        lhs2 = jnp.concatenate(
            [jnp.concatenate([ps[2 * p], ps[2 * p + 1]], axis=1) for p in range(Q_PER_KV // 2)],
            axis=0)
        rhs2 = jnp.concatenate([vd * lo2, vd * hi2], axis=0)
        o2 = _dot(lhs2, rhs2)
        for p in range(Q_PER_KV // 2):
            scale = jnp.where(lane < HEAD_DIM, rls[2 * p], rls[2 * p + 1])
            col = h * Q_PER_KV * HEAD_DIM + p * LANES
            o_ref[rows, col:col + LANES] = (o2[p * BLOCK:(p + 1) * BLOCK] * scale).astype(BF16)

    return [functools.partial(head_task, h) for h in range(N_KV_HEADS)]


def _attn_layer_kernel(h_ref, gain_ref, wp_ref, hg_ref, seg_ref, wo_ref, qpos_ref, kposp_ref,
                       kpos_ref, sinks_ref, out_ref, hn_ref, qkv_ref, o_ref, tail_ref):
    first = pl.program_id(0) == 0

    @pl.when(first)
    def _():
        tail_ref[...] = jnp.zeros(tail_ref.shape, BF16)

    hn_ref[...] = _rms(h_ref[...], gain_ref[...]).astype(BF16)
    kv_width = N_KV_HEADS * LANES
    step_rows = h_ref.shape[0]
    n_groups = step_rows // ATTN_GROUP
    bpg = ATTN_GROUP // BLOCK
    project, attend, emit = [], [], []
    for g in range(n_groups):
        grows = slice(g * ATTN_GROUP, (g + 1) * ATTN_GROUP)
        project.append(_qkv_tasks(hn_ref, wp_ref, hg_ref, seg_ref, qkv_ref, grows))

        def out_chunk(c, grows=grows):
            cols = slice(c * QKV_TN, (c + 1) * QKV_TN)
            out_ref[grows, cols] = h_ref[grows, cols] + _dot(o_ref[grows, :], wo_ref[:, cols])

        emit.append([functools.partial(out_chunk, c) for c in range(D_MODEL // QKV_TN)])
    for b in range(step_rows // BLOCK):
        rows = slice(b * BLOCK, (b + 1) * BLOCK)
        if b == 0:
            k_prev = lambda h: tail_ref[:, h * LANES:(h + 1) * LANES]
            v_prev = lambda h: tail_ref[:, kv_width + h * LANES:kv_width + (h + 1) * LANES]
            positions = lambda rows=rows: (qpos_ref[rows, :],
                                           jnp.where(first, PAD_POS, kposp_ref[...]), kpos_ref[0])
        else:
            prows = slice((b - 1) * BLOCK, b * BLOCK)
            k_prev = lambda h, prows=prows: qkv_ref[prows, K_COL + h * LANES:K_COL + (h + 1) * LANES]
            v_prev = lambda h, prows=prows: qkv_ref[prows, V_COL + h * LANES:V_COL + (h + 1) * LANES]
            positions = lambda rows=rows, b=b: (qpos_ref[rows, :], kpos_ref[b - 1], kpos_ref[b])
        attend.append(_attend_tasks(qkv_ref, rows, k_prev, v_prev, positions, sinks_ref, o_ref))

    for task in project[0]:
        task()
    for g in range(n_groups):
        softmax = [t for b in range(g * bpg, (g + 1) * bpg) for t in attend[b]]
        nxt = project[g + 1] if g + 1 < n_groups else []
        prv = emit[g - 1] if g > 0 else []
        for task in _interleave(softmax, nxt, prv):
            task()
    last = slice(step_rows - BLOCK, step_rows)
    tail_ref[...] = qkv_ref[last, K_COL:K_COL + 2 * kv_width]
    for task in emit[n_groups - 1]:
        task()


def _residual_epilogue(c, cols, accs, lhs_ref, extras, outs):
    outs[0][:, cols] = extras[0][:, cols] + accs[0]


def _proj_residual(a, w, h):
    t, k_dim = a.shape
    tm = min(1024, t)
    (out,) = _row_matmul(
        "proj_residual", t // tm, a, _rows(tm, k_dim), tm, w, (0,), 512, D_MODEL // 512,
        extras=[(h, _rows(tm, D_MODEL))],
        outs=[(jax.ShapeDtypeStruct((t, D_MODEL), F32), _rows(tm, D_MODEL))],
        epilogue=_residual_epilogue)
    return out


def _prep_qkv_weight(w_qkv):
    if w_qkv.shape[1] == QKV_COLS:
        return w_qkv
    w16 = w_qkv.astype(BF16)
    wq = w16[:, :D_MODEL]
    wk = w16[:, D_MODEL:D_MODEL + N_KV_HEADS * HEAD_DIM].reshape(D_MODEL, N_KV_HEADS, HEAD_DIM)
    wv = w16[:, D_MODEL + N_KV_HEADS * HEAD_DIM:].reshape(D_MODEL, N_KV_HEADS, HEAD_DIM)
    dup = lambda w: jnp.concatenate([w, w], axis=-1).reshape(D_MODEL, N_KV_HEADS * LANES)
    return jnp.concatenate([wq, dup(wk), dup(wv)], axis=1)


def _attention_layer(h, positions, mix_norm, w_qkv, q_norm, k_norm, sinks, w_o, side_casts=()):
    head_gain = jnp.concatenate([
        jnp.tile(q_norm.astype(F32), N_HEADS) * (HEAD_DIM ** -0.5 * LOG2E),
        jnp.tile(k_norm.astype(F32), 2 * N_KV_HEADS),
        jnp.ones((N_KV_HEADS * LANES,), F32)]).reshape(1, QKV_COLS)
    seg_ones = jnp.kron(jnp.eye(MXU_DIM // HEAD_DIM, dtype=F32),
                        jnp.ones((HEAD_DIM, HEAD_DIM), F32)).astype(BF16)
    t = h.shape[0]
    rows = ATTN_GROUP * (1 if side_casts else ATTN_GROUPS_PER_STEP)
    n_steps = t // rows
    bps = rows // BLOCK
    pos_row = positions.reshape(t // BLOCK, 1, BLOCK)
    in_specs = [
        _rows(rows, D_MODEL),
        _resident((1, D_MODEL)),
        _resident((D_MODEL, QKV_COLS)),
        _resident((1, QKV_COLS)),
        _resident((MXU_DIM, MXU_DIM)),
        _resident((D_MODEL, D_MODEL)),
        pl.BlockSpec((rows, 1), lambda i: (i, 0)),
        pl.BlockSpec((None, 1, BLOCK), lambda i: (jnp.maximum(bps * i - 1, 0), 0, 0)),
        pl.BlockSpec((bps, 1, BLOCK), lambda i: (i, 0, 0)),
        pl.BlockSpec(memory_space=pltpu.SMEM),
    ]
    args = [h, mix_norm.reshape(1, D_MODEL).astype(F32), _prep_qkv_weight(w_qkv), head_gain, seg_ones,
            w_o.astype(BF16), positions.reshape(t, 1), pos_row, pos_row, sinks.astype(F32)]
    scratch = [pltpu.VMEM((rows, D_MODEL), BF16), pltpu.VMEM((rows, QKV_COLS), BF16),
               pltpu.VMEM((rows, D_MODEL), BF16), pltpu.VMEM((BLOCK, 2 * N_KV_HEADS * LANES), BF16)]
    (out,), casts = _call(_attn_layer_kernel, "attn_layer", (n_steps,), ("arbitrary",), in_specs, args,
                          [_rows(rows, D_MODEL)], [jax.ShapeDtypeStruct((t, D_MODEL), F32)], scratch,
                          side_casts)
    return out, casts


MLP_TF = 512


def _mlp_kernel(h_ref, gain_ref, wu_ref, wd_ref, o_ref, hn_ref):
    j = pl.program_id(1)

    @pl.when(j == 0)
    def _():
        x = h_ref[...]
        hn_ref[...] = _rms(x, gain_ref[...]).astype(BF16)
        o_ref[...] = x

    act = jnp.maximum(_dot(hn_ref[...], wu_ref[...]), 0.0)
    act = (act * act).astype(BF16)
    o_ref[...] += _dot(act, wd_ref[...])


def _mlp(h, gain, w_up16, w_down16, side_casts=()):
    t = h.shape[0]
    tm = min(1024, t)
    tf = MLP_TF
    nj = D_FF // tf
    in_specs = [
        pl.BlockSpec((tm, D_MODEL), lambda i, j: (i, 0)),
        pl.BlockSpec((1, D_MODEL), lambda i, j: (0, 0)),
        pl.BlockSpec((D_MODEL, tf), lambda i, j: (0, j)),
        pl.BlockSpec((tf, D_MODEL), lambda i, j: (j, 0)),
    ]
    args = [h, gain.reshape(1, D_MODEL).astype(F32), w_up16, w_down16]
    (out,), casts = _call(_mlp_kernel, "mlp", (t // tm, nj), ("parallel", "arbitrary"), in_specs, args,
                          [pl.BlockSpec((tm, D_MODEL), lambda i, j: (i, 0))],
                          [jax.ShapeDtypeStruct((t, D_MODEL), F32)],
                          [pltpu.VMEM((tm, D_MODEL), BF16)], side_casts,
                          step_index=lambda i, j: i * nj + j)
    return out, casts


def _ple_epilogue(c, cols, accs, lhs_ref, extras, outs):
    p_ref, wple_ref = extras
    emb = _dot(p_ref[...].astype(BF16), wple_ref[:, cols])
    outs[0][:, cols] = lhs_ref[:, cols] + emb * _sigmoid(accs[0])


def _ple(h, gain, p_all, layer, w_ple, w_gate):
    t = h.shape[0]
    tm = min(1024, t)
    first = layer * (t // tm)
    (out,), _ = _row_matmul(
        "ple", t // tm, h, _rows(tm, D_MODEL), tm, w_gate, (0,), 512, D_MODEL // 512,
        extras=[(p_all, pl.BlockSpec((tm, PLE_DIM), lambda i: (first + i, 0))),
                (w_ple, _resident((PLE_DIM, D_MODEL)))],
        outs=[(jax.ShapeDtypeStruct((t, D_MODEL), F32), _rows(tm, D_MODEL))],
        epilogue=_ple_epilogue, gain=gain)
    return out


def _rglru_in_epilogue(c, cols, accs, lhs_ref, extras, outs):
    outs[0][:, cols] = accs[0]
    outs[1][:, cols] = _gelu_tanh(accs[1]).astype(BF16)


def _rglru_in(h, mix_norm, w_in):
    t = h.shape[0]
    tm = min(512, t)
    return _row_matmul(
        "rglru_in", t // tm, h, _rows(tm, D_MODEL), tm, w_in, (0, D_MODEL), 512, D_MODEL // 512,
        extras=[],
        outs=[(jax.ShapeDtypeStruct((t, D_MODEL), F32), _rows(tm, D_MODEL)),
              (jax.ShapeDtypeStruct((t, D_MODEL), BF16), _rows(tm, D_MODEL))],
        epilogue=_rglru_in_epilogue, gain=mix_norm)


RGLRU_TB = 256
RGLRU_SEG = RGLRU_TB // SUBLANES
RGLRU_PITCH = 36
TINY = 1e-36


def _rglru_scan_kernel(xb_ref, gate_ref, cw_ref, cb_ref, wa_ref, ba_ref, wx_ref, bx_ref, lam_ref,
                       y_ref, a_ref, b_ref, carry_ref, tail_ref):
    tb = RGLRU_TB
    seg = RGLRU_SEG
    n_slabs = D_MODEL // LANES

    @pl.when(pl.program_id(0) == 0)
    def _():
        carry_ref[...] = jnp.zeros((1, D_MODEL), F32)
        tail_ref[...] = jnp.zeros((CONV_WIDTH - 1, D_MODEL), F32)

    def shift1(v, k):
        prev_last = tail_ref[k:k + 1, :]
        tail_ref[k:k + 1, :] = v[tb - 1:tb, :]
        return jnp.concatenate([prev_last, v[:tb - 1, :]], axis=0)

    x = xb_ref[...]
    acc = cw_ref[0:1, :] * x
    for k in range(1, CONV_WIDTH):
        acc = cw_ref[k:k + 1, :] * x + shift1(acc, k - 1)
    xc = acc + cb_ref[...]

    lam = lam_ref[...]
    rate = (-LRU_C) * (jnp.log(1.0 + jnp.exp(-jnp.abs(lam))) + jnp.maximum(-lam, 0.0))
    for n in range(N_GATE_BLOCKS):
        sl = slice(n * GATE_BLOCK, (n + 1) * GATE_BLOCK)
        xn = xc[:, sl]
        xn16 = xn.astype(BF16)
        r = _sigmoid(_dot(xn16, wa_ref[n]) + ba_ref[:, sl])
        i_gate = _sigmoid(_dot(xn16, wx_ref[n]) + bx_ref[:, sl])
        a = jnp.exp(rate[:, sl] * r)
        v = 1.0 - a * a
        b = (v * lax.rsqrt(jnp.maximum(v, TINY))) * (i_gate * xn)
        for kk in range(GATE_BLOCK // LANES):
            slab = n * (GATE_BLOCK // LANES) + kk
            for s in range(SUBLANES):
                rows = slice(s * RGLRU_PITCH, s * RGLRU_PITCH + seg)
                a_ref[slab, rows, :] = a[s * seg:(s + 1) * seg, kk * LANES:(kk + 1) * LANES]
                b_ref[slab, rows, :] = b[s * seg:(s + 1) * seg, kk * LANES:(kk + 1) * LANES]

    def step(q, carry):
        hs, prods = carry
        rows = pl.ds(q, SUBLANES, stride=RGLRU_PITCH)
        new_hs = []
        new_prods = []
        for k in range(n_slabs):
            av = a_ref[k, rows, :]
            hk = av * hs[k] + b_ref[k, rows, :]
            pk = av * prods[k]
            b_ref[k, rows, :] = hk
            a_ref[k, rows, :] = pk
            new_hs.append(hk)
            new_prods.append(pk)
        return tuple(new_hs), tuple(new_prods)

    zeros = tuple(jnp.zeros((SUBLANES, LANES), F32) for _ in range(n_slabs))
    ones = tuple(jnp.ones((SUBLANES, LANES), F32) for _ in range(n_slabs))
    hs, prods = lax.fori_loop(0, seg, step, (zeros, ones))

    for k in range(n_slabs):
        lanes = slice(k * LANES, (k + 1) * LANES)
        cur = carry_ref[:, lanes]
        for s in range(SUBLANES):
            rows = slice(s * RGLRU_PITCH, s * RGLRU_PITCH + seg)
            hfull = b_ref[k, rows, :] + a_ref[k, rows, :] * cur
            out_rows = slice(s * seg, (s + 1) * seg)
            y_ref[out_rows, lanes] = (hfull * gate_ref[out_rows, lanes].astype(F32)).astype(BF16)
            cur = prods[k][s:s + 1, :] * cur + hs[k][s:s + 1, :]
        carry_ref[:, lanes] = cur


def _rglru_scan(xb, gate, conv_w, conv_b, w_gate_a, b_gate_a, w_gate_x, b_gate_x, lru_lambda):
    t = xb.shape[0]
    tb = RGLRU_TB
    row = lambda v: v.reshape(1, D_MODEL).astype(F32)
    return pl.pallas_call(
        _rglru_scan_kernel,
        grid=(t // tb,),
        in_specs=[
            _rows(tb, D_MODEL), _rows(tb, D_MODEL),
            _resident((CONV_WIDTH, D_MODEL)), _resident((1, D_MODEL)),
            _resident((N_GATE_BLOCKS, GATE_BLOCK, GATE_BLOCK)), _resident((1, D_MODEL)),
            _resident((N_GATE_BLOCKS, GATE_BLOCK, GATE_BLOCK)), _resident((1, D_MODEL)),
            _resident((1, D_MODEL)),
        ],
        out_specs=_rows(tb, D_MODEL),
        out_shape=jax.ShapeDtypeStruct((t, D_MODEL), BF16),
        scratch_shapes=[pltpu.VMEM((D_MODEL // LANES, SUBLANES * RGLRU_PITCH, LANES), F32),
                        pltpu.VMEM((D_MODEL // LANES, SUBLANES * RGLRU_PITCH, LANES), F32),
                        pltpu.VMEM((1, D_MODEL), F32),
                        pltpu.VMEM((CONV_WIDTH - 1, D_MODEL), F32)],
        compiler_params=_params("arbitrary"),
        name="rglru_scan",
    )(xb, gate, conv_w.astype(F32), row(conv_b), w_gate_a.astype(BF16), row(b_gate_a),
      w_gate_x.astype(BF16), row(b_gate_x), row(lru_lambda))


def _rglru_layer_kernel(h_ref, gain_ref, win_ref, cw_ref, cb_ref, wa_ref, ba_ref, wx_ref, bx_ref,
                        lam_ref, wo_ref, out_ref, hn_ref, xb_ref, gate_ref, y_ref, a_ref, b_ref,
                        carry_ref, tail_ref):
    tb = RGLRU_TB
    seg = RGLRU_SEG
    n_slabs = D_MODEL // LANES
    spb = GATE_BLOCK // LANES

    @pl.when(pl.program_id(0) == 0)
    def _():
        carry_ref[...] = jnp.zeros((1, D_MODEL), F32)
        tail_ref[...] = jnp.zeros((CONV_WIDTH - 1, D_MODEL), F32)

    hn_ref[...] = _rms(h_ref[...], gain_ref[...]).astype(BF16)
    lam = lam_ref[...]
    rate = (-LRU_C) * (jnp.log(1.0 + jnp.exp(-jnp.abs(lam))) + jnp.maximum(-lam, 0.0))

    def project(n):
        sl = slice(n * GATE_BLOCK, (n + 1) * GATE_BLOCK)
        xb_ref[:, sl] = _dot(hn_ref[...], win_ref[:, sl])
        gate_ref[:, sl] = _gelu_tanh(
            _dot(hn_ref[...], win_ref[:, D_MODEL + n * GATE_BLOCK:D_MODEL + (n + 1) * GATE_BLOCK])
        ).astype(BF16)

    def gates(n):
        sl = slice(n * GATE_BLOCK, (n + 1) * GATE_BLOCK)

        def shift1(v, k):
            prev_last = tail_ref[k:k + 1, sl]
            tail_ref[k:k + 1, sl] = v[tb - 1:tb, :]
            return jnp.concatenate([prev_last, v[:tb - 1, :]], axis=0)

        x = xb_ref[:, sl]
        acc = cw_ref[0:1, sl] * x
        for k in range(1, CONV_WIDTH):
            acc = cw_ref[k:k + 1, sl] * x + shift1(acc, k - 1)
        xn = acc + cb_ref[:, sl]
        xn16 = xn.astype(BF16)
        r = _sigmoid(_dot(xn16, wa_ref[n]) + ba_ref[:, sl])
        i_gate = _sigmoid(_dot(xn16, wx_ref[n]) + bx_ref[:, sl])
        a = jnp.exp(rate[:, sl] * r)
        v = 1.0 - a * a
        b = (v * lax.rsqrt(jnp.maximum(v, TINY))) * (i_gate * xn)
        for kk in range(spb):
            slab = n * spb + kk
            for s in range(SUBLANES):
                rows = slice(s * RGLRU_PITCH, s * RGLRU_PITCH + seg)
                a_ref[slab, rows, :] = a[s * seg:(s + 1) * seg, kk * LANES:(kk + 1) * LANES]
                b_ref[slab, rows, :] = b[s * seg:(s + 1) * seg, kk * LANES:(kk + 1) * LANES]

    project(0)
    for n in range(N_GATE_BLOCKS):
        if n + 1 < N_GATE_BLOCKS:
            project(n + 1)
        gates(n)

    def step(q, carry):
        hs, prods = carry
        rows = pl.ds(q, SUBLANES, stride=RGLRU_PITCH)
        new_hs = []
        new_prods = []
        for k in range(n_slabs):
            av = a_ref[k, rows, :]
            hk = av * hs[k] + b_ref[k, rows, :]
            pk = av * prods[k]
            b_ref[k, rows, :] = hk
            a_ref[k, rows, :] = pk
            new_hs.append(hk)
            new_prods.append(pk)
        return tuple(new_hs), tuple(new_prods)

    zeros = tuple(jnp.zeros((SUBLANES, LANES), F32) for _ in range(n_slabs))
    ones = tuple(jnp.ones((SUBLANES, LANES), F32) for _ in range(n_slabs))
    hs, prods = lax.fori_loop(0, seg, step, (zeros, ones))

    for k in range(n_slabs):
        lanes = slice(k * LANES, (k + 1) * LANES)
        cur = carry_ref[:, lanes]
        for s in range(SUBLANES):
            rows = slice(s * RGLRU_PITCH, s * RGLRU_PITCH + seg)
            hfull = b_ref[k, rows, :] + a_ref[k, rows, :] * cur
            out_rows = slice(s * seg, (s + 1) * seg)
            y_ref[out_rows, lanes] = (hfull * gate_ref[out_rows, lanes].astype(F32)).astype(BF16)
            cur = prods[k][s:s + 1, :] * cur + hs[k][s:s + 1, :]
        carry_ref[:, lanes] = cur

    for c in range(D_MODEL // 512):
        cols = slice(c * 512, (c + 1) * 512)
        out_ref[:, cols] = h_ref[:, cols] + _dot(y_ref[...], wo_ref[:, cols])


def _rglru_layer(h, mix_norm, w_in, conv_w, conv_b, w_gate_a, b_gate_a, w_gate_x, b_gate_x,
                 lru_lambda, w_o, side_casts=()):
    t = h.shape[0]
    tb = RGLRU_TB
    row = lambda v: v.reshape(1, D_MODEL).astype(F32)
    scan_scratch = pltpu.VMEM((D_MODEL // LANES, SUBLANES * RGLRU_PITCH, LANES), F32)
    in_specs = [
        _rows(tb, D_MODEL), _resident((1, D_MODEL)), _resident((D_MODEL, 2 * D_MODEL)),
        _resident((CONV_WIDTH, D_MODEL)), _resident((1, D_MODEL)),
        _resident((N_GATE_BLOCKS, GATE_BLOCK, GATE_BLOCK)), _resident((1, D_MODEL)),
        _resident((N_GATE_BLOCKS, GATE_BLOCK, GATE_BLOCK)), _resident((1, D_MODEL)),
        _resident((1, D_MODEL)), _resident((D_MODEL, D_MODEL)),
    ]
    args = [h, row(mix_norm), w_in.astype(BF16), conv_w.astype(F32), row(conv_b),
            w_gate_a.astype(BF16), row(b_gate_a), w_gate_x.astype(BF16), row(b_gate_x),
            row(lru_lambda), w_o.astype(BF16)]
    scratch = [pltpu.VMEM((tb, D_MODEL), BF16), pltpu.VMEM((tb, D_MODEL), F32),
               pltpu.VMEM((tb, D_MODEL), BF16), pltpu.VMEM((tb, D_MODEL), BF16),
               scan_scratch, scan_scratch,
               pltpu.VMEM((1, D_MODEL), F32), pltpu.VMEM((CONV_WIDTH - 1, D_MODEL), F32)]
    (out,), casts = _call(_rglru_layer_kernel, "rglru_layer", (t // tb,), ("arbitrary",), in_specs,
                          args, [_rows(tb, D_MODEL)], [jax.ShapeDtypeStruct((t, D_MODEL), F32)],
                          scratch, side_casts)
    return out, casts


S5_TM = 512
S5_IN_PITCH = 20
S5_GLU_PITCH = 36


def _s5_in_epilogue(c, cols, accs, lhs_ref, extras, outs, acc_ref):
    n = accs[0].shape[0] // S5_CHUNK
    for k in range(acc_ref.shape[0]):
        for ch in range(n):
            acc_ref[k, ch * S5_IN_PITCH:ch * S5_IN_PITCH + S5_CHUNK, :] = (
                accs[0][ch * S5_CHUNK:(ch + 1) * S5_CHUNK, k * LANES:(k + 1) * LANES])
        for l in range(S5_CHUNK):
            outs[0][l, :, cols.start + k * LANES:cols.start + (k + 1) * LANES] = (
                acc_ref[k, pl.ds(l, n, stride=S5_IN_PITCH), :])


def _s5_in(h, mix_norm, w_in):
    t = h.shape[0]
    nc = t // S5_CHUNK
    tm = min(S5_TM, t)
    tn = 512
    (u,), _ = _row_matmul(
        "s5_in", t // tm, h, _rows(tm, D_MODEL), tm, w_in, (0,), tn, D_MODEL // tn,
        extras=[],
        outs=[(jax.ShapeDtypeStruct((S5_CHUNK, nc, D_MODEL), F32),
               pl.BlockSpec((S5_CHUNK, tm // S5_CHUNK, D_MODEL), lambda i: (0, i, 0)))],
        epilogue=_s5_in_epilogue, gain=mix_norm,
        scratch=[pltpu.VMEM((tn // LANES, (tm // S5_CHUNK) * S5_IN_PITCH, LANES), F32)])
    return u


def _s5_ssm_kernel(u_ref, lr_ref, li_ref, ldt_ref, bre_ref, bim_ref, cre_ref, cim_ref, dsk_ref,
                   g_ref, tbig_ref, wend_ref, wc_ref, x_ref, e_ref, s_ref, y_ref):
    nc = u_ref.shape[1]
    L = S5_CHUNK
    P = S5_STATE_COLS
    lr = lr_ref[...]
    li = li_ref[...]
    dt = jnp.exp(ldt_ref[...])
    mag = jnp.exp(lr * dt)
    abar_re = mag * jnp.cos(li * dt)
    abar_im = mag * jnp.sin(li * dt)
    n_re = abar_re - 1.0
    n_im = abar_im
    den = lr * lr + li * li
    z_re = (n_re * lr + n_im * li) / den
    z_im = (n_im * lr - n_re * li) / den
    bre = bre_ref[...]
    bim = bim_ref[...]
    bbar_re = z_re * bre - z_im * bim
    bbar_im = z_re * bim + z_im * bre
    cre = cre_ref[...]
    cim = cim_ref[...]

    def power(d):
        m = jnp.exp((d * dt) * lr)
        ang = (d * dt) * li
        return m * jnp.cos(ang), m * jnp.sin(ang)

    def w_in_state(d):
        pr, pi = power(d)
        return jnp.concatenate([pr * bbar_re - pi * bbar_im, pr * bbar_im + pi * bbar_re],
                               axis=1).astype(BF16)

    def w_out_state(d):
        pr, pi = power(d)
        return jnp.concatenate([cre * pr - cim * pi, -(cre * pi + cim * pr)], axis=1).astype(BF16)

    wt0 = w_out_state(0)
    for d in range(L):
        wj = w_in_state(d)
        m_d = _dot_nt(wj, wt0).astype(BF16)
        for jj in range(L - d):
            tt = jj + d
            tbig_ref[jj * LANES:(jj + 1) * LANES, tt * LANES:(tt + 1) * LANES] = m_d
        wend_ref[(L - 1 - d) * LANES:(L - d) * LANES, :] = wj
        wc_ref[d * LANES:(d + 1) * LANES, :] = w_out_state(d + 1)
    for tt in range(0, L, 2):
        tbig_ref[(tt + 1) * LANES:(tt + 2) * LANES, tt * LANES:(tt + 1) * LANES] = (
            jnp.zeros((LANES, LANES), BF16))

    x_ref[...] = jnp.concatenate([u_ref[l] for l in range(L)], axis=1).astype(BF16)
    e_ref[...] = _dot(x_ref[...], wend_ref[...])

    row = lax.broadcasted_iota(jnp.int32, (SUBLANES, P), 0)
    steps = []
    for s in (1, 2, 4):
        pr, pi = power(L * s)
        steps.append((s, row >= s, pr, pi))
    carry_pw = [power(L * (r + 1)) for r in range(SUBLANES)]
    cpw_re = jnp.concatenate([c[0] for c in carry_pw], axis=0)
    cpw_im = jnp.concatenate([c[1] for c in carry_pw], axis=0)

    s_ref[0:SUBLANES, :] = jnp.zeros((SUBLANES, 2 * P), F32)

    def tile_step(i, carry):
        s_re, s_im = carry
        base = i * SUBLANES
        e = e_ref[base:base + SUBLANES, :]
        t_re = e[:, :P]
        t_im = e[:, P:]
        for s, keep, pr, pi in steps:
            sh_re = jnp.where(keep, pltpu.roll(t_re, s, 0), 0.0)
            sh_im = jnp.where(keep, pltpu.roll(t_im, s, 0), 0.0)
            t_re, t_im = t_re + pr * sh_re - pi * sh_im, t_im + pr * sh_im + pi * sh_re
        f_re = t_re + cpw_re * s_re - cpw_im * s_im
        f_im = t_im + cpw_re * s_im + cpw_im * s_re
        s_ref[base + SUBLANES:base + 2 * SUBLANES, :] = jnp.concatenate([f_re, f_im], axis=1)
        return f_re[SUBLANES - 1:SUBLANES, :], f_im[SUBLANES - 1:SUBLANES, :]

    n_tiles = nc // SUBLANES
    tiles_per_block = n_tiles // (L // 2)
    carry = (jnp.zeros((1, P), F32), jnp.zeros((1, P), F32))
    for b in range(L // 2):
        k_rows = (2 * b + 2) * LANES
        cols = slice(2 * b * LANES, (2 * b + 2) * LANES)
        y_ref[:, cols] = _dot(x_ref[:, :k_rows], tbig_ref[:k_rows, cols])
        for i in range(b * tiles_per_block, (b + 1) * tiles_per_block):
            carry = tile_step(i, carry)

    s_start = s_ref[SUBLANES - 1:SUBLANES - 1 + nc, :].astype(BF16)
    dsk = dsk_ref[...]
    for b in range(L // 2):
        cols = slice(2 * b * LANES, (2 * b + 2) * LANES)
        y = y_ref[:, cols] + _dot_nt(s_start, wc_ref[cols, :])
        for l in (2 * b, 2 * b + 1):
            yl = y[:, (l - 2 * b) * LANES:(l - 2 * b + 1) * LANES] + dsk * u_ref[l]
            g_ref[l] = _gelu_tanh(yl).astype(BF16)


def _s5_ssm(u, a_re, a_im, log_dt, b_re, b_im, c_re, c_im, d_skip):
    L, nc, _ = u.shape
    P = S5_STATE_COLS
    gpt = S5_GROUPS_PER_TILE
    nt = N_LANE_TILES
    f = lambda v: v.astype(F32)
    row = lambda v: f(v).reshape(nt, 1, P)
    ldt = jnp.broadcast_to(f(log_dt)[:, None], (N_SSM_GROUPS, SSM_STATE))
    same_group = (jnp.arange(LANES)[:, None] // SSM_GROUP) == (jnp.arange(P)[None, :] // SSM_STATE)

    def block_diag(x):
        return jnp.where(same_group, jnp.tile(x, (1, 1, gpt)), 0.0)

    def b_layout(b):
        bt = f(b).reshape(nt, gpt, SSM_STATE, SSM_GROUP).transpose(0, 1, 3, 2)
        return block_diag(bt.reshape(nt, LANES, SSM_STATE))

    def c_layout(c):
        return block_diag(f(c).reshape(nt, LANES, SSM_STATE))

    tile_row = pl.BlockSpec((None, 1, P), lambda k: (k, 0, 0))
    tile_mat = pl.BlockSpec((None, LANES, P), lambda k: (k, 0, 0))
    return pl.pallas_call(
        _s5_ssm_kernel,
        grid=(nt,),
        in_specs=[pl.BlockSpec((L, nc, LANES), lambda k: (0, 0, k)),
                  tile_row, tile_row, tile_row, tile_mat, tile_mat, tile_mat, tile_mat,
                  pl.BlockSpec((1, LANES), lambda k: (0, k))],
        out_specs=pl.BlockSpec((L, nc, LANES), lambda k: (0, 0, k)),
        out_shape=jax.ShapeDtypeStruct((L, nc, D_MODEL), BF16),
        scratch_shapes=[pltpu.VMEM((L * LANES, L * LANES), BF16),
                        pltpu.VMEM((L * LANES, 2 * P), BF16),
                        pltpu.VMEM((L * LANES, 2 * P), BF16),
                        pltpu.VMEM((nc, L * LANES), BF16),
                        pltpu.VMEM((nc, 2 * P), F32),
                        pltpu.VMEM((nc + SUBLANES, 2 * P), F32),
                        pltpu.VMEM((nc, L * LANES), F32)],
        compiler_params=_params("arbitrary"),
        name="s5_ssm",
    )(u, row(a_re), row(a_im), row(ldt), b_layout(b_re), b_layout(b_im), c_layout(c_re),
      c_layout(c_im), f(d_skip).reshape(1, D_MODEL))


def _glu_epilogue(c, cols, accs, lhs_ref, extras, outs, y_ref):
    y = accs[0] * _sigmoid(accs[1])
    n = y.shape[0] // S5_CHUNK
    assert n <= S5_GLU_PITCH
    for k in range(y_ref.shape[0]):
        for l in range(S5_CHUNK):
            y_ref[k, l * S5_GLU_PITCH:l * S5_GLU_PITCH + n, :] = (
                y[l * n:(l + 1) * n, k * LANES:(k + 1) * LANES])
        lanes = slice(cols.start + k * LANES, cols.start + (k + 1) * LANES)
        for ch in range(n):
            rows = slice(ch * S5_CHUNK, (ch + 1) * S5_CHUNK)
            outs[0][rows, lanes] = (extras[0][rows, lanes]
                                    + y_ref[k, pl.ds(ch, S5_CHUNK, stride=S5_GLU_PITCH), :])


def _s5_glu(g, w_glu, h, side_casts=()):
    L, nc, _ = g.shape
    t = nc * L
    tm = min(S5_TM, t)
    (out,), casts = _row_matmul(
        "s5_glu", t // tm, g, pl.BlockSpec((L, tm // L, D_MODEL), lambda i: (0, i, 0)), tm, w_glu,
        (0, D_MODEL), 512, D_MODEL // 512,
        extras=[(h, _rows(tm, D_MODEL))],
        outs=[(jax.ShapeDtypeStruct((t, D_MODEL), F32), _rows(tm, D_MODEL))],
        epilogue=_glu_epilogue,
        scratch=[pltpu.VMEM((512 // LANES, S5_CHUNK * S5_GLU_PITCH, LANES), F32)],
        side_casts=side_casts)
    return out, casts


def _s5_layer(h, mix_norm, w_in, a_re, a_im, log_dt, b_re, b_im, c_re, c_im, d_skip, w_glu,
              side_casts=()):
    u = _s5_in(h, mix_norm, w_in.astype(BF16))
    g = _s5_ssm(u, a_re, a_im, log_dt, b_re, b_im, c_re, c_im, d_skip)
    return _s5_glu(g, w_glu.astype(BF16), h, side_casts)


def kernel(x, p, positions,
           l0_mix_norm, l0_w_qkv, l0_q_norm, l0_k_norm, l0_sinks, l0_w_o,
           l0_mlp_norm, l0_w_up, l0_w_down, l0_ple_norm, l0_w_ple_gate, l0_w_ple,
           l1_mix_norm, l1_w_in, l1_conv_w, l1_conv_b, l1_w_gate_a, l1_b_gate_a, l1_w_gate_x,
           l1_b_gate_x, l1_lru_lambda, l1_w_o,
           l1_mlp_norm, l1_w_up, l1_w_down, l1_ple_norm, l1_w_ple_gate, l1_w_ple,
           l2_mix_norm, l2_w_in, l2_a_re, l2_a_im, l2_log_dt, l2_b_re, l2_b_im, l2_c_re, l2_c_im,
           l2_d_skip, l2_w_glu,
           l2_mlp_norm, l2_w_up, l2_w_down, l2_ple_norm, l2_w_ple_gate, l2_w_ple,
           l3_mix_norm, l3_w_qkv, l3_q_norm, l3_k_norm, l3_sinks, l3_w_o,
           l3_mlp_norm, l3_w_up, l3_w_down, l3_ple_norm, l3_w_ple_gate, l3_w_ple):
    batch, t, _ = x.shape
    assert batch == 1 and t % (8 * BLOCK) == 0
    pos = positions.reshape(t).astype(jnp.int32)
    tails = [
        (l0_mlp_norm, l0_w_up, l0_w_down, l0_ple_norm, l0_w_ple_gate, l0_w_ple),
        (l1_mlp_norm, l1_w_up, l1_w_down, l1_ple_norm, l1_w_ple_gate, l1_w_ple),
        (l2_mlp_norm, l2_w_up, l2_w_down, l2_ple_norm, l2_w_ple_gate, l2_w_ple),
        (l3_mlp_norm, l3_w_up, l3_w_down, l3_ple_norm, l3_w_ple_gate, l3_w_ple),
    ]
    h = x.reshape(t, D_MODEL).astype(F32)
    p_all = p.reshape(DEPTH * t, PLE_DIM)
    for i in range(DEPTH):
        if i == 0:
            h, (up16, down16, gate16, l1_w_in16, l1_w_o16) = _attention_layer(
                h, pos, l0_mix_norm, l0_w_qkv, l0_q_norm, l0_k_norm, l0_sinks, l0_w_o,
                side_casts=(l0_w_up, l0_w_down, l0_w_ple_gate, l1_w_in, l1_w_o))
        elif i == 1:
            h, (gate16, l2_w_in16, l2_w_glu16, l2_gate16) = _rglru_layer(
                h, l1_mix_norm, l1_w_in16, l1_conv_w, l1_conv_b, l1_w_gate_a, l1_b_gate_a,
                l1_w_gate_x, l1_b_gate_x, l1_lru_lambda, l1_w_o16,
                side_casts=(l1_w_ple_gate, l2_w_in, l2_w_glu, l2_w_ple_gate))
        elif i == 2:
            gate16 = l2_gate16
            h, (l3_w_qkv16, l3_w_o16, l3_gate16) = _s5_layer(
                h, l2_mix_norm, l2_w_in16, l2_a_re, l2_a_im, l2_log_dt, l2_b_re, l2_b_im,
                l2_c_re, l2_c_im, l2_d_skip, l2_w_glu16,
                side_casts=(l3_w_qkv, l3_w_o, l3_w_ple_gate))
        else:
            gate16 = l3_gate16
            h, _ = _attention_layer(h, pos, l3_mix_norm, l3_w_qkv16, l3_q_norm, l3_k_norm, l3_sinks,
                                    l3_w_o16)
        mlp_norm, _, _, ple_norm, _, w_ple = tails[i]
        next_mlp = tails[i + 1][1:3] if i + 1 < DEPTH else ()
        h, next16 = _mlp(h, mlp_norm, up16, down16, side_casts=next_mlp)
        if next16:
            up16, down16 = next16
        h = _ple(h, ple_norm, p_all, i, w_ple.astype(BF16), gate16)
    return h.reshape(batch, t, D_MODEL).astype(x.dtype)
```

```python
import functools
import math

import jax
import jax.numpy as jnp
from jax import lax
from jax.experimental import pallas as pl
from jax.experimental.pallas import tpu as pltpu

F32 = jnp.float32
BF16 = jnp.bfloat16

D_MODEL = 2048
DEPTH = 4
N_HEADS = 32
N_KV_HEADS = 4
HEAD_DIM = 64
Q_PER_KV = N_HEADS // N_KV_HEADS
WINDOW = 128
BLOCK = 128
PAD_POS = -(1 << 20)
N_GATE_BLOCKS = 8
GATE_BLOCK = D_MODEL // N_GATE_BLOCKS
CONV_WIDTH = 4
LRU_C = 8.0
SSM_GROUP = 16
N_SSM_GROUPS = D_MODEL // SSM_GROUP
SSM_STATE = 64
D_FF = 4 * D_MODEL
PLE_DIM = 256
EPS = 1e-6

LANES = 128
SUBLANES = 8
MXU_DIM = 256
VMEM_LIMIT_BYTES = 60 * 1024 * 1024
MASK_DIST = 1e33

S5_CHUNK = 16
S5_GROUPS_PER_TILE = LANES // SSM_GROUP
S5_STATE_COLS = S5_GROUPS_PER_TILE * SSM_STATE
N_LANE_TILES = D_MODEL // LANES


def _params(*semantics):
    return pltpu.CompilerParams(dimension_semantics=semantics,
                                vmem_limit_bytes=VMEM_LIMIT_BYTES)


def _rms(x, gain):
    ms = jnp.mean(x * x, axis=-1, keepdims=True)
    return x * lax.rsqrt(ms + EPS) * gain


def _gelu_tanh(x):
    return 0.5 * x * (1.0 + jnp.tanh(math.sqrt(2.0 / math.pi) * (x + 0.044715 * (x * x * x))))


def _sigmoid(x):
    return 0.5 * jnp.tanh(0.5 * x) + 0.5


def _dot(a, b):
    return jnp.dot(a, b, preferred_element_type=F32)


def _dot_nt(a, b):
    return lax.dot_general(a, b, (((1,), (1,)), ((), ())), preferred_element_type=F32)


def _resident(shape):
    zeros = (0,) * len(shape)
    return pl.BlockSpec(shape, lambda i: zeros, pipeline_mode=pl.Buffered(1))


def _with_side_casts(kernel, n_in, n_out, n_cast):
    def wrapped(*refs):
        ins = refs[:n_in]
        cast_ins = refs[n_in:n_in + n_cast]
        outs = refs[n_in + n_cast:n_in + n_cast + n_out]
        cast_outs = refs[n_in + n_cast + n_out:n_in + 2 * n_cast + n_out]
        scratch = refs[n_in + 2 * n_cast + n_out:]
        for src, dst in zip(cast_ins, cast_outs):
            if dst.shape == src.shape:
                dst[...] = src[...].astype(BF16)
            else:
                _qkv_rows_to_bf16(src, dst)
        kernel(*ins, *outs, *scratch)

    return wrapped


def _qkv_rows_to_bf16(src_ref, dst_ref):
    dst_ref[:, :D_MODEL] = src_ref[:, :D_MODEL].astype(BF16)
    for j in range(2 * N_KV_HEADS):
        head = src_ref[:, D_MODEL + j * HEAD_DIM:D_MODEL + (j + 1) * HEAD_DIM]
        dst_ref[:, D_MODEL + j * LANES:D_MODEL + (j + 1) * LANES] = (
            jnp.concatenate([head, head], axis=1).astype(BF16))


def _call(kernel, name, grid, semantics, in_specs, args, out_specs, out_shape, scratch=(),
          side_casts=(), step_index=None):
    in_specs = list(in_specs)
    out_specs = list(out_specs)
    out_shape = list(out_shape)
    args = list(args)
    n_in, n_out, n_cast = len(in_specs), len(out_specs), len(side_casts)
    if n_cast:
        n_steps = math.prod(grid)
        if step_index is None:
            step_index = lambda i: i
        for w in side_casts:
            out_cols = QKV_COLS if w.shape[1] == QKV_WIDTH else w.shape[1]
            rows = w.shape[0] // n_steps
            assert rows % 16 == 0 and rows * n_steps == w.shape[0]
            in_specs.append(pl.BlockSpec((rows, w.shape[1]), lambda *g: (step_index(*g), 0)))
            out_specs.append(pl.BlockSpec((rows, out_cols), lambda *g: (step_index(*g), 0)))
            out_shape.append(jax.ShapeDtypeStruct((w.shape[0], out_cols), BF16))
            args.append(w.astype(F32))
        kernel = _with_side_casts(kernel, n_in, n_out, n_cast)
    res = pl.pallas_call(
        kernel,
        grid=grid,
        in_specs=in_specs,
        out_specs=out_specs,
        out_shape=out_shape,
        scratch_shapes=list(scratch),
        compiler_params=_params(*semantics),
        name=name,
    )(*args)
    return list(res[:n_out]), list(res[n_out:])


def _row_matmul_kernel(*refs, n_extra, n_out, has_norm, col_offsets, tn, n_chunks, epilogue):
    lhs_ref = refs[0]
    pos = 1
    if has_norm:
        gain_ref = refs[1]
        pos = 2
    w_ref = refs[pos]
    pos += 1
    extra_refs = refs[pos:pos + n_extra]
    pos += n_extra
    out_refs = refs[pos:pos + n_out]
    pos += n_out
    if has_norm:
        src = refs[pos]
        pos += 1
        src[...] = _rms(lhs_ref[...], gain_ref[...]).astype(BF16)
    else:
        src = lhs_ref
    scratch_refs = refs[pos:]
    for c in range(n_chunks):
        cols = slice(c * tn, (c + 1) * tn)
        a = src[...]
        a = a.reshape(-1, a.shape[-1])
        accs = [_dot(a, w_ref[:, off + c * tn:off + (c + 1) * tn]) for off in col_offsets]
        epilogue(c, cols, accs, lhs_ref, extra_refs, out_refs, *scratch_refs)


def _row_matmul(name, n_blocks, lhs, lhs_spec, tm, w, col_offsets, tn, n_chunks, extras, outs,
                epilogue, gain=None, scratch=(), side_casts=()):
    has_norm = gain is not None
    k_dim = w.shape[0]
    args = [lhs]
    in_specs = [lhs_spec]
    if has_norm:
        args.append(gain.reshape(1, k_dim).astype(F32))
        in_specs.append(_resident((1, k_dim)))
    args.append(w)
    in_specs.append(_resident(w.shape))
    for a, s in extras:
        args.append(a)
        in_specs.append(s)
    kern = functools.partial(_row_matmul_kernel, n_extra=len(extras), n_out=len(outs),
                             has_norm=has_norm, col_offsets=tuple(col_offsets), tn=tn,
                             n_chunks=n_chunks, epilogue=epilogue)
    scratch = ([pltpu.VMEM((tm, k_dim), BF16)] if has_norm else []) + list(scratch)
    return _call(kern, name, (n_blocks,), ("parallel",), in_specs, args, [s for _, s in outs],
                 [o for o, _ in outs], scratch, side_casts)


def _rows(tm, width):
    return pl.BlockSpec((tm, width), lambda i: (i, 0))


QKV_TN = 512
QKV_WIDTH = (N_HEADS + 2 * N_KV_HEADS) * HEAD_DIM
QKV_COLS = D_MODEL + 2 * N_KV_HEADS * LANES
QKV_NORM_BLOCKS = (D_MODEL + N_KV_HEADS * LANES) // QKV_TN
LOG2E = math.log2(math.e)


ATTN_GROUP = 256
ATTN_GROUPS_PER_STEP = 2
K_COL = D_MODEL
V_COL = D_MODEL + N_KV_HEADS * LANES


def _interleave(*task_lists):
    keyed = []
    for n_list, tasks in enumerate(task_lists):
        for k, task in enumerate(tasks):
            keyed.append(((k + 0.5) / len(tasks), n_list, k, task))
    keyed.sort(key=lambda item: item[:3])
    return [item[3] for item in keyed]


def _qkv_tasks(hn_ref, wp_ref, gain_ref, seg_ref, qkv_ref, rows):
    def chunk(c):
        cols = slice(c * QKV_TN, (c + 1) * QKV_TN)
        x = _dot(hn_ref[rows, :], wp_ref[:, cols])
        if c < QKV_NORM_BLOCKS:
            x2 = (x * x).astype(BF16)
            ss = jnp.concatenate(
                [_dot(x2[:, k * MXU_DIM:(k + 1) * MXU_DIM], seg_ref[...])
                 for k in range(QKV_TN // MXU_DIM)], axis=1)
            r = lax.rsqrt(ss * (1.0 / HEAD_DIM) + EPS)
            qkv_ref[rows, cols] = (x * r * gain_ref[:, cols]).astype(BF16)
        else:
            qkv_ref[rows, cols] = x.astype(BF16)

    return [functools.partial(chunk, c) for c in range(QKV_COLS // QKV_TN)]


def _alibi_slopes():
    return [float(2.0 ** (-8.0 * (i + 1) / N_HEADS)) for i in range(N_HEADS)]


def _attend_tasks(qkv_ref, rows, k_prev, v_prev, positions, sinks_ref, o_ref):
    shared = {}

    def prepare():
        qpos, kpos_prev, kpos_cur = positions()
        kpos = jnp.concatenate([kpos_prev, kpos_cur], axis=1)
        dist = qpos - kpos
        valid = (dist >= 0) & (dist < WINDOW)
        lane = lax.broadcasted_iota(jnp.int32, (BLOCK, LANES), 1)
        lo_f = (lane < HEAD_DIM).astype(F32)
        lo = lo_f.astype(BF16)
        hi = (1.0 - lo_f).astype(BF16)
        shared.update(distm=jnp.where(valid, dist.astype(F32), MASK_DIST), lane=lane, lo=lo, hi=hi,
                      lo2=jnp.concatenate([lo, lo], axis=0), hi2=jnp.concatenate([hi, hi], axis=0))

    slopes = _alibi_slopes()

    def head_task(h):
        if not shared:
            prepare()
        distm, lane, lo, hi = shared["distm"], shared["lane"], shared["lo"], shared["hi"]
        lo2, hi2 = shared["lo2"], shared["hi2"]
        kk = jnp.concatenate([k_prev(h), qkv_ref[rows, K_COL + h * LANES:K_COL + (h + 1) * LANES]],
                             axis=0)
        vd = jnp.concatenate([v_prev(h), qkv_ref[rows, V_COL + h * LANES:V_COL + (h + 1) * LANES]],
                             axis=0)
        qs = []
        for g in range(Q_PER_KV):
            col = h * Q_PER_KV * HEAD_DIM + (g // 2) * LANES
            qp = qkv_ref[rows, col:col + LANES]
            qs.append(qp * (lo if g % 2 == 0 else hi))
        s_all = _dot_nt(jnp.concatenate(qs, axis=0), kk)
        ps = []
        rls = []
        for g in range(Q_PER_KV):
            head = h * Q_PER_KV + g
            sg = s_all[g * BLOCK:(g + 1) * BLOCK] - (slopes[head] * LOG2E) * distm
            sink = sinks_ref[head] * LOG2E
            m = jnp.maximum(jnp.max(sg, axis=-1, keepdims=True), sink)
            p = jnp.exp2(sg - m)
            denom = jnp.sum(p, axis=-1, keepdims=True) + jnp.exp2(sink - m)
            ps.append(p.astype(BF16))
            rls.append(1.0 / denom)
        lhs2 = jnp.concatenate(
            [jnp.concatenate([ps[2 * p], ps[2 * p + 1]], axis=1) for p in range(Q_PER_KV // 2)],
            axis=0)
        rhs2 = jnp.concatenate([vd * lo2, vd * hi2], axis=0)
        o2 = _dot(lhs2, rhs2)
        for p in range(Q_PER_KV // 2):
            scale = jnp.where(lane < HEAD_DIM, rls[2 * p], rls[2 * p + 1])
            col = h * Q_PER_KV * HEAD_DIM + p * LANES
            o_ref[rows, col:col + LANES] = (o2[p * BLOCK:(p + 1) * BLOCK] * scale).astype(BF16)

    return [functools.partial(head_task, h) for h in range(N_KV_HEADS)]


def _attn_layer_kernel(h_ref, gain_ref, wp_ref, hg_ref, seg_ref, wo_ref, qpos_ref, kposp_ref,
                       kpos_ref, sinks_ref, out_ref, hn_ref, qkv_ref, o_ref, tail_ref):
    first = pl.program_id(0) == 0

    @pl.when(first)
    def _():
        tail_ref[...] = jnp.zeros(tail_ref.shape, BF16)

    hn_ref[...] = _rms(h_ref[...], gain_ref[...]).astype(BF16)
    kv_width = N_KV_HEADS * LANES
    step_rows = h_ref.shape[0]
    n_groups = step_rows // ATTN_GROUP
    bpg = ATTN_GROUP // BLOCK
    project, attend, emit = [], [], []
    for g in range(n_groups):
        grows = slice(g * ATTN_GROUP, (g + 1) * ATTN_GROUP)
        project.append(_qkv_tasks(hn_ref, wp_ref, hg_ref, seg_ref, qkv_ref, grows))

        def out_chunk(c, grows=grows):
            cols = slice(c * QKV_TN, (c + 1) * QKV_TN)
            out_ref[grows, cols] = h_ref[grows, cols] + _dot(o_ref[grows, :], wo_ref[:, cols])

        emit.append([functools.partial(out_chunk, c) for c in range(D_MODEL // QKV_TN)])
    for b in range(step_rows // BLOCK):
        rows = slice(b * BLOCK, (b + 1) * BLOCK)
        if b == 0:
            k_prev = lambda h: tail_ref[:, h * LANES:(h + 1) * LANES]
            v_prev = lambda h: tail_ref[:, kv_width + h * LANES:kv_width + (h + 1) * LANES]
            positions = lambda rows=rows: (qpos_ref[rows, :],
                                           jnp.where(first, PAD_POS, kposp_ref[...]), kpos_ref[0])
        else:
            prows = slice((b - 1) * BLOCK, b * BLOCK)
            k_prev = lambda h, prows=prows: qkv_ref[prows, K_COL + h * LANES:K_COL + (h + 1) * LANES]
            v_prev = lambda h, prows=prows: qkv_ref[prows, V_COL + h * LANES:V_COL + (h + 1) * LANES]
            positions = lambda rows=rows, b=b: (qpos_ref[rows, :], kpos_ref[b - 1], kpos_ref[b])
        attend.append(_attend_tasks(qkv_ref, rows, k_prev, v_prev, positions, sinks_ref, o_ref))

    for task in project[0]:
        task()
    for g in range(n_groups):
        softmax = [t for b in range(g * bpg, (g + 1) * bpg) for t in attend[b]]
        nxt = project[g + 1] if g + 1 < n_groups else []
        prv = emit[g - 1] if g > 0 else []
        for task in _interleave(softmax, nxt, prv):
            task()
    last = slice(step_rows - BLOCK, step_rows)
    tail_ref[...] = qkv_ref[last, K_COL:K_COL + 2 * kv_width]
    for task in emit[n_groups - 1]:
        task()


def _prep_qkv_weight(w_qkv):
    if w_qkv.shape[1] == QKV_COLS:
        return w_qkv
    w16 = w_qkv.astype(BF16)
    wq = w16[:, :D_MODEL]
    wk = w16[:, D_MODEL:D_MODEL + N_KV_HEADS * HEAD_DIM].reshape(D_MODEL, N_KV_HEADS, HEAD_DIM)
    wv = w16[:, D_MODEL + N_KV_HEADS * HEAD_DIM:].reshape(D_MODEL, N_KV_HEADS, HEAD_DIM)
    dup = lambda w: jnp.concatenate([w, w], axis=-1).reshape(D_MODEL, N_KV_HEADS * LANES)
    return jnp.concatenate([wq, dup(wk), dup(wv)], axis=1)


def _attention_layer(h, positions, mix_norm, w_qkv, q_norm, k_norm, sinks, w_o, side_casts=()):
    head_gain = jnp.concatenate([
        jnp.tile(q_norm.astype(F32), N_HEADS) * (HEAD_DIM ** -0.5 * LOG2E),
        jnp.tile(k_norm.astype(F32), 2 * N_KV_HEADS),
        jnp.ones((N_KV_HEADS * LANES,), F32)]).reshape(1, QKV_COLS)
    seg_ones = jnp.kron(jnp.eye(MXU_DIM // HEAD_DIM, dtype=F32),
                        jnp.ones((HEAD_DIM, HEAD_DIM), F32)).astype(BF16)
    t = h.shape[0]
    rows = ATTN_GROUP * (1 if side_casts else ATTN_GROUPS_PER_STEP)
    n_steps = t // rows
    bps = rows // BLOCK
    pos_row = positions.reshape(t // BLOCK, 1, BLOCK)
    in_specs = [
        _rows(rows, D_MODEL),
        _resident((1, D_MODEL)),
        _resident((D_MODEL, QKV_COLS)),
        _resident((1, QKV_COLS)),
        _resident((MXU_DIM, MXU_DIM)),
        _resident((D_MODEL, D_MODEL)),
        pl.BlockSpec((rows, 1), lambda i: (i, 0)),
        pl.BlockSpec((None, 1, BLOCK), lambda i: (jnp.maximum(bps * i - 1, 0), 0, 0)),
        pl.BlockSpec((bps, 1, BLOCK), lambda i: (i, 0, 0)),
        pl.BlockSpec(memory_space=pltpu.SMEM),
    ]
    args = [h, mix_norm.reshape(1, D_MODEL).astype(F32), _prep_qkv_weight(w_qkv), head_gain, seg_ones,
            w_o.astype(BF16), positions.reshape(t, 1), pos_row, pos_row, sinks.astype(F32)]
    scratch = [pltpu.VMEM((rows, D_MODEL), BF16), pltpu.VMEM((rows, QKV_COLS), BF16),
               pltpu.VMEM((rows, D_MODEL), BF16), pltpu.VMEM((BLOCK, 2 * N_KV_HEADS * LANES), BF16)]
    (out,), casts = _call(_attn_layer_kernel, "attn_layer", (n_steps,), ("arbitrary",), in_specs, args,
                          [_rows(rows, D_MODEL)], [jax.ShapeDtypeStruct((t, D_MODEL), F32)], scratch,
                          side_casts)
    return out, casts


MLP_TF = 512


def _mlp_kernel(h_ref, gain_ref, wu_ref, wd_ref, o_ref, hn_ref):
    j = pl.program_id(1)

    @pl.when(j == 0)
    def _():
        x = h_ref[...]
        hn_ref[...] = _rms(x, gain_ref[...]).astype(BF16)
        o_ref[...] = x

    act = jnp.maximum(_dot(hn_ref[...], wu_ref[...]), 0.0)
    act = (act * act).astype(BF16)
    o_ref[...] += _dot(act, wd_ref[...])


def _mlp(h, gain, w_up16, w_down16, side_casts=()):
    t = h.shape[0]
    tm = min(1024, t)
    tf = MLP_TF
    nj = D_FF // tf
    in_specs = [
        pl.BlockSpec((tm, D_MODEL), lambda i, j: (i, 0)),
        pl.BlockSpec((1, D_MODEL), lambda i, j: (0, 0)),
        pl.BlockSpec((D_MODEL, tf), lambda i, j: (0, j)),
        pl.BlockSpec((tf, D_MODEL), lambda i, j: (j, 0)),
    ]
    args = [h, gain.reshape(1, D_MODEL).astype(F32), w_up16, w_down16]
    (out,), casts = _call(_mlp_kernel, "mlp", (t // tm, nj), ("parallel", "arbitrary"), in_specs, args,
                          [pl.BlockSpec((tm, D_MODEL), lambda i, j: (i, 0))],
                          [jax.ShapeDtypeStruct((t, D_MODEL), F32)],
                          [pltpu.VMEM((tm, D_MODEL), BF16)], side_casts,
                          step_index=lambda i, j: i * nj + j)
    return out, casts


def _ple_epilogue(c, cols, accs, lhs_ref, extras, outs):
    p_ref, wple_ref = extras
    emb = _dot(p_ref[...].astype(BF16), wple_ref[:, cols])
    outs[0][:, cols] = lhs_ref[:, cols] + emb * _sigmoid(accs[0])


def _ple(h, gain, p_all, layer, w_ple, w_gate):
    t = h.shape[0]
    tm = min(1024, t)
    first = layer * (t // tm)
    (out,), _ = _row_matmul(
        "ple", t // tm, h, _rows(tm, D_MODEL), tm, w_gate, (0,), 512, D_MODEL // 512,
        extras=[(p_all, pl.BlockSpec((tm, PLE_DIM), lambda i: (first + i, 0))),
                (w_ple, _resident((PLE_DIM, D_MODEL)))],
        outs=[(jax.ShapeDtypeStruct((t, D_MODEL), F32), _rows(tm, D_MODEL))],
        epilogue=_ple_epilogue, gain=gain)
    return out


RGLRU_TB = 256
RGLRU_SEG = RGLRU_TB // SUBLANES
RGLRU_PITCH = 36
TINY = 1e-36


def _rglru_layer_kernel(h_ref, gain_ref, win_ref, cw_ref, cb_ref, wa_ref, ba_ref, wx_ref, bx_ref,
                        lam_ref, wo_ref, out_ref, hn_ref, xb_ref, gate_ref, y_ref, a_ref, b_ref,
                        carry_ref, tail_ref):
    tb = RGLRU_TB
    seg = RGLRU_SEG
    n_slabs = D_MODEL // LANES
    spb = GATE_BLOCK // LANES

    @pl.when(pl.program_id(0) == 0)
    def _():
        carry_ref[...] = jnp.zeros((1, D_MODEL), F32)
        tail_ref[...] = jnp.zeros((CONV_WIDTH - 1, D_MODEL), F32)

    hn_ref[...] = _rms(h_ref[...], gain_ref[...]).astype(BF16)
    lam = lam_ref[...]
    rate = (-LRU_C) * (jnp.log(1.0 + jnp.exp(-jnp.abs(lam))) + jnp.maximum(-lam, 0.0))

    def project(n):
        sl = slice(n * GATE_BLOCK, (n + 1) * GATE_BLOCK)
        xb_ref[:, sl] = _dot(hn_ref[...], win_ref[:, sl])
        gate_ref[:, sl] = _gelu_tanh(
            _dot(hn_ref[...], win_ref[:, D_MODEL + n * GATE_BLOCK:D_MODEL + (n + 1) * GATE_BLOCK])
        ).astype(BF16)

    def gates(n):
        sl = slice(n * GATE_BLOCK, (n + 1) * GATE_BLOCK)

        def shift1(v, k):
            prev_last = tail_ref[k:k + 1, sl]
            tail_ref[k:k + 1, sl] = v[tb - 1:tb, :]
            return jnp.concatenate([prev_last, v[:tb - 1, :]], axis=0)

        x = xb_ref[:, sl]
        acc = cw_ref[0:1, sl] * x
        for k in range(1, CONV_WIDTH):
            acc = cw_ref[k:k + 1, sl] * x + shift1(acc, k - 1)
        xn = acc + cb_ref[:, sl]
        xn16 = xn.astype(BF16)
        r = _sigmoid(_dot(xn16, wa_ref[n]) + ba_ref[:, sl])
        i_gate = _sigmoid(_dot(xn16, wx_ref[n]) + bx_ref[:, sl])
        a = jnp.exp(rate[:, sl] * r)
        v = 1.0 - a * a
        b = (v * lax.rsqrt(jnp.maximum(v, TINY))) * (i_gate * xn)
        for kk in range(spb):
            slab = n * spb + kk
            for s in range(SUBLANES):
                rows = slice(s * RGLRU_PITCH, s * RGLRU_PITCH + seg)
                a_ref[slab, rows, :] = a[s * seg:(s + 1) * seg, kk * LANES:(kk + 1) * LANES]
                b_ref[slab, rows, :] = b[s * seg:(s + 1) * seg, kk * LANES:(kk + 1) * LANES]

    project(0)
    for n in range(N_GATE_BLOCKS):
        if n + 1 < N_GATE_BLOCKS:
            project(n + 1)
        gates(n)

    def step(q, carry):
        hs, prods = carry
        rows = pl.ds(q, SUBLANES, stride=RGLRU_PITCH)
        new_hs = []
        new_prods = []
        for k in range(n_slabs):
            av = a_ref[k, rows, :]
            hk = av * hs[k] + b_ref[k, rows, :]
            pk = av * prods[k]
            b_ref[k, rows, :] = hk
            a_ref[k, rows, :] = pk
            new_hs.append(hk)
            new_prods.append(pk)
        return tuple(new_hs), tuple(new_prods)

    zeros = tuple(jnp.zeros((SUBLANES, LANES), F32) for _ in range(n_slabs))
    ones = tuple(jnp.ones((SUBLANES, LANES), F32) for _ in range(n_slabs))
    hs, prods = lax.fori_loop(0, seg, step, (zeros, ones))

    for k in range(n_slabs):
        lanes = slice(k * LANES, (k + 1) * LANES)
        cur = carry_ref[:, lanes]
        for s in range(SUBLANES):
            rows = slice(s * RGLRU_PITCH, s * RGLRU_PITCH + seg)
            hfull = b_ref[k, rows, :] + a_ref[k, rows, :] * cur
            out_rows = slice(s * seg, (s + 1) * seg)
            y_ref[out_rows, lanes] = (hfull * gate_ref[out_rows, lanes].astype(F32)).astype(BF16)
            cur = prods[k][s:s + 1, :] * cur + hs[k][s:s + 1, :]
        carry_ref[:, lanes] = cur

    for c in range(D_MODEL // 512):
        cols = slice(c * 512, (c + 1) * 512)
        out_ref[:, cols] = h_ref[:, cols] + _dot(y_ref[...], wo_ref[:, cols])


def _rglru_layer(h, mix_norm, w_in, conv_w, conv_b, w_gate_a, b_gate_a, w_gate_x, b_gate_x,
                 lru_lambda, w_o, side_casts=()):
    t = h.shape[0]
    tb = RGLRU_TB
    row = lambda v: v.reshape(1, D_MODEL).astype(F32)
    scan_scratch = pltpu.VMEM((D_MODEL // LANES, SUBLANES * RGLRU_PITCH, LANES), F32)
    in_specs = [
        _rows(tb, D_MODEL), _resident((1, D_MODEL)), _resident((D_MODEL, 2 * D_MODEL)),
        _resident((CONV_WIDTH, D_MODEL)), _resident((1, D_MODEL)),
        _resident((N_GATE_BLOCKS, GATE_BLOCK, GATE_BLOCK)), _resident((1, D_MODEL)),
        _resident((N_GATE_BLOCKS, GATE_BLOCK, GATE_BLOCK)), _resident((1, D_MODEL)),
        _resident((1, D_MODEL)), _resident((D_MODEL, D_MODEL)),
    ]
    args = [h, row(mix_norm), w_in.astype(BF16), conv_w.astype(F32), row(conv_b),
            w_gate_a.astype(BF16), row(b_gate_a), w_gate_x.astype(BF16), row(b_gate_x),
            row(lru_lambda), w_o.astype(BF16)]
    scratch = [pltpu.VMEM((tb, D_MODEL), BF16), pltpu.VMEM((tb, D_MODEL), F32),
               pltpu.VMEM((tb, D_MODEL), BF16), pltpu.VMEM((tb, D_MODEL), BF16),
               scan_scratch, scan_scratch,
               pltpu.VMEM((1, D_MODEL), F32), pltpu.VMEM((CONV_WIDTH - 1, D_MODEL), F32)]
    (out,), casts = _call(_rglru_layer_kernel, "rglru_layer", (t // tb,), ("arbitrary",), in_specs,
                          args, [_rows(tb, D_MODEL)], [jax.ShapeDtypeStruct((t, D_MODEL), F32)],
                          scratch, side_casts)
    return out, casts


S5_TM = 512
S5_IN_PITCH = 20
S5_GLU_PITCH = 36


def _s5_in_epilogue(c, cols, accs, lhs_ref, extras, outs, acc_ref):
    n = accs[0].shape[0] // S5_CHUNK
    for k in range(acc_ref.shape[0]):
        for ch in range(n):
            acc_ref[k, ch * S5_IN_PITCH:ch * S5_IN_PITCH + S5_CHUNK, :] = (
                accs[0][ch * S5_CHUNK:(ch + 1) * S5_CHUNK, k * LANES:(k + 1) * LANES])
        for l in range(S5_CHUNK):
            outs[0][l, :, cols.start + k * LANES:cols.start + (k + 1) * LANES] = (
                acc_ref[k, pl.ds(l, n, stride=S5_IN_PITCH), :])


def _s5_in(h, mix_norm, w_in, side_casts=()):
    t = h.shape[0]
    nc = t // S5_CHUNK
    tm = min(S5_TM, t)
    tn = 512
    (u,), casts = _row_matmul(
        "s5_in", t // tm, h, _rows(tm, D_MODEL), tm, w_in, (0,), tn, D_MODEL // tn,
        extras=[],
        outs=[(jax.ShapeDtypeStruct((S5_CHUNK, nc, D_MODEL), F32),
               pl.BlockSpec((S5_CHUNK, tm // S5_CHUNK, D_MODEL), lambda i: (0, i, 0)))],
        epilogue=_s5_in_epilogue, gain=mix_norm,
        scratch=[pltpu.VMEM((tn // LANES, (tm // S5_CHUNK) * S5_IN_PITCH, LANES), F32)],
        side_casts=side_casts)
    return u, casts


def _s5_ssm_kernel(u_ref, lr_ref, li_ref, ldt_ref, bre_ref, bim_ref, cre_ref, cim_ref, dsk_ref,
                   g_ref, tbig_ref, wend_ref, wc_ref, x_ref, e_ref, s_ref, y_ref):
    nc = u_ref.shape[1]
    L = S5_CHUNK
    P = S5_STATE_COLS
    lr = lr_ref[...]
    li = li_ref[...]
    dt = jnp.exp(ldt_ref[...])
    mag = jnp.exp(lr * dt)
    abar_re = mag * jnp.cos(li * dt)
    abar_im = mag * jnp.sin(li * dt)
    n_re = abar_re - 1.0
    n_im = abar_im
    den = lr * lr + li * li
    z_re = (n_re * lr + n_im * li) / den
    z_im = (n_im * lr - n_re * li) / den
    bre = bre_ref[...]
    bim = bim_ref[...]
    bbar_re = z_re * bre - z_im * bim
    bbar_im = z_re * bim + z_im * bre
    cre = cre_ref[...]
    cim = cim_ref[...]

    def power(d):
        m = jnp.exp((d * dt) * lr)
        ang = (d * dt) * li
        return m * jnp.cos(ang), m * jnp.sin(ang)

    def w_in_state(d):
        pr, pi = power(d)
        return jnp.concatenate([pr * bbar_re - pi * bbar_im, pr * bbar_im + pi * bbar_re],
                               axis=1).astype(BF16)

    def w_out_state(d):
        pr, pi = power(d)
        return jnp.concatenate([cre * pr - cim * pi, -(cre * pi + cim * pr)], axis=1).astype(BF16)

    wt0 = w_out_state(0)
    for d in range(L):
        wj = w_in_state(d)
        m_d = _dot_nt(wj, wt0).astype(BF16)
        for jj in range(L - d):
            tt = jj + d
            tbig_ref[jj * LANES:(jj + 1) * LANES, tt * LANES:(tt + 1) * LANES] = m_d
        wend_ref[(L - 1 - d) * LANES:(L - d) * LANES, :] = wj
        wc_ref[d * LANES:(d + 1) * LANES, :] = w_out_state(d + 1)
    for tt in range(0, L, 2):
        tbig_ref[(tt + 1) * LANES:(tt + 2) * LANES, tt * LANES:(tt + 1) * LANES] = (
            jnp.zeros((LANES, LANES), BF16))

    x_ref[...] = jnp.concatenate([u_ref[l] for l in range(L)], axis=1).astype(BF16)
    e_ref[...] = _dot(x_ref[...], wend_ref[...])

    row = lax.broadcasted_iota(jnp.int32, (SUBLANES, P), 0)
    steps = []
    for s in (1, 2, 4):
        pr, pi = power(L * s)
        steps.append((s, row >= s, pr, pi))
    carry_pw = [power(L * (r + 1)) for r in range(SUBLANES)]
    cpw_re = jnp.concatenate([c[0] for c in carry_pw], axis=0)
    cpw_im = jnp.concatenate([c[1] for c in carry_pw], axis=0)

    s_ref[0:SUBLANES, :] = jnp.zeros((SUBLANES, 2 * P), F32)

    def tile_step(i, carry):
        s_re, s_im = carry
        base = i * SUBLANES
        e = e_ref[base:base + SUBLANES, :]
        t_re = e[:, :P]
        t_im = e[:, P:]
        for s, keep, pr, pi in steps:
            sh_re = jnp.where(keep, pltpu.roll(t_re, s, 0), 0.0)
            sh_im = jnp.where(keep, pltpu.roll(t_im, s, 0), 0.0)
            t_re, t_im = t_re + pr * sh_re - pi * sh_im, t_im + pr * sh_im + pi * sh_re
        f_re = t_re + cpw_re * s_re - cpw_im * s_im
        f_im = t_im + cpw_re * s_im + cpw_im * s_re
        s_ref[base + SUBLANES:base + 2 * SUBLANES, :] = jnp.concatenate([f_re, f_im], axis=1)
        return f_re[SUBLANES - 1:SUBLANES, :], f_im[SUBLANES - 1:SUBLANES, :]

    n_tiles = nc // SUBLANES
    tiles_per_block = n_tiles // (L // 2)
    carry = (jnp.zeros((1, P), F32), jnp.zeros((1, P), F32))
    for b in range(L // 2):
        k_rows = (2 * b + 2) * LANES
        cols = slice(2 * b * LANES, (2 * b + 2) * LANES)
        y_ref[:, cols] = _dot(x_ref[:, :k_rows], tbig_ref[:k_rows, cols])
        for i in range(b * tiles_per_block, (b + 1) * tiles_per_block):
            carry = tile_step(i, carry)

    s_start = s_ref[SUBLANES - 1:SUBLANES - 1 + nc, :].astype(BF16)
    dsk = dsk_ref[...]
    for b in range(L // 2):
        cols = slice(2 * b * LANES, (2 * b + 2) * LANES)
        y = y_ref[:, cols] + _dot_nt(s_start, wc_ref[cols, :])
        for l in (2 * b, 2 * b + 1):
            yl = y[:, (l - 2 * b) * LANES:(l - 2 * b + 1) * LANES] + dsk * u_ref[l]
            g_ref[l] = _gelu_tanh(yl).astype(BF16)


def _s5_ssm(u, a_re, a_im, log_dt, b_re, b_im, c_re, c_im, d_skip, side_casts=()):
    L, nc, _ = u.shape
    P = S5_STATE_COLS
    gpt = S5_GROUPS_PER_TILE
    nt = N_LANE_TILES
    f = lambda v: v.astype(F32)
    row = lambda v: f(v).reshape(nt, 1, P)
    ldt = jnp.broadcast_to(f(log_dt)[:, None], (N_SSM_GROUPS, SSM_STATE))
    same_group = (jnp.arange(LANES)[:, None] // SSM_GROUP) == (jnp.arange(P)[None, :] // SSM_STATE)

    def block_diag(x):
        return jnp.where(same_group, jnp.tile(x, (1, 1, gpt)), 0.0)

    def b_layout(b):
        bt = f(b).reshape(nt, gpt, SSM_STATE, SSM_GROUP).transpose(0, 1, 3, 2)
        return block_diag(bt.reshape(nt, LANES, SSM_STATE))

    def c_layout(c):
        return block_diag(f(c).reshape(nt, LANES, SSM_STATE))

    tile_row = pl.BlockSpec((None, 1, P), lambda k: (k, 0, 0))
    tile_mat = pl.BlockSpec((None, LANES, P), lambda k: (k, 0, 0))
    in_specs = [pl.BlockSpec((L, nc, LANES), lambda k: (0, 0, k)),
                tile_row, tile_row, tile_row, tile_mat, tile_mat, tile_mat, tile_mat,
                pl.BlockSpec((1, LANES), lambda k: (0, k))]
    args = [u, row(a_re), row(a_im), row(ldt), b_layout(b_re), b_layout(b_im), c_layout(c_re),
            c_layout(c_im), f(d_skip).reshape(1, D_MODEL)]
    scratch = [pltpu.VMEM((L * LANES, L * LANES), BF16),
               pltpu.VMEM((L * LANES, 2 * P), BF16),
               pltpu.VMEM((L * LANES, 2 * P), BF16),
               pltpu.VMEM((nc, L * LANES), BF16),
               pltpu.VMEM((nc, 2 * P), F32),
               pltpu.VMEM((nc + SUBLANES, 2 * P), F32),
               pltpu.VMEM((nc, L * LANES), F32)]
    (g,), casts = _call(_s5_ssm_kernel, "s5_ssm", (nt,), ("arbitrary",), in_specs, args,
                        [pl.BlockSpec((L, nc, LANES), lambda k: (0, 0, k))],
                        [jax.ShapeDtypeStruct((L, nc, D_MODEL), BF16)], scratch, side_casts)
    return g, casts


def _glu_epilogue(c, cols, accs, lhs_ref, extras, outs, y_ref):
    y = accs[0] * _sigmoid(accs[1])
    n = y.shape[0] // S5_CHUNK
    assert n <= S5_GLU_PITCH
    for k in range(y_ref.shape[0]):
        for l in range(S5_CHUNK):
            y_ref[k, l * S5_GLU_PITCH:l * S5_GLU_PITCH + n, :] = (
                y[l * n:(l + 1) * n, k * LANES:(k + 1) * LANES])
        lanes = slice(cols.start + k * LANES, cols.start + (k + 1) * LANES)
        for ch in range(n):
            rows = slice(ch * S5_CHUNK, (ch + 1) * S5_CHUNK)
            outs[0][rows, lanes] = (extras[0][rows, lanes]
                                    + y_ref[k, pl.ds(ch, S5_CHUNK, stride=S5_GLU_PITCH), :])


def _s5_glu(g, w_glu, h, side_casts=()):
    L, nc, _ = g.shape
    t = nc * L
    tm = min(S5_TM, t)
    (out,), casts = _row_matmul(
        "s5_glu", t // tm, g, pl.BlockSpec((L, tm // L, D_MODEL), lambda i: (0, i, 0)), tm, w_glu,
        (0, D_MODEL), 512, D_MODEL // 512,
        extras=[(h, _rows(tm, D_MODEL))],
        outs=[(jax.ShapeDtypeStruct((t, D_MODEL), F32), _rows(tm, D_MODEL))],
        epilogue=_glu_epilogue,
        scratch=[pltpu.VMEM((512 // LANES, S5_CHUNK * S5_GLU_PITCH, LANES), F32)],
        side_casts=side_casts)
    return out, casts


def _s5_layer(h, mix_norm, w_in, a_re, a_im, log_dt, b_re, b_im, c_re, c_im, d_skip, w_glu,
              side_casts=((), (), ())):
    u, casts_in = _s5_in(h, mix_norm, w_in.astype(BF16), side_casts[0])
    g, casts_ssm = _s5_ssm(u, a_re, a_im, log_dt, b_re, b_im, c_re, c_im, d_skip, side_casts[1])
    out, casts_glu = _s5_glu(g, w_glu.astype(BF16), h, side_casts[2])
    return out, (casts_in, casts_ssm, casts_glu)


def kernel(x, p, positions,
           l0_mix_norm, l0_w_qkv, l0_q_norm, l0_k_norm, l0_sinks, l0_w_o,
           l0_mlp_norm, l0_w_up, l0_w_down, l0_ple_norm, l0_w_ple_gate, l0_w_ple,
           l1_mix_norm, l1_w_in, l1_conv_w, l1_conv_b, l1_w_gate_a, l1_b_gate_a, l1_w_gate_x,
           l1_b_gate_x, l1_lru_lambda, l1_w_o,
           l1_mlp_norm, l1_w_up, l1_w_down, l1_ple_norm, l1_w_ple_gate, l1_w_ple,
           l2_mix_norm, l2_w_in, l2_a_re, l2_a_im, l2_log_dt, l2_b_re, l2_b_im, l2_c_re, l2_c_im,
           l2_d_skip, l2_w_glu,
           l2_mlp_norm, l2_w_up, l2_w_down, l2_ple_norm, l2_w_ple_gate, l2_w_ple,
           l3_mix_norm, l3_w_qkv, l3_q_norm, l3_k_norm, l3_sinks, l3_w_o,
           l3_mlp_norm, l3_w_up, l3_w_down, l3_ple_norm, l3_w_ple_gate, l3_w_ple):
    batch, t, _ = x.shape
    assert batch == 1 and t % (8 * BLOCK) == 0
    pos = positions.reshape(t).astype(jnp.int32)
    tails = [
        (l0_mlp_norm, l0_w_up, l0_w_down, l0_ple_norm, l0_w_ple_gate, l0_w_ple),
        (l1_mlp_norm, l1_w_up, l1_w_down, l1_ple_norm, l1_w_ple_gate, l1_w_ple),
        (l2_mlp_norm, l2_w_up, l2_w_down, l2_ple_norm, l2_w_ple_gate, l2_w_ple),
        (l3_mlp_norm, l3_w_up, l3_w_down, l3_ple_norm, l3_w_ple_gate, l3_w_ple),
    ]
    h = x.reshape(t, D_MODEL).astype(F32)
    p_all = p.reshape(DEPTH * t, PLE_DIM)
    for i in range(DEPTH):
        if i == 0:
            h, (up16, down16, gate16, l1_w_in16, l1_w_o16) = _attention_layer(
                h, pos, l0_mix_norm, l0_w_qkv, l0_q_norm, l0_k_norm, l0_sinks, l0_w_o,
                side_casts=(l0_w_up, l0_w_down, l0_w_ple_gate, l1_w_in, l1_w_o))
        elif i == 1:
            h, (gate16, l2_w_in16, l2_w_glu16, l2_gate16) = _rglru_layer(
                h, l1_mix_norm, l1_w_in16, l1_conv_w, l1_conv_b, l1_w_gate_a, l1_b_gate_a,
                l1_w_gate_x, l1_b_gate_x, l1_lru_lambda, l1_w_o16,
                side_casts=(l1_w_ple_gate, l2_w_in, l2_w_glu, l2_w_ple_gate))
        elif i == 2:
            gate16 = l2_gate16
            h, ((up16, down16), l3_mlp16, (l3_w_qkv16, l3_w_o16, l3_gate16)) = _s5_layer(
                h, l2_mix_norm, l2_w_in16, l2_a_re, l2_a_im, l2_log_dt, l2_b_re, l2_b_im,
                l2_c_re, l2_c_im, l2_d_skip, l2_w_glu16,
                side_casts=((l2_w_up, l2_w_down), (l3_w_up, l3_w_down),
                            (l3_w_qkv, l3_w_o, l3_w_ple_gate)))
        else:
            gate16 = l3_gate16
            up16, down16 = l3_mlp16
            h, _ = _attention_layer(h, pos, l3_mix_norm, l3_w_qkv16, l3_q_norm, l3_k_norm, l3_sinks,
                                    l3_w_o16)
        mlp_norm, _, _, ple_norm, _, w_ple = tails[i]
        h, next16 = _mlp(h, mlp_norm, up16, down16, side_casts=(l1_w_up, l1_w_down) if i == 0 else ())
        if next16:
            up16, down16 = next16
        h = _ple(h, ple_norm, p_all, i, w_ple.astype(BF16), gate16)
    return h.reshape(batch, t, D_MODEL).astype(x.dtype)
```

```python
import functools
import math

import jax
import jax.numpy as jnp
from jax import lax
from jax.experimental import pallas as pl
from jax.experimental.pallas import tpu as pltpu

F32 = jnp.float32
BF16 = jnp.bfloat16

D_MODEL = 2048
DEPTH = 4
N_HEADS = 32
N_KV_HEADS = 4
HEAD_DIM = 64
Q_PER_KV = N_HEADS // N_KV_HEADS
WINDOW = 128
BLOCK = 128
PAD_POS = -(1 << 20)
N_GATE_BLOCKS = 8
GATE_BLOCK = D_MODEL // N_GATE_BLOCKS
CONV_WIDTH = 4
LRU_C = 8.0
SSM_GROUP = 16
N_SSM_GROUPS = D_MODEL // SSM_GROUP
SSM_STATE = 64
D_FF = 4 * D_MODEL
PLE_DIM = 256
EPS = 1e-6

LANES = 128
SUBLANES = 8
MXU_DIM = 256
VMEM_LIMIT_BYTES = 60 * 1024 * 1024
MASK_DIST = 1e33

ROW_BLOCK = 1024
COL_CHUNK = 512

S5_CHUNK = 16
S5_GROUPS_PER_TILE = LANES // SSM_GROUP
S5_STATE_COLS = S5_GROUPS_PER_TILE * SSM_STATE
N_LANE_TILES = D_MODEL // LANES


def _params(*semantics):
    return pltpu.CompilerParams(dimension_semantics=semantics,
                                vmem_limit_bytes=VMEM_LIMIT_BYTES)


def _rms(x, gain):
    ms = jnp.mean(x * x, axis=-1, keepdims=True)
    return x * lax.rsqrt(ms + EPS) * gain


def _gelu_tanh(x):
    return 0.5 * x * (1.0 + jnp.tanh(math.sqrt(2.0 / math.pi) * (x + 0.044715 * (x * x * x))))


def _sigmoid(x):
    return 0.5 * jnp.tanh(0.5 * x) + 0.5


def _dot(a, b):
    return jnp.dot(a, b, preferred_element_type=F32)


def _dot_nt(a, b):
    return lax.dot_general(a, b, (((1,), (1,)), ((), ())), preferred_element_type=F32)


def _resident(shape):
    zeros = (0,) * len(shape)
    return pl.BlockSpec(shape, lambda i: zeros, pipeline_mode=pl.Buffered(1))


def _with_side_casts(kernel, n_in, n_out, n_cast):
    def wrapped(*refs):
        ins = refs[:n_in]
        cast_ins = refs[n_in:n_in + n_cast]
        outs = refs[n_in + n_cast:n_in + n_cast + n_out]
        cast_outs = refs[n_in + n_cast + n_out:n_in + 2 * n_cast + n_out]
        scratch = refs[n_in + 2 * n_cast + n_out:]
        for src, dst in zip(cast_ins, cast_outs):
            if dst.shape == src.shape:
                dst[...] = src[...].astype(BF16)
            else:
                _qkv_rows_to_bf16(src, dst)
        kernel(*ins, *outs, *scratch)

    return wrapped


def _qkv_rows_to_bf16(src_ref, dst_ref):
    dst_ref[:, :D_MODEL] = src_ref[:, :D_MODEL].astype(BF16)
    for j in range(2 * N_KV_HEADS):
        head = src_ref[:, D_MODEL + j * HEAD_DIM:D_MODEL + (j + 1) * HEAD_DIM]
        dst_ref[:, D_MODEL + j * LANES:D_MODEL + (j + 1) * LANES] = (
            jnp.concatenate([head, head], axis=1).astype(BF16))


def _call(kernel, name, grid, semantics, in_specs, args, out_specs, out_shape, scratch=(),
          side_casts=(), step_index=None):
    in_specs = list(in_specs)
    out_specs = list(out_specs)
    out_shape = list(out_shape)
    args = list(args)
    n_in, n_out, n_cast = len(in_specs), len(out_specs), len(side_casts)
    if n_cast:
        n_steps = math.prod(grid)
        if step_index is None:
            step_index = lambda i: i
        for w in side_casts:
            out_cols = QKV_COLS if w.shape[1] == QKV_WIDTH else w.shape[1]
            rows = w.shape[0] // n_steps
            assert rows % 16 == 0 and rows * n_steps == w.shape[0]
            in_specs.append(pl.BlockSpec((rows, w.shape[1]), lambda *g: (step_index(*g), 0)))
            out_specs.append(pl.BlockSpec((rows, out_cols), lambda *g: (step_index(*g), 0)))
            out_shape.append(jax.ShapeDtypeStruct((w.shape[0], out_cols), BF16))
            args.append(w.astype(F32))
        kernel = _with_side_casts(kernel, n_in, n_out, n_cast)
    res = pl.pallas_call(
        kernel,
        grid=grid,
        in_specs=in_specs,
        out_specs=out_specs,
        out_shape=out_shape,
        scratch_shapes=list(scratch),
        compiler_params=_params(*semantics),
        name=name,
    )(*args)
    return list(res[:n_out]), list(res[n_out:])


def _row_matmul_kernel(*refs, n_extra, n_out, has_norm, col_offsets, tn, n_chunks, epilogue):
    lhs_ref = refs[0]
    pos = 1
    if has_norm:
        gain_ref = refs[1]
        pos = 2
    w_ref = refs[pos]
    pos += 1
    extra_refs = refs[pos:pos + n_extra]
    pos += n_extra
    out_refs = refs[pos:pos + n_out]
    pos += n_out
    if has_norm:
        src = refs[pos]
        pos += 1
        src[...] = _rms(lhs_ref[...], gain_ref[...]).astype(BF16)
    else:
        src = lhs_ref
    scratch_refs = refs[pos:]
    for c in range(n_chunks):
        cols = slice(c * tn, (c + 1) * tn)
        a = src[...]
        a = a.reshape(-1, a.shape[-1])
        accs = [_dot(a, w_ref[:, off + c * tn:off + (c + 1) * tn]) for off in col_offsets]
        epilogue(c, cols, accs, lhs_ref, extra_refs, out_refs, *scratch_refs)


def _row_matmul(name, n_blocks, lhs, lhs_spec, tm, w, col_offsets, tn, n_chunks, extras, outs,
                epilogue, gain=None, scratch=(), side_casts=()):
    has_norm = gain is not None
    k_dim = w.shape[0]
    args = [lhs]
    in_specs = [lhs_spec]
    if has_norm:
        args.append(gain.reshape(1, k_dim).astype(F32))
        in_specs.append(_resident((1, k_dim)))
    args.append(w)
    in_specs.append(_resident(w.shape))
    for a, s in extras:
        args.append(a)
        in_specs.append(s)
    kern = functools.partial(_row_matmul_kernel, n_extra=len(extras), n_out=len(outs),
                             has_norm=has_norm, col_offsets=tuple(col_offsets), tn=tn,
                             n_chunks=n_chunks, epilogue=epilogue)
    scratch = ([pltpu.VMEM((tm, k_dim), BF16)] if has_norm else []) + list(scratch)
    return _call(kern, name, (n_blocks,), ("parallel",), in_specs, args, [s for _, s in outs],
                 [o for o, _ in outs], scratch, side_casts)


def _rows(tm, width):
    return pl.BlockSpec((tm, width), lambda i: (i, 0))


QKV_TN = 512
QKV_WIDTH = (N_HEADS + 2 * N_KV_HEADS) * HEAD_DIM
QKV_COLS = D_MODEL + 2 * N_KV_HEADS * LANES
QKV_NORM_BLOCKS = (D_MODEL + N_KV_HEADS * LANES) // QKV_TN
LOG2E = math.log2(math.e)


ATTN_GROUP = 256
ATTN_GROUPS_PER_STEP = 2
K_COL = D_MODEL
V_COL = D_MODEL + N_KV_HEADS * LANES


def _interleave(*task_lists):
    keyed = []
    for n_list, tasks in enumerate(task_lists):
        for k, task in enumerate(tasks):
            keyed.append(((k + 0.5) / len(tasks), n_list, k, task))
    keyed.sort(key=lambda item: item[:3])
    return [item[3] for item in keyed]


def _qkv_tasks(hn_ref, wp_ref, gain_ref, seg_ref, qkv_ref, rows):
    def chunk(c):
        cols = slice(c * QKV_TN, (c + 1) * QKV_TN)
        x = _dot(hn_ref[rows, :], wp_ref[:, cols])
        if c < QKV_NORM_BLOCKS:
            x2 = (x * x).astype(BF16)
            ss = jnp.concatenate(
                [_dot(x2[:, k * MXU_DIM:(k + 1) * MXU_DIM], seg_ref[...])
                 for k in range(QKV_TN // MXU_DIM)], axis=1)
            r = lax.rsqrt(ss * (1.0 / HEAD_DIM) + EPS)
            qkv_ref[rows, cols] = (x * r * gain_ref[:, cols]).astype(BF16)
        else:
            qkv_ref[rows, cols] = x.astype(BF16)

    return [functools.partial(chunk, c) for c in range(QKV_COLS // QKV_TN)]


def _alibi_slopes():
    return [float(2.0 ** (-8.0 * (i + 1) / N_HEADS)) for i in range(N_HEADS)]


def _attend_tasks(qkv_ref, rows, k_prev, v_prev, positions, sinks_ref, o_ref):
    shared = {}

    def prepare():
        qpos, kpos_prev, kpos_cur = positions()
        kpos = jnp.concatenate([kpos_prev, kpos_cur], axis=1)
        dist = qpos - kpos
        valid = (dist >= 0) & (dist < WINDOW)
        lane = lax.broadcasted_iota(jnp.int32, (BLOCK, LANES), 1)
        lo_f = (lane < HEAD_DIM).astype(F32)
        lo = lo_f.astype(BF16)
        hi = (1.0 - lo_f).astype(BF16)
        shared.update(distm=jnp.where(valid, dist.astype(F32), MASK_DIST), lane=lane, lo=lo, hi=hi,
                      lo2=jnp.concatenate([lo, lo], axis=0), hi2=jnp.concatenate([hi, hi], axis=0))

    slopes = _alibi_slopes()

    def head_task(h):
        if not shared:
            prepare()
        distm, lane, lo, hi = shared["distm"], shared["lane"], shared["lo"], shared["hi"]
        lo2, hi2 = shared["lo2"], shared["hi2"]
        kk = jnp.concatenate([k_prev(h), qkv_ref[rows, K_COL + h * LANES:K_COL + (h + 1) * LANES]],
                             axis=0)
        vd = jnp.concatenate([v_prev(h), qkv_ref[rows, V_COL + h * LANES:V_COL + (h + 1) * LANES]],
                             axis=0)
        qs = []
        for g in range(Q_PER_KV):
            col = h * Q_PER_KV * HEAD_DIM + (g // 2) * LANES
            qp = qkv_ref[rows, col:col + LANES]
            qs.append(qp * (lo if g % 2 == 0 else hi))
        s_all = _dot_nt(jnp.concatenate(qs, axis=0), kk)
        ps = []
        rls = []
        for g in range(Q_PER_KV):
            head = h * Q_PER_KV + g
            sg = s_all[g * BLOCK:(g + 1) * BLOCK] - (slopes[head] * LOG2E) * distm
            sink = sinks_ref[head] * LOG2E
            m = jnp.maximum(jnp.max(sg, axis=-1, keepdims=True), sink)
            p = jnp.exp2(sg - m)
            denom = jnp.sum(p, axis=-1, keepdims=True) + jnp.exp2(sink - m)
            ps.append(p.astype(BF16))
            rls.append(1.0 / denom)
        lhs2 = jnp.concatenate(
            [jnp.concatenate([ps[2 * p], ps[2 * p + 1]], axis=1) for p in range(Q_PER_KV // 2)],
            axis=0)
        rhs2 = jnp.concatenate([vd * lo2, vd * hi2], axis=0)
        o2 = _dot(lhs2, rhs2)
        for p in range(Q_PER_KV // 2):
            scale = jnp.where(lane < HEAD_DIM, rls[2 * p], rls[2 * p + 1])
            col = h * Q_PER_KV * HEAD_DIM + p * LANES
            o_ref[rows, col:col + LANES] = (o2[p * BLOCK:(p + 1) * BLOCK] * scale).astype(BF16)

    return [functools.partial(head_task, h) for h in range(N_KV_HEADS)]


def _attn_layer_kernel(h_ref, gain_ref, wp_ref, hg_ref, seg_ref, wo_ref, qpos_ref, kposp_ref,
                       kpos_ref, sinks_ref, out_ref, hn_ref, qkv_ref, o_ref, tail_ref):
    first = pl.program_id(0) == 0

    @pl.when(first)
    def _():
        tail_ref[...] = jnp.zeros(tail_ref.shape, BF16)

    hn_ref[...] = _rms(h_ref[...], gain_ref[...]).astype(BF16)
    kv_width = N_KV_HEADS * LANES
    step_rows = h_ref.shape[0]
    n_groups = step_rows // ATTN_GROUP
    bpg = ATTN_GROUP // BLOCK
    project, attend, emit = [], [], []
    for g in range(n_groups):
        grows = slice(g * ATTN_GROUP, (g + 1) * ATTN_GROUP)
        project.append(_qkv_tasks(hn_ref, wp_ref, hg_ref, seg_ref, qkv_ref, grows))

        def out_chunk(c, grows=grows):
            cols = slice(c * QKV_TN, (c + 1) * QKV_TN)
            out_ref[grows, cols] = h_ref[grows, cols] + _dot(o_ref[grows, :], wo_ref[:, cols])

        emit.append([functools.partial(out_chunk, c) for c in range(D_MODEL // QKV_TN)])
    for b in range(step_rows // BLOCK):
        rows = slice(b * BLOCK, (b + 1) * BLOCK)
        if b == 0:
            k_prev = lambda h: tail_ref[:, h * LANES:(h + 1) * LANES]
            v_prev = lambda h: tail_ref[:, kv_width + h * LANES:kv_width + (h + 1) * LANES]
            positions = lambda rows=rows: (qpos_ref[rows, :],
                                           jnp.where(first, PAD_POS, kposp_ref[...]), kpos_ref[0])
        else:
            prows = slice((b - 1) * BLOCK, b * BLOCK)
            k_prev = lambda h, prows=prows: qkv_ref[prows, K_COL + h * LANES:K_COL + (h + 1) * LANES]
            v_prev = lambda h, prows=prows: qkv_ref[prows, V_COL + h * LANES:V_COL + (h + 1) * LANES]
            positions = lambda rows=rows, b=b: (qpos_ref[rows, :], kpos_ref[b - 1], kpos_ref[b])
        attend.append(_attend_tasks(qkv_ref, rows, k_prev, v_prev, positions, sinks_ref, o_ref))

    for task in project[0]:
        task()
    for g in range(n_groups):
        softmax = [t for b in range(g * bpg, (g + 1) * bpg) for t in attend[b]]
        nxt = project[g + 1] if g + 1 < n_groups else []
        prv = emit[g - 1] if g > 0 else []
        for task in _interleave(softmax, nxt, prv):
            task()
    last = slice(step_rows - BLOCK, step_rows)
    tail_ref[...] = qkv_ref[last, K_COL:K_COL + 2 * kv_width]
    for task in emit[n_groups - 1]:
        task()


def _prep_qkv_weight(w_qkv):
    if w_qkv.shape[1] == QKV_COLS:
        return w_qkv
    w16 = w_qkv.astype(BF16)
    wq = w16[:, :D_MODEL]
    wk = w16[:, D_MODEL:D_MODEL + N_KV_HEADS * HEAD_DIM].reshape(D_MODEL, N_KV_HEADS, HEAD_DIM)
    wv = w16[:, D_MODEL + N_KV_HEADS * HEAD_DIM:].reshape(D_MODEL, N_KV_HEADS, HEAD_DIM)
    dup = lambda w: jnp.concatenate([w, w], axis=-1).reshape(D_MODEL, N_KV_HEADS * LANES)
    return jnp.concatenate([wq, dup(wk), dup(wv)], axis=1)


def _attention_layer(h, positions, mix_norm, w_qkv, q_norm, k_norm, sinks, w_o, side_casts=()):
    head_gain = jnp.concatenate([
        jnp.tile(q_norm.astype(F32), N_HEADS) * (HEAD_DIM ** -0.5 * LOG2E),
        jnp.tile(k_norm.astype(F32), 2 * N_KV_HEADS),
        jnp.ones((N_KV_HEADS * LANES,), F32)]).reshape(1, QKV_COLS)
    seg_ones = jnp.kron(jnp.eye(MXU_DIM // HEAD_DIM, dtype=F32),
                        jnp.ones((HEAD_DIM, HEAD_DIM), F32)).astype(BF16)
    t = h.shape[0]
    rows = ATTN_GROUP * (1 if side_casts else ATTN_GROUPS_PER_STEP)
    n_steps = t // rows
    bps = rows // BLOCK
    pos_row = positions.reshape(t // BLOCK, 1, BLOCK)
    in_specs = [
        _rows(rows, D_MODEL),
        _resident((1, D_MODEL)),
        _resident((D_MODEL, QKV_COLS)),
        _resident((1, QKV_COLS)),
        _resident((MXU_DIM, MXU_DIM)),
        _resident((D_MODEL, D_MODEL)),
        pl.BlockSpec((rows, 1), lambda i: (i, 0)),
        pl.BlockSpec((None, 1, BLOCK), lambda i: (jnp.maximum(bps * i - 1, 0), 0, 0)),
        pl.BlockSpec((bps, 1, BLOCK), lambda i: (i, 0, 0)),
        pl.BlockSpec(memory_space=pltpu.SMEM),
    ]
    args = [h, mix_norm.reshape(1, D_MODEL).astype(F32), _prep_qkv_weight(w_qkv), head_gain, seg_ones,
            w_o.astype(BF16), positions.reshape(t, 1), pos_row, pos_row, sinks.astype(F32)]
    scratch = [pltpu.VMEM((rows, D_MODEL), BF16), pltpu.VMEM((rows, QKV_COLS), BF16),
               pltpu.VMEM((rows, D_MODEL), BF16), pltpu.VMEM((BLOCK, 2 * N_KV_HEADS * LANES), BF16)]
    (out,), casts = _call(_attn_layer_kernel, "attn_layer", (n_steps,), ("arbitrary",), in_specs, args,
                          [_rows(rows, D_MODEL)], [jax.ShapeDtypeStruct((t, D_MODEL), F32)], scratch,
                          side_casts)
    return out, casts


MLP_TF = 512


def _mlp_kernel(h_ref, gain_ref, wu_ref, wd_ref, o_ref, hn_ref):
    j = pl.program_id(1)

    @pl.when(j == 0)
    def _():
        x = h_ref[...]
        hn_ref[...] = _rms(x, gain_ref[...]).astype(BF16)
        o_ref[...] = x

    act = jnp.maximum(_dot(hn_ref[...], wu_ref[...]), 0.0)
    act = (act * act).astype(BF16)
    o_ref[...] += _dot(act, wd_ref[...])


def _mlp(h, gain, w_up16, w_down16, side_casts=()):
    t = h.shape[0]
    tm = min(ROW_BLOCK, t)
    tf = MLP_TF
    nj = D_FF // tf
    in_specs = [
        pl.BlockSpec((tm, D_MODEL), lambda i, j: (i, 0)),
        pl.BlockSpec((1, D_MODEL), lambda i, j: (0, 0)),
        pl.BlockSpec((D_MODEL, tf), lambda i, j: (0, j)),
        pl.BlockSpec((tf, D_MODEL), lambda i, j: (j, 0)),
    ]
    args = [h, gain.reshape(1, D_MODEL).astype(F32), w_up16, w_down16]
    (out,), casts = _call(_mlp_kernel, "mlp", (t // tm, nj), ("parallel", "arbitrary"), in_specs, args,
                          [pl.BlockSpec((tm, D_MODEL), lambda i, j: (i, 0))],
                          [jax.ShapeDtypeStruct((t, D_MODEL), F32)],
                          [pltpu.VMEM((tm, D_MODEL), BF16)], side_casts,
                          step_index=lambda i, j: i * nj + j)
    return out, casts


def _ple_epilogue(c, cols, accs, lhs_ref, extras, outs):
    p_ref, wple_ref = extras
    emb = _dot(p_ref[...].astype(BF16), wple_ref[:, cols])
    outs[0][:, cols] = lhs_ref[:, cols] + emb * _sigmoid(accs[0])


def _ple(h, gain, p_all, layer, w_ple, w_gate):
    t = h.shape[0]
    tm = min(ROW_BLOCK, t)
    first = layer * (t // tm)
    (out,), _ = _row_matmul(
        "ple", t // tm, h, _rows(tm, D_MODEL), tm, w_gate, (0,), COL_CHUNK, D_MODEL // COL_CHUNK,
        extras=[(p_all, pl.BlockSpec((tm, PLE_DIM), lambda i: (first + i, 0))),
                (w_ple, _resident((PLE_DIM, D_MODEL)))],
        outs=[(jax.ShapeDtypeStruct((t, D_MODEL), F32), _rows(tm, D_MODEL))],
        epilogue=_ple_epilogue, gain=gain)
    return out


RGLRU_TB = 256
RGLRU_SEG = RGLRU_TB // SUBLANES
RGLRU_PITCH = 36
TINY = 1e-36


def _rglru_layer_kernel(h_ref, gain_ref, win_ref, cw_ref, cb_ref, wa_ref, ba_ref, wx_ref, bx_ref,
                        lam_ref, wo_ref, out_ref, hn_ref, xb_ref, gate_ref, y_ref, a_ref, b_ref,
                        carry_ref, tail_ref):
    tb = RGLRU_TB
    seg = RGLRU_SEG
    n_slabs = D_MODEL // LANES
    spb = GATE_BLOCK // LANES

    @pl.when(pl.program_id(0) == 0)
    def _():
        carry_ref[...] = jnp.zeros((1, D_MODEL), F32)
        tail_ref[...] = jnp.zeros((CONV_WIDTH - 1, D_MODEL), F32)

    hn_ref[...] = _rms(h_ref[...], gain_ref[...]).astype(BF16)
    lam = lam_ref[...]
    rate = (-LRU_C) * (jnp.log(1.0 + jnp.exp(-jnp.abs(lam))) + jnp.maximum(-lam, 0.0))

    def project(n):
        sl = slice(n * GATE_BLOCK, (n + 1) * GATE_BLOCK)
        xb_ref[:, sl] = _dot(hn_ref[...], win_ref[:, sl])
        gate_ref[:, sl] = _gelu_tanh(
            _dot(hn_ref[...], win_ref[:, D_MODEL + n * GATE_BLOCK:D_MODEL + (n + 1) * GATE_BLOCK])
        ).astype(BF16)

    def gates(n):
        sl = slice(n * GATE_BLOCK, (n + 1) * GATE_BLOCK)

        def shift1(v, k):
            prev_last = tail_ref[k:k + 1, sl]
            tail_ref[k:k + 1, sl] = v[tb - 1:tb, :]
            return jnp.concatenate([prev_last, v[:tb - 1, :]], axis=0)

        x = xb_ref[:, sl]
        acc = cw_ref[0:1, sl] * x
        for k in range(1, CONV_WIDTH):
            acc = cw_ref[k:k + 1, sl] * x + shift1(acc, k - 1)
        xn = acc + cb_ref[:, sl]
        xn16 = xn.astype(BF16)
        r = _sigmoid(_dot(xn16, wa_ref[n]) + ba_ref[:, sl])
        i_gate = _sigmoid(_dot(xn16, wx_ref[n]) + bx_ref[:, sl])
        a = jnp.exp(rate[:, sl] * r)
        v = 1.0 - a * a
        b = (v * lax.rsqrt(jnp.maximum(v, TINY))) * (i_gate * xn)
        for kk in range(spb):
            slab = n * spb + kk
            for s in range(SUBLANES):
                rows = slice(s * RGLRU_PITCH, s * RGLRU_PITCH + seg)
                a_ref[slab, rows, :] = a[s * seg:(s + 1) * seg, kk * LANES:(kk + 1) * LANES]
                b_ref[slab, rows, :] = b[s * seg:(s + 1) * seg, kk * LANES:(kk + 1) * LANES]

    project(0)
    for n in range(N_GATE_BLOCKS):
        if n + 1 < N_GATE_BLOCKS:
            project(n + 1)
        gates(n)

    def step(q, carry):
        hs, prods = carry
        rows = pl.ds(q, SUBLANES, stride=RGLRU_PITCH)
        new_hs = []
        new_prods = []
        for k in range(n_slabs):
            av = a_ref[k, rows, :]
            hk = av * hs[k] + b_ref[k, rows, :]
            pk = av * prods[k]
            b_ref[k, rows, :] = hk
            a_ref[k, rows, :] = pk
            new_hs.append(hk)
            new_prods.append(pk)
        return tuple(new_hs), tuple(new_prods)

    zeros = tuple(jnp.zeros((SUBLANES, LANES), F32) for _ in range(n_slabs))
    ones = tuple(jnp.ones((SUBLANES, LANES), F32) for _ in range(n_slabs))
    hs, prods = lax.fori_loop(0, seg, step, (zeros, ones))

    for k in range(n_slabs):
        lanes = slice(k * LANES, (k + 1) * LANES)
        cur = carry_ref[:, lanes]
        for s in range(SUBLANES):
            rows = slice(s * RGLRU_PITCH, s * RGLRU_PITCH + seg)
            hfull = b_ref[k, rows, :] + a_ref[k, rows, :] * cur
            out_rows = slice(s * seg, (s + 1) * seg)
            y_ref[out_rows, lanes] = (hfull * gate_ref[out_rows, lanes].astype(F32)).astype(BF16)
            cur = prods[k][s:s + 1, :] * cur + hs[k][s:s + 1, :]
        carry_ref[:, lanes] = cur

    for c in range(D_MODEL // COL_CHUNK):
        cols = slice(c * COL_CHUNK, (c + 1) * COL_CHUNK)
        out_ref[:, cols] = h_ref[:, cols] + _dot(y_ref[...], wo_ref[:, cols])


def _rglru_layer(h, mix_norm, w_in, conv_w, conv_b, w_gate_a, b_gate_a, w_gate_x, b_gate_x,
                 lru_lambda, w_o, side_casts=()):
    t = h.shape[0]
    tb = RGLRU_TB
    row = lambda v: v.reshape(1, D_MODEL).astype(F32)
    scan_scratch = pltpu.VMEM((D_MODEL // LANES, SUBLANES * RGLRU_PITCH, LANES), F32)
    in_specs = [
        _rows(tb, D_MODEL), _resident((1, D_MODEL)), _resident((D_MODEL, 2 * D_MODEL)),
        _resident((CONV_WIDTH, D_MODEL)), _resident((1, D_MODEL)),
        _resident((N_GATE_BLOCKS, GATE_BLOCK, GATE_BLOCK)), _resident((1, D_MODEL)),
        _resident((N_GATE_BLOCKS, GATE_BLOCK, GATE_BLOCK)), _resident((1, D_MODEL)),
        _resident((1, D_MODEL)), _resident((D_MODEL, D_MODEL)),
    ]
    args = [h, row(mix_norm), w_in.astype(BF16), conv_w.astype(F32), row(conv_b),
            w_gate_a.astype(BF16), row(b_gate_a), w_gate_x.astype(BF16), row(b_gate_x),
            row(lru_lambda), w_o.astype(BF16)]
    scratch = [pltpu.VMEM((tb, D_MODEL), BF16), pltpu.VMEM((tb, D_MODEL), F32),
               pltpu.VMEM((tb, D_MODEL), BF16), pltpu.VMEM((tb, D_MODEL), BF16),
               scan_scratch, scan_scratch,
               pltpu.VMEM((1, D_MODEL), F32), pltpu.VMEM((CONV_WIDTH - 1, D_MODEL), F32)]
    (out,), casts = _call(_rglru_layer_kernel, "rglru_layer", (t // tb,), ("arbitrary",), in_specs,
                          args, [_rows(tb, D_MODEL)], [jax.ShapeDtypeStruct((t, D_MODEL), F32)],
                          scratch, side_casts)
    return out, casts


S5_TM = 512
S5_IN_PITCH = 20
S5_GLU_PITCH = 36


def _s5_in_epilogue(c, cols, accs, lhs_ref, extras, outs, acc_ref):
    n = accs[0].shape[0] // S5_CHUNK
    for k in range(acc_ref.shape[0]):
        for ch in range(n):
            acc_ref[k, ch * S5_IN_PITCH:ch * S5_IN_PITCH + S5_CHUNK, :] = (
                accs[0][ch * S5_CHUNK:(ch + 1) * S5_CHUNK, k * LANES:(k + 1) * LANES])
        for l in range(S5_CHUNK):
            outs[0][l, :, cols.start + k * LANES:cols.start + (k + 1) * LANES] = (
                acc_ref[k, pl.ds(l, n, stride=S5_IN_PITCH), :])


def _s5_in(h, mix_norm, w_in, side_casts=()):
    t = h.shape[0]
    nc = t // S5_CHUNK
    tm = min(S5_TM, t)
    tn = COL_CHUNK
    (u,), casts = _row_matmul(
        "s5_in", t // tm, h, _rows(tm, D_MODEL), tm, w_in, (0,), tn, D_MODEL // tn,
        extras=[],
        outs=[(jax.ShapeDtypeStruct((S5_CHUNK, nc, D_MODEL), F32),
               pl.BlockSpec((S5_CHUNK, tm // S5_CHUNK, D_MODEL), lambda i: (0, i, 0)))],
        epilogue=_s5_in_epilogue, gain=mix_norm,
        scratch=[pltpu.VMEM((tn // LANES, (tm // S5_CHUNK) * S5_IN_PITCH, LANES), F32)],
        side_casts=side_casts)
    return u, casts


def _s5_ssm_kernel(u_ref, lr_ref, li_ref, ldt_ref, bre_ref, bim_ref, cre_ref, cim_ref, dsk_ref,
                   g_ref, tbig_ref, wend_ref, wc_ref, x_ref, e_ref, s_ref, y_ref):
    nc = u_ref.shape[1]
    L = S5_CHUNK
    P = S5_STATE_COLS
    lr = lr_ref[...]
    li = li_ref[...]
    dt = jnp.exp(ldt_ref[...])
    mag = jnp.exp(lr * dt)
    abar_re = mag * jnp.cos(li * dt)
    abar_im = mag * jnp.sin(li * dt)
    n_re = abar_re - 1.0
    n_im = abar_im
    den = lr * lr + li * li
    z_re = (n_re * lr + n_im * li) / den
    z_im = (n_im * lr - n_re * li) / den
    bre = bre_ref[...]
    bim = bim_ref[...]
    bbar_re = z_re * bre - z_im * bim
    bbar_im = z_re * bim + z_im * bre
    cre = cre_ref[...]
    cim = cim_ref[...]

    def power(d):
        m = jnp.exp((d * dt) * lr)
        ang = (d * dt) * li
        return m * jnp.cos(ang), m * jnp.sin(ang)

    def w_in_state(d):
        pr, pi = power(d)
        return jnp.concatenate([pr * bbar_re - pi * bbar_im, pr * bbar_im + pi * bbar_re],
                               axis=1).astype(BF16)

    def w_out_state(d):
        pr, pi = power(d)
        return jnp.concatenate([cre * pr - cim * pi, -(cre * pi + cim * pr)], axis=1).astype(BF16)

    wt0 = w_out_state(0)
    for d in range(L):
        wj = w_in_state(d)
        m_d = _dot_nt(wj, wt0).astype(BF16)
        for jj in range(L - d):
            tt = jj + d
            tbig_ref[jj * LANES:(jj + 1) * LANES, tt * LANES:(tt + 1) * LANES] = m_d
        wend_ref[(L - 1 - d) * LANES:(L - d) * LANES, :] = wj
        wc_ref[d * LANES:(d + 1) * LANES, :] = w_out_state(d + 1)
    for tt in range(0, L, 2):
        tbig_ref[(tt + 1) * LANES:(tt + 2) * LANES, tt * LANES:(tt + 1) * LANES] = (
            jnp.zeros((LANES, LANES), BF16))

    x_ref[...] = jnp.concatenate([u_ref[l] for l in range(L)], axis=1).astype(BF16)
    e_ref[...] = _dot(x_ref[...], wend_ref[...])

    row = lax.broadcasted_iota(jnp.int32, (SUBLANES, P), 0)
    steps = []
    for s in (1, 2, 4):
        pr, pi = power(L * s)
        steps.append((s, row >= s, pr, pi))
    carry_pw = [power(L * (r + 1)) for r in range(SUBLANES)]
    cpw_re = jnp.concatenate([c[0] for c in carry_pw], axis=0)
    cpw_im = jnp.concatenate([c[1] for c in carry_pw], axis=0)

    s_ref[0:SUBLANES, :] = jnp.zeros((SUBLANES, 2 * P), F32)

    def tile_step(i, carry):
        s_re, s_im = carry
        base = i * SUBLANES
        e = e_ref[base:base + SUBLANES, :]
        t_re = e[:, :P]
        t_im = e[:, P:]
        for s, keep, pr, pi in steps:
            sh_re = jnp.where(keep, pltpu.roll(t_re, s, 0), 0.0)
            sh_im = jnp.where(keep, pltpu.roll(t_im, s, 0), 0.0)
            t_re, t_im = t_re + pr * sh_re - pi * sh_im, t_im + pr * sh_im + pi * sh_re
        f_re = t_re + cpw_re * s_re - cpw_im * s_im
        f_im = t_im + cpw_re * s_im + cpw_im * s_re
        s_ref[base + SUBLANES:base + 2 * SUBLANES, :] = jnp.concatenate([f_re, f_im], axis=1)
        return f_re[SUBLANES - 1:SUBLANES, :], f_im[SUBLANES - 1:SUBLANES, :]

    n_tiles = nc // SUBLANES
    tiles_per_block = n_tiles // (L // 2)
    carry = (jnp.zeros((1, P), F32), jnp.zeros((1, P), F32))
    for b in range(L // 2):
        k_rows = (2 * b + 2) * LANES
        cols = slice(2 * b * LANES, (2 * b + 2) * LANES)
        y_ref[:, cols] = _dot(x_ref[:, :k_rows], tbig_ref[:k_rows, cols])
        for i in range(b * tiles_per_block, (b + 1) * tiles_per_block):
            carry = tile_step(i, carry)

    s_start = s_ref[SUBLANES - 1:SUBLANES - 1 + nc, :].astype(BF16)
    dsk = dsk_ref[...]
    for b in range(L // 2):
        cols = slice(2 * b * LANES, (2 * b + 2) * LANES)
        y = y_ref[:, cols] + _dot_nt(s_start, wc_ref[cols, :])
        for l in (2 * b, 2 * b + 1):
            yl = y[:, (l - 2 * b) * LANES:(l - 2 * b + 1) * LANES] + dsk * u_ref[l]
            g_ref[l] = _gelu_tanh(yl).astype(BF16)


def _s5_ssm(u, a_re, a_im, log_dt, b_re, b_im, c_re, c_im, d_skip, side_casts=()):
    L, nc, _ = u.shape
    P = S5_STATE_COLS
    gpt = S5_GROUPS_PER_TILE
    nt = N_LANE_TILES
    f = lambda v: v.astype(F32)
    row = lambda v: f(v).reshape(nt, 1, P)
    ldt = jnp.broadcast_to(f(log_dt)[:, None], (N_SSM_GROUPS, SSM_STATE))
    same_group = (jnp.arange(LANES)[:, None] // SSM_GROUP) == (jnp.arange(P)[None, :] // SSM_STATE)

    def block_diag(x):
        return jnp.where(same_group, jnp.tile(x, (1, 1, gpt)), 0.0)

    def b_layout(b):
        bt = f(b).reshape(nt, gpt, SSM_STATE, SSM_GROUP).transpose(0, 1, 3, 2)
        return block_diag(bt.reshape(nt, LANES, SSM_STATE))

    def c_layout(c):
        return block_diag(f(c).reshape(nt, LANES, SSM_STATE))

    tile_row = pl.BlockSpec((None, 1, P), lambda k: (k, 0, 0))
    tile_mat = pl.BlockSpec((None, LANES, P), lambda k: (k, 0, 0))
    in_specs = [pl.BlockSpec((L, nc, LANES), lambda k: (0, 0, k)),
                tile_row, tile_row, tile_row, tile_mat, tile_mat, tile_mat, tile_mat,
                pl.BlockSpec((1, LANES), lambda k: (0, k))]
    args = [u, row(a_re), row(a_im), row(ldt), b_layout(b_re), b_layout(b_im), c_layout(c_re),
            c_layout(c_im), f(d_skip).reshape(1, D_MODEL)]
    scratch = [pltpu.VMEM((L * LANES, L * LANES), BF16),
               pltpu.VMEM((L * LANES, 2 * P), BF16),
               pltpu.VMEM((L * LANES, 2 * P), BF16),
               pltpu.VMEM((nc, L * LANES), BF16),
               pltpu.VMEM((nc, 2 * P), F32),
               pltpu.VMEM((nc + SUBLANES, 2 * P), F32),
               pltpu.VMEM((nc, L * LANES), F32)]
    (g,), casts = _call(_s5_ssm_kernel, "s5_ssm", (nt,), ("arbitrary",), in_specs, args,
                        [pl.BlockSpec((L, nc, LANES), lambda k: (0, 0, k))],
                        [jax.ShapeDtypeStruct((L, nc, D_MODEL), BF16)], scratch, side_casts)
    return g, casts


def _glu_epilogue(c, cols, accs, lhs_ref, extras, outs, y_ref):
    y = accs[0] * _sigmoid(accs[1])
    n = y.shape[0] // S5_CHUNK
    assert n <= S5_GLU_PITCH
    for k in range(y_ref.shape[0]):
        for l in range(S5_CHUNK):
            y_ref[k, l * S5_GLU_PITCH:l * S5_GLU_PITCH + n, :] = (
                y[l * n:(l + 1) * n, k * LANES:(k + 1) * LANES])
        lanes = slice(cols.start + k * LANES, cols.start + (k + 1) * LANES)
        for ch in range(n):
            rows = slice(ch * S5_CHUNK, (ch + 1) * S5_CHUNK)
            outs[0][rows, lanes] = (extras[0][rows, lanes]
                                    + y_ref[k, pl.ds(ch, S5_CHUNK, stride=S5_GLU_PITCH), :])


def _s5_glu(g, w_glu, h, side_casts=()):
    L, nc, _ = g.shape
    t = nc * L
    tm = min(S5_TM, t)
    (out,), casts = _row_matmul(
        "s5_glu", t // tm, g, pl.BlockSpec((L, tm // L, D_MODEL), lambda i: (0, i, 0)), tm, w_glu,
        (0, D_MODEL), COL_CHUNK, D_MODEL // COL_CHUNK,
        extras=[(h, _rows(tm, D_MODEL))],
        outs=[(jax.ShapeDtypeStruct((t, D_MODEL), F32), _rows(tm, D_MODEL))],
        epilogue=_glu_epilogue,
        scratch=[pltpu.VMEM((COL_CHUNK // LANES, S5_CHUNK * S5_GLU_PITCH, LANES), F32)],
        side_casts=side_casts)
    return out, casts


def _s5_layer(h, mix_norm, w_in, a_re, a_im, log_dt, b_re, b_im, c_re, c_im, d_skip, w_glu,
              side_casts=((), (), ())):
    u, casts_in = _s5_in(h, mix_norm, w_in.astype(BF16), side_casts[0])
    g, casts_ssm = _s5_ssm(u, a_re, a_im, log_dt, b_re, b_im, c_re, c_im, d_skip, side_casts[1])
    out, casts_glu = _s5_glu(g, w_glu.astype(BF16), h, side_casts[2])
    return out, (casts_in, casts_ssm, casts_glu)


def kernel(x, p, positions,
           l0_mix_norm, l0_w_qkv, l0_q_norm, l0_k_norm, l0_sinks, l0_w_o,
           l0_mlp_norm, l0_w_up, l0_w_down, l0_ple_norm, l0_w_ple_gate, l0_w_ple,
           l1_mix_norm, l1_w_in, l1_conv_w, l1_conv_b, l1_w_gate_a, l1_b_gate_a, l1_w_gate_x,
           l1_b_gate_x, l1_lru_lambda, l1_w_o,
           l1_mlp_norm, l1_w_up, l1_w_down, l1_ple_norm, l1_w_ple_gate, l1_w_ple,
           l2_mix_norm, l2_w_in, l2_a_re, l2_a_im, l2_log_dt, l2_b_re, l2_b_im, l2_c_re, l2_c_im,
           l2_d_skip, l2_w_glu,
           l2_mlp_norm, l2_w_up, l2_w_down, l2_ple_norm, l2_w_ple_gate, l2_w_ple,
           l3_mix_norm, l3_w_qkv, l3_q_norm, l3_k_norm, l3_sinks, l3_w_o,
           l3_mlp_norm, l3_w_up, l3_w_down, l3_ple_norm, l3_w_ple_gate, l3_w_ple):
    batch, t, _ = x.shape
    assert batch == 1 and t % (8 * BLOCK) == 0
    pos = positions.reshape(t).astype(jnp.int32)
    tails = [
        (l0_mlp_norm, l0_w_up, l0_w_down, l0_ple_norm, l0_w_ple_gate, l0_w_ple),
        (l1_mlp_norm, l1_w_up, l1_w_down, l1_ple_norm, l1_w_ple_gate, l1_w_ple),
        (l2_mlp_norm, l2_w_up, l2_w_down, l2_ple_norm, l2_w_ple_gate, l2_w_ple),
        (l3_mlp_norm, l3_w_up, l3_w_down, l3_ple_norm, l3_w_ple_gate, l3_w_ple),
    ]
    h = x.reshape(t, D_MODEL).astype(F32)
    p_all = p.reshape(DEPTH * t, PLE_DIM)
    for i in range(DEPTH):
        if i == 0:
            h, (up16, down16, gate16, l1_w_in16, l1_w_o16) = _attention_layer(
                h, pos, l0_mix_norm, l0_w_qkv, l0_q_norm, l0_k_norm, l0_sinks, l0_w_o,
                side_casts=(l0_w_up, l0_w_down, l0_w_ple_gate, l1_w_in, l1_w_o))
        elif i == 1:
            h, (gate16, l2_w_in16, l2_w_glu16, l2_gate16) = _rglru_layer(
                h, l1_mix_norm, l1_w_in16, l1_conv_w, l1_conv_b, l1_w_gate_a, l1_b_gate_a,
                l1_w_gate_x, l1_b_gate_x, l1_lru_lambda, l1_w_o16,
                side_casts=(l1_w_ple_gate, l2_w_in, l2_w_glu, l2_w_ple_gate))
        elif i == 2:
            gate16 = l2_gate16
            h, (_, l3_mlp16, (l3_w_qkv16, l3_w_o16, l3_gate16)) = _s5_layer(
                h, l2_mix_norm, l2_w_in16, l2_a_re, l2_a_im, l2_log_dt, l2_b_re, l2_b_im,
                l2_c_re, l2_c_im, l2_d_skip, l2_w_glu16,
                side_casts=((), (l3_w_up, l3_w_down), (l3_w_qkv, l3_w_o, l3_w_ple_gate)))
        else:
            gate16 = l3_gate16
            up16, down16 = l3_mlp16
            h, _ = _attention_layer(h, pos, l3_mix_norm, l3_w_qkv16, l3_q_norm, l3_k_norm, l3_sinks,
                                    l3_w_o16)
        mlp_norm, _, _, ple_norm, _, w_ple = tails[i]
        next_mlp = {0: (l1_w_up, l1_w_down), 1: (l2_w_up, l2_w_down)}.get(i, ())
        h, next16 = _mlp(h, mlp_norm, up16, down16, side_casts=next_mlp)
        if next16:
            up16, down16 = next16
        h = _ple(h, ple_norm, p_all, i, w_ple.astype(BF16), gate16)
    return h.reshape(batch, t, D_MODEL).astype(x.dtype)
```

```python
import functools
import math

import jax
import jax.numpy as jnp
from jax import lax
from jax.experimental import pallas as pl
from jax.experimental.pallas import tpu as pltpu

F32 = jnp.float32
BF16 = jnp.bfloat16

D_MODEL = 2048
DEPTH = 4
N_HEADS = 32
N_KV_HEADS = 4
HEAD_DIM = 64
Q_PER_KV = N_HEADS // N_KV_HEADS
WINDOW = 128
BLOCK = 128
PAD_POS = -(1 << 20)
N_GATE_BLOCKS = 8
GATE_BLOCK = D_MODEL // N_GATE_BLOCKS
CONV_WIDTH = 4
LRU_C = 8.0
SSM_GROUP = 16
N_SSM_GROUPS = D_MODEL // SSM_GROUP
SSM_STATE = 64
D_FF = 4 * D_MODEL
PLE_DIM = 256
EPS = 1e-6

LANES = 128
SUBLANES = 8
MXU_DIM = 256
VMEM_LIMIT_BYTES = 60 * 1024 * 1024
MASK_DIST = 1e33

ROW_BLOCK = 1024
PLE_ROWS = 512
PREP_STEPS = 8
COL_CHUNK = 512

S5_CHUNK = 16
S5_GROUPS_PER_TILE = LANES // SSM_GROUP
S5_STATE_COLS = S5_GROUPS_PER_TILE * SSM_STATE
N_LANE_TILES = D_MODEL // LANES


def _params(*semantics):
    return pltpu.CompilerParams(dimension_semantics=semantics,
                                vmem_limit_bytes=VMEM_LIMIT_BYTES)


def _rms(x, gain):
    ms = jnp.mean(x * x, axis=-1, keepdims=True)
    return x * lax.rsqrt(ms + EPS) * gain


def _gelu_tanh(x):
    return 0.5 * x * (1.0 + jnp.tanh(math.sqrt(2.0 / math.pi) * (x + 0.044715 * (x * x * x))))


def _sigmoid(x):
    return 0.5 * jnp.tanh(0.5 * x) + 0.5


def _dot(a, b):
    return jnp.dot(a, b, preferred_element_type=F32)


def _dot_nt(a, b):
    return lax.dot_general(a, b, (((1,), (1,)), ((), ())), preferred_element_type=F32)


def _resident(shape):
    zeros = (0,) * len(shape)
    return pl.BlockSpec(shape, lambda i: zeros, pipeline_mode=pl.Buffered(1))


def _with_side_casts(kernel, n_in, n_out, n_cast):
    def wrapped(*refs):
        ins = refs[:n_in]
        cast_ins = refs[n_in:n_in + n_cast]
        outs = refs[n_in + n_cast:n_in + n_cast + n_out]
        cast_outs = refs[n_in + n_cast + n_out:n_in + 2 * n_cast + n_out]
        scratch = refs[n_in + 2 * n_cast + n_out:]
        for src, dst in zip(cast_ins, cast_outs):
            if dst.shape == src.shape:
                dst[...] = src[...].astype(BF16)
            else:
                _qkv_rows_to_bf16(src, dst)
        kernel(*ins, *outs, *scratch)

    return wrapped


def _qkv_rows_to_bf16(src_ref, dst_ref):
    dst_ref[:, :D_MODEL] = src_ref[:, :D_MODEL].astype(BF16)
    for j in range(2 * N_KV_HEADS):
        head = src_ref[:, D_MODEL + j * HEAD_DIM:D_MODEL + (j + 1) * HEAD_DIM]
        dst_ref[:, D_MODEL + j * LANES:D_MODEL + (j + 1) * LANES] = (
            jnp.concatenate([head, head], axis=1).astype(BF16))


def _call(kernel, name, grid, semantics, in_specs, args, out_specs, out_shape, scratch=(),
          side_casts=(), step_index=None):
    in_specs = list(in_specs)
    out_specs = list(out_specs)
    out_shape = list(out_shape)
    args = list(args)
    n_in, n_out, n_cast = len(in_specs), len(out_specs), len(side_casts)
    if n_cast:
        n_steps = math.prod(grid)
        if step_index is None:
            step_index = lambda i: i
        for w in side_casts:
            out_cols = QKV_COLS if w.shape[1] == QKV_WIDTH else w.shape[1]
            rows = w.shape[0] // n_steps
            assert rows % 16 == 0 and rows * n_steps == w.shape[0]
            in_specs.append(pl.BlockSpec((rows, w.shape[1]), lambda *g: (step_index(*g), 0)))
            out_specs.append(pl.BlockSpec((rows, out_cols), lambda *g: (step_index(*g), 0)))
            out_shape.append(jax.ShapeDtypeStruct((w.shape[0], out_cols), BF16))
            args.append(w.astype(F32))
        kernel = _with_side_casts(kernel, n_in, n_out, n_cast)
    res = pl.pallas_call(
        kernel,
        grid=grid,
        in_specs=in_specs,
        out_specs=out_specs,
        out_shape=out_shape,
        scratch_shapes=list(scratch),
        compiler_params=_params(*semantics),
        name=name,
    )(*args)
    return list(res[:n_out]), list(res[n_out:])


def _row_matmul_kernel(*refs, n_extra, n_out, has_norm, col_offsets, tn, n_chunks, epilogue):
    lhs_ref = refs[0]
    pos = 1
    if has_norm:
        gain_ref = refs[1]
        pos = 2
    w_ref = refs[pos]
    pos += 1
    extra_refs = refs[pos:pos + n_extra]
    pos += n_extra
    out_refs = refs[pos:pos + n_out]
    pos += n_out
    if has_norm:
        src = refs[pos]
        pos += 1
        src[...] = _rms(lhs_ref[...], gain_ref[...]).astype(BF16)
    else:
        src = lhs_ref
    scratch_refs = refs[pos:]
    for c in range(n_chunks):
        cols = slice(c * tn, (c + 1) * tn)
        a = src[...]
        a = a.reshape(-1, a.shape[-1])
        accs = [_dot(a, w_ref[:, off + c * tn:off + (c + 1) * tn]) for off in col_offsets]
        epilogue(c, cols, accs, lhs_ref, extra_refs, out_refs, *scratch_refs)


def _row_matmul(name, n_blocks, lhs, lhs_spec, tm, w, col_offsets, tn, n_chunks, extras, outs,
                epilogue, gain=None, scratch=(), side_casts=()):
    has_norm = gain is not None
    k_dim = w.shape[0]
    args = [lhs]
    in_specs = [lhs_spec]
    if has_norm:
        args.append(gain.reshape(1, k_dim).astype(F32))
        in_specs.append(_resident((1, k_dim)))
    args.append(w)
    in_specs.append(_resident(w.shape))
    for a, s in extras:
        args.append(a)
        in_specs.append(s)
    kern = functools.partial(_row_matmul_kernel, n_extra=len(extras), n_out=len(outs),
                             has_norm=has_norm, col_offsets=tuple(col_offsets), tn=tn,
                             n_chunks=n_chunks, epilogue=epilogue)
    scratch = ([pltpu.VMEM((tm, k_dim), BF16)] if has_norm else []) + list(scratch)
    return _call(kern, name, (n_blocks,), ("parallel",), in_specs, args, [s for _, s in outs],
                 [o for o, _ in outs], scratch, side_casts)


def _rows(tm, width):
    return pl.BlockSpec((tm, width), lambda i: (i, 0))


QKV_TN = 512
QKV_WIDTH = (N_HEADS + 2 * N_KV_HEADS) * HEAD_DIM
QKV_COLS = D_MODEL + 2 * N_KV_HEADS * LANES
QKV_NORM_BLOCKS = (D_MODEL + N_KV_HEADS * LANES) // QKV_TN
LOG2E = math.log2(math.e)


ATTN_GROUP = 256
ATTN_GROUPS_PER_STEP = 2
K_COL = D_MODEL
V_COL = D_MODEL + N_KV_HEADS * LANES


def _interleave(*task_lists):
    keyed = []
    for n_list, tasks in enumerate(task_lists):
        for k, task in enumerate(tasks):
            keyed.append(((k + 0.5) / len(tasks), n_list, k, task))
    keyed.sort(key=lambda item: item[:3])
    return [item[3] for item in keyed]


def _qkv_tasks(hn_ref, wp_ref, gain_ref, seg_ref, qkv_ref, rows):
    def chunk(c):
        cols = slice(c * QKV_TN, (c + 1) * QKV_TN)
        x = _dot(hn_ref[rows, :], wp_ref[:, cols])
        if c < QKV_NORM_BLOCKS:
            x2 = (x * x).astype(BF16)
            ss = jnp.concatenate(
                [_dot(x2[:, k * MXU_DIM:(k + 1) * MXU_DIM], seg_ref[...])
                 for k in range(QKV_TN // MXU_DIM)], axis=1)
            r = lax.rsqrt(ss * (1.0 / HEAD_DIM) + EPS)
            qkv_ref[rows, cols] = (x * r * gain_ref[:, cols]).astype(BF16)
        else:
            qkv_ref[rows, cols] = x.astype(BF16)

    return [functools.partial(chunk, c) for c in range(QKV_COLS // QKV_TN)]


def _alibi_slopes():
    return [float(2.0 ** (-8.0 * (i + 1) / N_HEADS)) for i in range(N_HEADS)]


def _attend_tasks(qkv_ref, rows, k_prev, v_prev, positions, sinks_ref, o_ref):
    shared = {}

    def prepare():
        qpos, kpos_prev, kpos_cur = positions()
        kpos = jnp.concatenate([kpos_prev, kpos_cur], axis=1)
        dist = qpos - kpos
        valid = (dist >= 0) & (dist < WINDOW)
        lane = lax.broadcasted_iota(jnp.int32, (BLOCK, LANES), 1)
        lo_f = (lane < HEAD_DIM).astype(F32)
        lo = lo_f.astype(BF16)
        hi = (1.0 - lo_f).astype(BF16)
        shared.update(distm=jnp.where(valid, dist.astype(F32), MASK_DIST), lane=lane, lo=lo, hi=hi,
                      lo2=jnp.concatenate([lo, lo], axis=0), hi2=jnp.concatenate([hi, hi], axis=0))

    slopes = _alibi_slopes()

    def head_task(h):
        if not shared:
            prepare()
        distm, lane, lo, hi = shared["distm"], shared["lane"], shared["lo"], shared["hi"]
        lo2, hi2 = shared["lo2"], shared["hi2"]
        kk = jnp.concatenate([k_prev(h), qkv_ref[rows, K_COL + h * LANES:K_COL + (h + 1) * LANES]],
                             axis=0)
        vd = jnp.concatenate([v_prev(h), qkv_ref[rows, V_COL + h * LANES:V_COL + (h + 1) * LANES]],
                             axis=0)
        qs = []
        for g in range(Q_PER_KV):
            col = h * Q_PER_KV * HEAD_DIM + (g // 2) * LANES
            qp = qkv_ref[rows, col:col + LANES]
            qs.append(qp * (lo if g % 2 == 0 else hi))
        s_all = _dot_nt(jnp.concatenate(qs, axis=0), kk)
        ps = []
        rls = []
        for g in range(Q_PER_KV):
            head = h * Q_PER_KV + g
            sg = s_all[g * BLOCK:(g + 1) * BLOCK] - (slopes[head] * LOG2E) * distm
            sink = sinks_ref[head] * LOG2E
            m = jnp.maximum(jnp.max(sg, axis=-1, keepdims=True), sink)
            p = jnp.exp2(sg - m)
            denom = jnp.sum(p, axis=-1, keepdims=True) + jnp.exp2(sink - m)
            ps.append(p.astype(BF16))
            rls.append(1.0 / denom)
        lhs2 = jnp.concatenate(
            [jnp.concatenate([ps[2 * p], ps[2 * p + 1]], axis=1) for p in range(Q_PER_KV // 2)],
            axis=0)
        rhs2 = jnp.concatenate([vd * lo2, vd * hi2], axis=0)
        o2 = _dot(lhs2, rhs2)
        for p in range(Q_PER_KV // 2):
            scale = jnp.where(lane < HEAD_DIM, rls[2 * p], rls[2 * p + 1])
            col = h * Q_PER_KV * HEAD_DIM + p * LANES
            o_ref[rows, col:col + LANES] = (o2[p * BLOCK:(p + 1) * BLOCK] * scale).astype(BF16)

    return [functools.partial(head_task, h) for h in range(N_KV_HEADS)]


def _attn_layer_kernel(h_ref, gain_ref, wp_ref, hg_ref, seg_ref, wo_ref, qpos_ref, kposp_ref,
                       kpos_ref, sinks_ref, out_ref, hn_ref, qkv_ref, o_ref, tail_ref):
    first = pl.program_id(0) == 0

    @pl.when(first)
    def _():
        tail_ref[...] = jnp.zeros(tail_ref.shape, BF16)

    hn_ref[...] = _rms(h_ref[...], gain_ref[...]).astype(BF16)
    kv_width = N_KV_HEADS * LANES
    step_rows = h_ref.shape[0]
    n_groups = step_rows // ATTN_GROUP
    bpg = ATTN_GROUP // BLOCK
    project, attend, emit = [], [], []
    for g in range(n_groups):
        grows = slice(g * ATTN_GROUP, (g + 1) * ATTN_GROUP)
        project.append(_qkv_tasks(hn_ref, wp_ref, hg_ref, seg_ref, qkv_ref, grows))

        def out_chunk(c, grows=grows):
            cols = slice(c * QKV_TN, (c + 1) * QKV_TN)
            out_ref[grows, cols] = h_ref[grows, cols] + _dot(o_ref[grows, :], wo_ref[:, cols])

        emit.append([functools.partial(out_chunk, c) for c in range(D_MODEL // QKV_TN)])
    for b in range(step_rows // BLOCK):
        rows = slice(b * BLOCK, (b + 1) * BLOCK)
        if b == 0:
            k_prev = lambda h: tail_ref[:, h * LANES:(h + 1) * LANES]
            v_prev = lambda h: tail_ref[:, kv_width + h * LANES:kv_width + (h + 1) * LANES]
            positions = lambda rows=rows: (qpos_ref[rows, :],
                                           jnp.where(first, PAD_POS, kposp_ref[...]), kpos_ref[0])
        else:
            prows = slice((b - 1) * BLOCK, b * BLOCK)
            k_prev = lambda h, prows=prows: qkv_ref[prows, K_COL + h * LANES:K_COL + (h + 1) * LANES]
            v_prev = lambda h, prows=prows: qkv_ref[prows, V_COL + h * LANES:V_COL + (h + 1) * LANES]
            positions = lambda rows=rows, b=b: (qpos_ref[rows, :], kpos_ref[b - 1], kpos_ref[b])
        attend.append(_attend_tasks(qkv_ref, rows, k_prev, v_prev, positions, sinks_ref, o_ref))

    for task in project[0]:
        task()
    for g in range(n_groups):
        softmax = [t for b in range(g * bpg, (g + 1) * bpg) for t in attend[b]]
        nxt = project[g + 1] if g + 1 < n_groups else []
        prv = emit[g - 1] if g > 0 else []
        for task in _interleave(softmax, nxt, prv):
            task()
    last = slice(step_rows - BLOCK, step_rows)
    tail_ref[...] = qkv_ref[last, K_COL:K_COL + 2 * kv_width]
    for task in emit[n_groups - 1]:
        task()


def _prep_qkv_weight(w_qkv):
    if w_qkv.shape[1] == QKV_COLS:
        return w_qkv
    w16 = w_qkv.astype(BF16)
    wq = w16[:, :D_MODEL]
    wk = w16[:, D_MODEL:D_MODEL + N_KV_HEADS * HEAD_DIM].reshape(D_MODEL, N_KV_HEADS, HEAD_DIM)
    wv = w16[:, D_MODEL + N_KV_HEADS * HEAD_DIM:].reshape(D_MODEL, N_KV_HEADS, HEAD_DIM)
    dup = lambda w: jnp.concatenate([w, w], axis=-1).reshape(D_MODEL, N_KV_HEADS * LANES)
    return jnp.concatenate([wq, dup(wk), dup(wv)], axis=1)


def _attention_layer(h, positions, mix_norm, w_qkv, q_norm, k_norm, sinks, w_o, side_casts=()):
    head_gain = jnp.concatenate([
        jnp.tile(q_norm.astype(F32), N_HEADS) * (HEAD_DIM ** -0.5 * LOG2E),
        jnp.tile(k_norm.astype(F32), 2 * N_KV_HEADS),
        jnp.ones((N_KV_HEADS * LANES,), F32)]).reshape(1, QKV_COLS)
    seg_ones = jnp.kron(jnp.eye(MXU_DIM // HEAD_DIM, dtype=F32),
                        jnp.ones((HEAD_DIM, HEAD_DIM), F32)).astype(BF16)
    t = h.shape[0]
    rows = ATTN_GROUP * (1 if side_casts else ATTN_GROUPS_PER_STEP)
    n_steps = t // rows
    bps = rows // BLOCK
    pos_row = positions.reshape(t // BLOCK, 1, BLOCK)
    in_specs = [
        _rows(rows, D_MODEL),
        _resident((1, D_MODEL)),
        _resident((D_MODEL, QKV_COLS)),
        _resident((1, QKV_COLS)),
        _resident((MXU_DIM, MXU_DIM)),
        _resident((D_MODEL, D_MODEL)),
        pl.BlockSpec((rows, 1), lambda i: (i, 0)),
        pl.BlockSpec((None, 1, BLOCK), lambda i: (jnp.maximum(bps * i - 1, 0), 0, 0)),
        pl.BlockSpec((bps, 1, BLOCK), lambda i: (i, 0, 0)),
        pl.BlockSpec(memory_space=pltpu.SMEM),
    ]
    args = [h, mix_norm.reshape(1, D_MODEL).astype(F32), _prep_qkv_weight(w_qkv), head_gain, seg_ones,
            w_o.astype(BF16), positions.reshape(t, 1), pos_row, pos_row, sinks.astype(F32)]
    scratch = [pltpu.VMEM((rows, D_MODEL), BF16), pltpu.VMEM((rows, QKV_COLS), BF16),
               pltpu.VMEM((rows, D_MODEL), BF16), pltpu.VMEM((BLOCK, 2 * N_KV_HEADS * LANES), BF16)]
    (out,), casts = _call(_attn_layer_kernel, "attn_layer", (n_steps,), ("arbitrary",), in_specs, args,
                          [_rows(rows, D_MODEL)], [jax.ShapeDtypeStruct((t, D_MODEL), F32)], scratch,
                          side_casts)
    return out, casts


MLP_TF = 512


def _mlp_kernel(h_ref, gain_ref, wu_ref, wd_ref, o_ref, hn_ref):
    j = pl.program_id(1)

    @pl.when(j == 0)
    def _():
        x = h_ref[...]
        hn_ref[...] = _rms(x, gain_ref[...]).astype(BF16)
        o_ref[...] = x

    act = jnp.maximum(_dot(hn_ref[...], wu_ref[...]), 0.0)
    act = (act * act).astype(BF16)
    o_ref[...] += _dot(act, wd_ref[...])


def _mlp(h, gain, w_up16, w_down16, side_casts=()):
    t = h.shape[0]
    tm = min(ROW_BLOCK, t)
    tf = MLP_TF
    nj = D_FF // tf
    in_specs = [
        pl.BlockSpec((tm, D_MODEL), lambda i, j: (i, 0)),
        pl.BlockSpec((1, D_MODEL), lambda i, j: (0, 0)),
        pl.BlockSpec((D_MODEL, tf), lambda i, j: (0, j)),
        pl.BlockSpec((tf, D_MODEL), lambda i, j: (j, 0)),
    ]
    args = [h, gain.reshape(1, D_MODEL).astype(F32), w_up16, w_down16]
    (out,), casts = _call(_mlp_kernel, "mlp", (t // tm, nj), ("parallel", "arbitrary"), in_specs, args,
                          [pl.BlockSpec((tm, D_MODEL), lambda i, j: (i, 0))],
                          [jax.ShapeDtypeStruct((t, D_MODEL), F32)],
                          [pltpu.VMEM((tm, D_MODEL), BF16)], side_casts,
                          step_index=lambda i, j: i * nj + j)
    return out, casts


def _ple_epilogue(c, cols, accs, lhs_ref, extras, outs):
    p_ref, wple_ref = extras
    emb = _dot(p_ref[...].astype(BF16), wple_ref[:, cols])
    outs[0][:, cols] = lhs_ref[:, cols] + emb * _sigmoid(accs[0])


def _ple(h, gain, p_all, layer, w_ple, w_gate):
    t = h.shape[0]
    tm = min(PLE_ROWS, t)
    first = layer * (t // tm)
    (out,), _ = _row_matmul(
        "ple", t // tm, h, _rows(tm, D_MODEL), tm, w_gate, (0,), COL_CHUNK, D_MODEL // COL_CHUNK,
        extras=[(p_all, pl.BlockSpec((tm, PLE_DIM), lambda i: (first + i, 0))),
                (w_ple, _resident((PLE_DIM, D_MODEL)))],
        outs=[(jax.ShapeDtypeStruct((t, D_MODEL), F32), _rows(tm, D_MODEL))],
        epilogue=_ple_epilogue, gain=gain)
    return out


RGLRU_TB = 256
RGLRU_SEG = RGLRU_TB // SUBLANES
RGLRU_PITCH = 36
TINY = 1e-36


def _rglru_layer_kernel(h_ref, gain_ref, win_ref, cw_ref, cb_ref, wa_ref, ba_ref, wx_ref, bx_ref,
                        lam_ref, wo_ref, out_ref, hn_ref, xb_ref, gate_ref, y_ref, a_ref, b_ref,
                        carry_ref, tail_ref):
    tb = RGLRU_TB
    seg = RGLRU_SEG
    n_slabs = D_MODEL // LANES
    spb = GATE_BLOCK // LANES

    @pl.when(pl.program_id(0) == 0)
    def _():
        carry_ref[...] = jnp.zeros((1, D_MODEL), F32)
        tail_ref[...] = jnp.zeros((CONV_WIDTH - 1, D_MODEL), F32)

    hn_ref[...] = _rms(h_ref[...], gain_ref[...]).astype(BF16)
    lam = lam_ref[...]
    rate = (-LRU_C) * (jnp.log(1.0 + jnp.exp(-jnp.abs(lam))) + jnp.maximum(-lam, 0.0))

    def project(n):
        sl = slice(n * GATE_BLOCK, (n + 1) * GATE_BLOCK)
        xb_ref[:, sl] = _dot(hn_ref[...], win_ref[:, sl])
        gate_ref[:, sl] = _gelu_tanh(
            _dot(hn_ref[...], win_ref[:, D_MODEL + n * GATE_BLOCK:D_MODEL + (n + 1) * GATE_BLOCK])
        ).astype(BF16)

    def gates(n):
        sl = slice(n * GATE_BLOCK, (n + 1) * GATE_BLOCK)

        def shift1(v, k):
            prev_last = tail_ref[k:k + 1, sl]
            tail_ref[k:k + 1, sl] = v[tb - 1:tb, :]
            return jnp.concatenate([prev_last, v[:tb - 1, :]], axis=0)

        x = xb_ref[:, sl]
        acc = cw_ref[0:1, sl] * x
        for k in range(1, CONV_WIDTH):
            acc = cw_ref[k:k + 1, sl] * x + shift1(acc, k - 1)
        xn = acc + cb_ref[:, sl]
        xn16 = xn.astype(BF16)
        r = _sigmoid(_dot(xn16, wa_ref[n]) + ba_ref[:, sl])
        i_gate = _sigmoid(_dot(xn16, wx_ref[n]) + bx_ref[:, sl])
        a = jnp.exp(rate[:, sl] * r)
        v = 1.0 - a * a
        b = (v * lax.rsqrt(jnp.maximum(v, TINY))) * (i_gate * xn)
        for kk in range(spb):
            slab = n * spb + kk
            for s in range(SUBLANES):
                rows = slice(s * RGLRU_PITCH, s * RGLRU_PITCH + seg)
                a_ref[slab, rows, :] = a[s * seg:(s + 1) * seg, kk * LANES:(kk + 1) * LANES]
                b_ref[slab, rows, :] = b[s * seg:(s + 1) * seg, kk * LANES:(kk + 1) * LANES]

    project(0)
    for n in range(N_GATE_BLOCKS):
        if n + 1 < N_GATE_BLOCKS:
            project(n + 1)
        gates(n)

    def step(q, carry):
        hs, prods = carry
        rows = pl.ds(q, SUBLANES, stride=RGLRU_PITCH)
        new_hs = []
        new_prods = []
        for k in range(n_slabs):
            av = a_ref[k, rows, :]
            hk = av * hs[k] + b_ref[k, rows, :]
            pk = av * prods[k]
            b_ref[k, rows, :] = hk
            a_ref[k, rows, :] = pk
            new_hs.append(hk)
            new_prods.append(pk)
        return tuple(new_hs), tuple(new_prods)

    zeros = tuple(jnp.zeros((SUBLANES, LANES), F32) for _ in range(n_slabs))
    ones = tuple(jnp.ones((SUBLANES, LANES), F32) for _ in range(n_slabs))
    hs, prods = lax.fori_loop(0, seg, step, (zeros, ones))

    for k in range(n_slabs):
        lanes = slice(k * LANES, (k + 1) * LANES)
        cur = carry_ref[:, lanes]
        for s in range(SUBLANES):
            rows = slice(s * RGLRU_PITCH, s * RGLRU_PITCH + seg)
            hfull = b_ref[k, rows, :] + a_ref[k, rows, :] * cur
            out_rows = slice(s * seg, (s + 1) * seg)
            y_ref[out_rows, lanes] = (hfull * gate_ref[out_rows, lanes].astype(F32)).astype(BF16)
            cur = prods[k][s:s + 1, :] * cur + hs[k][s:s + 1, :]
        carry_ref[:, lanes] = cur

    for c in range(D_MODEL // COL_CHUNK):
        cols = slice(c * COL_CHUNK, (c + 1) * COL_CHUNK)
        out_ref[:, cols] = h_ref[:, cols] + _dot(y_ref[...], wo_ref[:, cols])


def _rglru_layer(h, mix_norm, w_in, conv_w, conv_b, w_gate_a, b_gate_a, w_gate_x, b_gate_x,
                 lru_lambda, w_o, side_casts=()):
    t = h.shape[0]
    tb = RGLRU_TB
    row = lambda v: v.reshape(1, D_MODEL).astype(F32)
    scan_scratch = pltpu.VMEM((D_MODEL // LANES, SUBLANES * RGLRU_PITCH, LANES), F32)
    in_specs = [
        _rows(tb, D_MODEL), _resident((1, D_MODEL)), _resident((D_MODEL, 2 * D_MODEL)),
        _resident((CONV_WIDTH, D_MODEL)), _resident((1, D_MODEL)),
        _resident((N_GATE_BLOCKS, GATE_BLOCK, GATE_BLOCK)), _resident((1, D_MODEL)),
        _resident((N_GATE_BLOCKS, GATE_BLOCK, GATE_BLOCK)), _resident((1, D_MODEL)),
        _resident((1, D_MODEL)), _resident((D_MODEL, D_MODEL)),
    ]
    args = [h, row(mix_norm), w_in.astype(BF16), conv_w.astype(F32), row(conv_b),
            w_gate_a.astype(BF16), row(b_gate_a), w_gate_x.astype(BF16), row(b_gate_x),
            row(lru_lambda), w_o.astype(BF16)]
    scratch = [pltpu.VMEM((tb, D_MODEL), BF16), pltpu.VMEM((tb, D_MODEL), F32),
               pltpu.VMEM((tb, D_MODEL), BF16), pltpu.VMEM((tb, D_MODEL), BF16),
               scan_scratch, scan_scratch,
               pltpu.VMEM((1, D_MODEL), F32), pltpu.VMEM((CONV_WIDTH - 1, D_MODEL), F32)]
    (out,), casts = _call(_rglru_layer_kernel, "rglru_layer", (t // tb,), ("arbitrary",), in_specs,
                          args, [_rows(tb, D_MODEL)], [jax.ShapeDtypeStruct((t, D_MODEL), F32)],
                          scratch, side_casts)
    return out, casts


S5_TM = 512
S5_IN_PITCH = 20
S5_GLU_PITCH = 36


def _s5_in_epilogue(c, cols, accs, lhs_ref, extras, outs, acc_ref):
    n = accs[0].shape[0] // S5_CHUNK
    for k in range(acc_ref.shape[0]):
        for ch in range(n):
            acc_ref[k, ch * S5_IN_PITCH:ch * S5_IN_PITCH + S5_CHUNK, :] = (
                accs[0][ch * S5_CHUNK:(ch + 1) * S5_CHUNK, k * LANES:(k + 1) * LANES])
        for l in range(S5_CHUNK):
            outs[0][l, :, cols.start + k * LANES:cols.start + (k + 1) * LANES] = (
                acc_ref[k, pl.ds(l, n, stride=S5_IN_PITCH), :])


def _s5_in(h, mix_norm, w_in, side_casts=()):
    t = h.shape[0]
    nc = t // S5_CHUNK
    tm = min(S5_TM, t)
    tn = COL_CHUNK
    (u,), casts = _row_matmul(
        "s5_in", t // tm, h, _rows(tm, D_MODEL), tm, w_in, (0,), tn, D_MODEL // tn,
        extras=[],
        outs=[(jax.ShapeDtypeStruct((S5_CHUNK, nc, D_MODEL), F32),
               pl.BlockSpec((S5_CHUNK, tm // S5_CHUNK, D_MODEL), lambda i: (0, i, 0)))],
        epilogue=_s5_in_epilogue, gain=mix_norm,
        scratch=[pltpu.VMEM((tn // LANES, (tm // S5_CHUNK) * S5_IN_PITCH, LANES), F32)],
        side_casts=side_casts)
    return u, casts


def _s5_ssm_kernel(u_ref, lr_ref, li_ref, ldt_ref, bre_ref, bim_ref, cre_ref, cim_ref, dsk_ref,
                   g_ref, tbig_ref, wend_ref, wc_ref, x_ref, e_ref, s_ref, y_ref):
    nc = u_ref.shape[1]
    L = S5_CHUNK
    P = S5_STATE_COLS
    lr = lr_ref[...]
    li = li_ref[...]
    dt = jnp.exp(ldt_ref[...])
    mag = jnp.exp(lr * dt)
    abar_re = mag * jnp.cos(li * dt)
    abar_im = mag * jnp.sin(li * dt)
    n_re = abar_re - 1.0
    n_im = abar_im
    den = lr * lr + li * li
    z_re = (n_re * lr + n_im * li) / den
    z_im = (n_im * lr - n_re * li) / den
    bre = bre_ref[...]
    bim = bim_ref[...]
    bbar_re = z_re * bre - z_im * bim
    bbar_im = z_re * bim + z_im * bre
    cre = cre_ref[...]
    cim = cim_ref[...]

    def power(d):
        m = jnp.exp((d * dt) * lr)
        ang = (d * dt) * li
        return m * jnp.cos(ang), m * jnp.sin(ang)

    def w_in_state(d):
        pr, pi = power(d)
        return jnp.concatenate([pr * bbar_re - pi * bbar_im, pr * bbar_im + pi * bbar_re],
                               axis=1).astype(BF16)

    def w_out_state(d):
        pr, pi = power(d)
        return jnp.concatenate([cre * pr - cim * pi, -(cre * pi + cim * pr)], axis=1).astype(BF16)

    wt0 = w_out_state(0)
    for d in range(L):
        wj = w_in_state(d)
        m_d = _dot_nt(wj, wt0).astype(BF16)
        for jj in range(L - d):
            tt = jj + d
            tbig_ref[jj * LANES:(jj + 1) * LANES, tt * LANES:(tt + 1) * LANES] = m_d
        wend_ref[(L - 1 - d) * LANES:(L - d) * LANES, :] = wj
        wc_ref[d * LANES:(d + 1) * LANES, :] = w_out_state(d + 1)
    for tt in range(0, L, 2):
        tbig_ref[(tt + 1) * LANES:(tt + 2) * LANES, tt * LANES:(tt + 1) * LANES] = (
            jnp.zeros((LANES, LANES), BF16))

    x_ref[...] = jnp.concatenate([u_ref[l] for l in range(L)], axis=1).astype(BF16)
    e_ref[...] = _dot(x_ref[...], wend_ref[...])

    row = lax.broadcasted_iota(jnp.int32, (SUBLANES, P), 0)
    steps = []
    for s in (1, 2, 4):
        pr, pi = power(L * s)
        steps.append((s, row >= s, pr, pi))
    carry_pw = [power(L * (r + 1)) for r in range(SUBLANES)]
    cpw_re = jnp.concatenate([c[0] for c in carry_pw], axis=0)
    cpw_im = jnp.concatenate([c[1] for c in carry_pw], axis=0)

    s_ref[0:SUBLANES, :] = jnp.zeros((SUBLANES, 2 * P), F32)

    def tile_step(i, carry):
        s_re, s_im = carry
        base = i * SUBLANES
        e = e_ref[base:base + SUBLANES, :]
        t_re = e[:, :P]
        t_im = e[:, P:]
        for s, keep, pr, pi in steps:
            sh_re = jnp.where(keep, pltpu.roll(t_re, s, 0), 0.0)
            sh_im = jnp.where(keep, pltpu.roll(t_im, s, 0), 0.0)
            t_re, t_im = t_re + pr * sh_re - pi * sh_im, t_im + pr * sh_im + pi * sh_re
        f_re = t_re + cpw_re * s_re - cpw_im * s_im
        f_im = t_im + cpw_re * s_im + cpw_im * s_re
        s_ref[base + SUBLANES:base + 2 * SUBLANES, :] = jnp.concatenate([f_re, f_im], axis=1)
        return f_re[SUBLANES - 1:SUBLANES, :], f_im[SUBLANES - 1:SUBLANES, :]

    n_tiles = nc // SUBLANES
    tiles_per_block = n_tiles // (L // 2)
    carry = (jnp.zeros((1, P), F32), jnp.zeros((1, P), F32))
    for b in range(L // 2):
        k_rows = (2 * b + 2) * LANES
        cols = slice(2 * b * LANES, (2 * b + 2) * LANES)
        y_ref[:, cols] = _dot(x_ref[:, :k_rows], tbig_ref[:k_rows, cols])
        for i in range(b * tiles_per_block, (b + 1) * tiles_per_block):
            carry = tile_step(i, carry)

    s_start = s_ref[SUBLANES - 1:SUBLANES - 1 + nc, :].astype(BF16)
    dsk = dsk_ref[...]
    for b in range(L // 2):
        cols = slice(2 * b * LANES, (2 * b + 2) * LANES)
        y = y_ref[:, cols] + _dot_nt(s_start, wc_ref[cols, :])
        for l in (2 * b, 2 * b + 1):
            yl = y[:, (l - 2 * b) * LANES:(l - 2 * b + 1) * LANES] + dsk * u_ref[l]
            g_ref[l] = _gelu_tanh(yl).astype(BF16)


def _s5_ssm(u, a_re, a_im, log_dt, b_re, b_im, c_re, c_im, d_skip, side_casts=()):
    L, nc, _ = u.shape
    P = S5_STATE_COLS
    gpt = S5_GROUPS_PER_TILE
    nt = N_LANE_TILES
    f = lambda v: v.astype(F32)
    row = lambda v: f(v).reshape(nt, 1, P)
    ldt = jnp.broadcast_to(f(log_dt)[:, None], (N_SSM_GROUPS, SSM_STATE))
    same_group = (jnp.arange(LANES)[:, None] // SSM_GROUP) == (jnp.arange(P)[None, :] // SSM_STATE)

    def block_diag(x):
        return jnp.where(same_group, jnp.tile(x, (1, 1, gpt)), 0.0)

    def b_layout(b):
        bt = f(b).reshape(nt, gpt, SSM_STATE, SSM_GROUP).transpose(0, 1, 3, 2)
        return block_diag(bt.reshape(nt, LANES, SSM_STATE))

    def c_layout(c):
        return block_diag(f(c).reshape(nt, LANES, SSM_STATE))

    tile_row = pl.BlockSpec((None, 1, P), lambda k: (k, 0, 0))
    tile_mat = pl.BlockSpec((None, LANES, P), lambda k: (k, 0, 0))
    in_specs = [pl.BlockSpec((L, nc, LANES), lambda k: (0, 0, k)),
                tile_row, tile_row, tile_row, tile_mat, tile_mat, tile_mat, tile_mat,
                pl.BlockSpec((1, LANES), lambda k: (0, k))]
    args = [u, row(a_re), row(a_im), row(ldt), b_layout(b_re), b_layout(b_im), c_layout(c_re),
            c_layout(c_im), f(d_skip).reshape(1, D_MODEL)]
    scratch = [pltpu.VMEM((L * LANES, L * LANES), BF16),
               pltpu.VMEM((L * LANES, 2 * P), BF16),
               pltpu.VMEM((L * LANES, 2 * P), BF16),
               pltpu.VMEM((nc, L * LANES), BF16),
               pltpu.VMEM((nc, 2 * P), F32),
               pltpu.VMEM((nc + SUBLANES, 2 * P), F32),
               pltpu.VMEM((nc, L * LANES), F32)]
    (g,), casts = _call(_s5_ssm_kernel, "s5_ssm", (nt,), ("arbitrary",), in_specs, args,
                        [pl.BlockSpec((L, nc, LANES), lambda k: (0, 0, k))],
                        [jax.ShapeDtypeStruct((L, nc, D_MODEL), BF16)], scratch, side_casts)
    return g, casts


def _glu_epilogue(c, cols, accs, lhs_ref, extras, outs, y_ref):
    y = accs[0] * _sigmoid(accs[1])
    n = y.shape[0] // S5_CHUNK
    assert n <= S5_GLU_PITCH
    for k in range(y_ref.shape[0]):
        for l in range(S5_CHUNK):
            y_ref[k, l * S5_GLU_PITCH:l * S5_GLU_PITCH + n, :] = (
                y[l * n:(l + 1) * n, k * LANES:(k + 1) * LANES])
        lanes = slice(cols.start + k * LANES, cols.start + (k + 1) * LANES)
        for ch in range(n):
            rows = slice(ch * S5_CHUNK, (ch + 1) * S5_CHUNK)
            outs[0][rows, lanes] = (extras[0][rows, lanes]
                                    + y_ref[k, pl.ds(ch, S5_CHUNK, stride=S5_GLU_PITCH), :])


def _s5_glu(g, w_glu, h, side_casts=()):
    L, nc, _ = g.shape
    t = nc * L
    tm = min(S5_TM, t)
    (out,), casts = _row_matmul(
        "s5_glu", t // tm, g, pl.BlockSpec((L, tm // L, D_MODEL), lambda i: (0, i, 0)), tm, w_glu,
        (0, D_MODEL), COL_CHUNK, D_MODEL // COL_CHUNK,
        extras=[(h, _rows(tm, D_MODEL))],
        outs=[(jax.ShapeDtypeStruct((t, D_MODEL), F32), _rows(tm, D_MODEL))],
        epilogue=_glu_epilogue,
        scratch=[pltpu.VMEM((COL_CHUNK // LANES, S5_CHUNK * S5_GLU_PITCH, LANES), F32)],
        side_casts=side_casts)
    return out, casts


def _s5_layer(h, mix_norm, w_in, a_re, a_im, log_dt, b_re, b_im, c_re, c_im, d_skip, w_glu,
              side_casts=((), (), ())):
    u, casts_in = _s5_in(h, mix_norm, w_in.astype(BF16), side_casts[0])
    g, casts_ssm = _s5_ssm(u, a_re, a_im, log_dt, b_re, b_im, c_re, c_im, d_skip, side_casts[1])
    out, casts_glu = _s5_glu(g, w_glu.astype(BF16), h, side_casts[2])
    return out, (casts_in, casts_ssm, casts_glu)


def kernel(x, p, positions,
           l0_mix_norm, l0_w_qkv, l0_q_norm, l0_k_norm, l0_sinks, l0_w_o,
           l0_mlp_norm, l0_w_up, l0_w_down, l0_ple_norm, l0_w_ple_gate, l0_w_ple,
           l1_mix_norm, l1_w_in, l1_conv_w, l1_conv_b, l1_w_gate_a, l1_b_gate_a, l1_w_gate_x,
           l1_b_gate_x, l1_lru_lambda, l1_w_o,
           l1_mlp_norm, l1_w_up, l1_w_down, l1_ple_norm, l1_w_ple_gate, l1_w_ple,
           l2_mix_norm, l2_w_in, l2_a_re, l2_a_im, l2_log_dt, l2_b_re, l2_b_im, l2_c_re, l2_c_im,
           l2_d_skip, l2_w_glu,
           l2_mlp_norm, l2_w_up, l2_w_down, l2_ple_norm, l2_w_ple_gate, l2_w_ple,
           l3_mix_norm, l3_w_qkv, l3_q_norm, l3_k_norm, l3_sinks, l3_w_o,
           l3_mlp_norm, l3_w_up, l3_w_down, l3_ple_norm, l3_w_ple_gate, l3_w_ple):
    batch, t, _ = x.shape
    assert batch == 1 and t % (8 * BLOCK) == 0
    pos = positions.reshape(t).astype(jnp.int32)
    tails = [
        (l0_mlp_norm, l0_w_up, l0_w_down, l0_ple_norm, l0_w_ple_gate, l0_w_ple),
        (l1_mlp_norm, l1_w_up, l1_w_down, l1_ple_norm, l1_w_ple_gate, l1_w_ple),
        (l2_mlp_norm, l2_w_up, l2_w_down, l2_ple_norm, l2_w_ple_gate, l2_w_ple),
        (l3_mlp_norm, l3_w_up, l3_w_down, l3_ple_norm, l3_w_ple_gate, l3_w_ple),
    ]
    h = x.reshape(t, D_MODEL).astype(F32)
    p_all = p.reshape(DEPTH * t, PLE_DIM)
    for i in range(DEPTH):
        if i == 0:
            _, (l0_w_qkv16, l0_w_o16) = _call(
                lambda: None, "weight_prep", (PREP_STEPS,), ("parallel",), [], [], [], [],
                side_casts=(l0_w_qkv, l0_w_o))
            h, (up16, down16, gate16, l1_w_in16, l1_w_o16) = _attention_layer(
                h, pos, l0_mix_norm, l0_w_qkv16, l0_q_norm, l0_k_norm, l0_sinks, l0_w_o16,
                side_casts=(l0_w_up, l0_w_down, l0_w_ple_gate, l1_w_in, l1_w_o))
        elif i == 1:
            h, (gate16, l2_w_in16, l2_w_glu16, l2_gate16) = _rglru_layer(
                h, l1_mix_norm, l1_w_in16, l1_conv_w, l1_conv_b, l1_w_gate_a, l1_b_gate_a,
                l1_w_gate_x, l1_b_gate_x, l1_lru_lambda, l1_w_o16,
                side_casts=(l1_w_ple_gate, l2_w_in, l2_w_glu, l2_w_ple_gate))
        elif i == 2:
            gate16 = l2_gate16
            h, (_, l3_mlp16, (l3_w_qkv16, l3_w_o16, l3_gate16)) = _s5_layer(
                h, l2_mix_norm, l2_w_in16, l2_a_re, l2_a_im, l2_log_dt, l2_b_re, l2_b_im,
                l2_c_re, l2_c_im, l2_d_skip, l2_w_glu16,
                side_casts=((), (l3_w_up, l3_w_down), (l3_w_qkv, l3_w_o, l3_w_ple_gate)))
        else:
            gate16 = l3_gate16
            up16, down16 = l3_mlp16
            h, _ = _attention_layer(h, pos, l3_mix_norm, l3_w_qkv16, l3_q_norm, l3_k_norm, l3_sinks,
                                    l3_w_o16)
        mlp_norm, _, _, ple_norm, _, w_ple = tails[i]
        next_mlp = {0: (l1_w_up, l1_w_down), 1: (l2_w_up, l2_w_down)}.get(i, ())
        h, next16 = _mlp(h, mlp_norm, up16, down16, side_casts=next_mlp)
        if next16:
            up16, down16 = next16
        h = _ple(h, ple_norm, p_all, i, w_ple.astype(BF16), gate16)
    return h.reshape(batch, t, D_MODEL).astype(x.dtype)
```
